```python
import math
import jax, jax.numpy as jnp
from jax import lax
import numpy as np

D_MODEL = 1024
BATCH = 16
SEQ = 2048
DEPTH = 2

GRID_W = 64
N_GROUPS = 4
GROUP_W = D_MODEL // N_GROUPS
MIX_W = N_GROUPS * GROUP_W
SSD_HEAD_DIM = 64
SSD_HEADS = GROUP_W // SSD_HEAD_DIM
SSD_BC_GROUPS = 2
SSD_STATE = 128
SSD_CONV = 3
SSD_CHUNK = 128
SSD_XBC = GROUP_W + 2 * SSD_BC_GROUPS * SSD_STATE
SSD_IN = GROUP_W + SSD_XBC + 2 * SSD_HEADS
NA_HEAD_DIM = 64
NA_HEADS = GROUP_W // NA_HEAD_DIM
NA_KH = 8
NA_KW = 16
NA_QB = 16
NA_KSPAN = 32
NA_IN = 3 * GROUP_W
ML_HEAD_DIM = 64
ML_HEADS = GROUP_W // ML_HEAD_DIM
ML_CHUNK = 128
ML_IN = 4 * GROUP_W + 4 * ML_HEADS
HY_CH = GROUP_W
HY_ORDER = 2
HY_CONV = 3
HY_EMB = 33
HY_FFN = 64
HY_SHORT_DECAY_FRAC = 0.3
HY_LONG_DECAY_FRAC = 1.5
HY_DECAY_TARGET = 1e-2
HY_IN = 3 * HY_CH
OFF_NA = SSD_IN
OFF_ML = OFF_NA + NA_IN
OFF_HY = OFF_ML + ML_IN
P_IN = OFF_HY + HY_IN
FFN_HIDDEN = int(math.ceil(8 * D_MODEL / 3 / 256)) * 256
NORM_EPS = 1e-6

kernel_name = "hybrid_parallel_mixer_encoder"

F32 = jnp.float32


def rms_norm(x, gain=None):
    xf = x.astype(F32)
    y = xf * lax.rsqrt(jnp.mean(xf * xf, axis=-1, keepdims=True) + NORM_EPS)
    if gain is not None:
        y = y * gain.astype(F32)
    return y.astype(x.dtype)


def depthwise_conv(x, w, b):
    K = w.shape[0]
    y = lax.conv_general_dilated(x, w[:, None, :].astype(x.dtype), window_strides=(1,),
                                 padding=[(K // 2, K // 2)],
                                 dimension_numbers=('NWC', 'WIO', 'NWC'),
                                 feature_group_count=x.shape[-1])
    return y + b.astype(x.dtype)


def ssd_chunk_scan(x, dt, A, Bm, Cm):
    b, L, H, P = x.shape
    N = Bm.shape[-1]
    T = SSD_CHUNK
    nc = L // T
    x = x.reshape(b, nc, T, H, P)
    Bm = Bm.reshape(b, nc, T, H, N)
    Cm = Cm.reshape(b, nc, T, H, N)
    dt = dt.reshape(b, nc, T, H)
    a_cum = jnp.cumsum(jnp.moveaxis(dt * A, -1, 1), axis=-1)
    xdt = x * dt[..., None]
    lower = jnp.tril(jnp.ones((T, T), bool))
    decay = jnp.exp(jnp.where(lower, a_cum[..., :, None] - a_cum[..., None, :], -jnp.inf))
    scores = jnp.einsum('bcthn,bcshn->bhcts', Cm, Bm) * decay
    y_diag = jnp.einsum('bhcts,bcshp->bcthp', scores, xdt)
    decay_to_end = jnp.exp(a_cum[..., -1:] - a_cum)
    chunk_states = jnp.einsum('bcshn,bhcs,bcshp->bchpn', Bm, decay_to_end, xdt)
    chunk_decay = jnp.exp(a_cum[..., -1])

    def step(state, inp):
        s_c, d_c = inp
        return state * d_c[..., None, None] + s_c, state

    _, prev = lax.scan(step, jnp.zeros((b, H, P, N), x.dtype),
                       (jnp.moveaxis(chunk_states, 1, 0), jnp.moveaxis(chunk_decay, 2, 0)))
    prev = jnp.moveaxis(prev, 0, 1)
    y_off = jnp.einsum('bcthn,bchpn,bhct->bcthp', Cm, prev, jnp.exp(a_cum))
    return (y_diag + y_off).reshape(b, L, H, P)


def ssd_mixer(u, conv_w, conv_b, dt_bias, a_log, d_skip, norm_g):
    b, L, _ = u.shape
    z = u[..., :GROUP_W]
    xbc = jax.nn.silu(depthwise_conv(u[..., GROUP_W:GROUP_W + SSD_XBC], conv_w, conv_b)).astype(F32)
    dt_raw = u[..., GROUP_W + SSD_XBC:].astype(F32).reshape(b, L, 2, SSD_HEADS)
    xs = xbc[..., :GROUP_W].reshape(b, L, SSD_HEADS, SSD_HEAD_DIM)
    rep = SSD_HEADS // SSD_BC_GROUPS
    nb = SSD_BC_GROUPS * SSD_STATE
    Bm = jnp.repeat(xbc[..., GROUP_W:GROUP_W + nb].reshape(b, L, SSD_BC_GROUPS, SSD_STATE), rep, axis=2)
    Cm = jnp.repeat(xbc[..., GROUP_W + nb:].reshape(b, L, SSD_BC_GROUPS, SSD_STATE), rep, axis=2)
    dt = jax.nn.softplus(dt_raw + dt_bias.astype(F32))
    A = -jnp.exp(a_log.astype(F32))
    flip = lambda t: jnp.flip(t, axis=1)
    y_fwd = ssd_chunk_scan(xs, dt[:, :, 0], A[0], Bm, Cm)
    y_bwd = flip(ssd_chunk_scan(flip(xs), flip(dt[:, :, 1]), A[1], flip(Bm), flip(Cm)))
    y = y_fwd + y_bwd + d_skip.astype(F32)[:, None] * xs
    y = y.reshape(b, L, GROUP_W) * jax.nn.silu(z.astype(F32))
    return rms_norm(y, norm_g).astype(u.dtype)


def neighborhood_attention(u, rpb):
    b, L, _ = u.shape
    rows = L // GRID_W
    kh = min(NA_KH, rows)
    q, k, v = [t.reshape(b, rows, GRID_W, NA_HEADS, NA_HEAD_DIM).transpose(0, 3, 1, 2, 4)
               for t in jnp.split(u, 3, axis=-1)]
    r = jnp.arange(rows)
    row_idx = jnp.clip(r - kh // 2, 0, rows - kh)[:, None] + jnp.arange(kh)[None, :]
    dr_idx = row_idx - r[:, None] + (NA_KH - 1)
    scale = NA_HEAD_DIM ** -0.5

    def col_block(j0):
        ks0 = jnp.clip(j0 - NA_KW // 2, 0, GRID_W - NA_KSPAN)
        qb = lax.dynamic_slice_in_dim(q, j0, NA_QB, axis=3)
        kb = lax.dynamic_slice_in_dim(k, ks0, NA_KSPAN, axis=3)[:, :, row_idx]
        vb = lax.dynamic_slice_in_dim(v, ks0, NA_KSPAN, axis=3)[:, :, row_idx]
        j = j0 + jnp.arange(NA_QB)
        cs = jnp.clip(j - NA_KW // 2, 0, GRID_W - NA_KW)
        kc = ks0 + jnp.arange(NA_KSPAN)
        valid = (kc[None, :] >= cs[:, None]) & (kc[None, :] < cs[:, None] + NA_KW)
        dc_idx = jnp.clip(kc[None, :] - j[:, None] + (NA_KW - 1), 0, 2 * NA_KW - 2)
        bias = rpb[:, dr_idx[:, None, :, None], dc_idx[None, :, None, :]]
        s = jnp.einsum('bhrqd,bhrkud->bhrqku', qb, kb).astype(F32) * scale + bias.astype(F32)
        s = jnp.where(valid[:, None, :], s, -jnp.inf)
        p = jax.nn.softmax(s.reshape(s.shape[:4] + (kh * NA_KSPAN,)), axis=-1)
        p = p.reshape(s.shape).astype(u.dtype)
        return jnp.einsum('bhrqku,bhrkud->bhrqd', p, vb)

    o = lax.map(col_block, jnp.arange(0, GRID_W, NA_QB))
    return o.transpose(1, 3, 0, 4, 2, 5).reshape(b, L, GROUP_W)


def mlstm_scan(q, k, v, log_i, log_f):
    b, H, L, d = q.shape
    T = ML_CHUNK
    nc = L // T
    to_chunks = lambda t: jnp.moveaxis(t.reshape(t.shape[:2] + (nc, T) + t.shape[3:]), 2, 0)
    lower = jnp.tril(jnp.ones((T, T), bool))

    def step(carry, inp):
        C, n, m = carry
        qc, kc, vc, li, lf = inp
        bcum = jnp.cumsum(lf, axis=-1)
        g_inter = bcum + m[..., None]
        g_intra = jnp.where(lower, bcum[..., :, None] - bcum[..., None, :] + li[..., None, :], -jnp.inf)
        m_t = jnp.maximum(g_inter, jnp.max(g_intra, axis=-1))
        w_inter = jnp.exp(g_inter - m_t)
        s = jnp.einsum('bhtd,bhsd->bhts', qc, kc) * jnp.exp(g_intra - m_t[..., None])
        num = w_inter[..., None] * jnp.einsum('bhtd,bhde->bhte', qc, C) + jnp.einsum('bhts,bhse->bhte', s, vc)
        den = w_inter * jnp.einsum('bhtd,bhd->bht', qc, n) + jnp.sum(s, axis=-1)
        h = num / jnp.maximum(jnp.abs(den), jnp.exp(-m_t))[..., None]
        g_state = bcum[..., -1] + m
        g_src = bcum[..., -1:] - bcum + li
        m_new = jnp.maximum(g_state, jnp.max(g_src, axis=-1))
        w_src = jnp.exp(g_src - m_new[..., None])
        carry_decay = jnp.exp(g_state - m_new)
        C_new = carry_decay[..., None, None] * C + jnp.einsum('bhs,bhsd,bhse->bhde', w_src, kc, vc)
        n_new = carry_decay[..., None] * n + jnp.einsum('bhs,bhsd->bhd', w_src, kc)
        return (C_new, n_new, m_new), h

    init = (jnp.zeros((b, H, d, d), F32), jnp.zeros((b, H, d), F32), jnp.zeros((b, H), F32))
    _, h = lax.scan(step, init, (to_chunks(q), to_chunks(k), to_chunks(v), to_chunks(log_i), to_chunks(log_f)))
    return jnp.moveaxis(h, 0, 2).reshape(b, H, L, d)


def mlstm_mixer(u, i_bias, f_bias, norm_g):
    b, L, _ = u.shape
    heads = lambda t: t.astype(F32).reshape(b, L, ML_HEADS, ML_HEAD_DIM).transpose(0, 2, 1, 3)
    q = heads(u[..., :GROUP_W])
    k = heads(u[..., GROUP_W:2 * GROUP_W]) * (ML_HEAD_DIM ** -0.5)
    v = heads(u[..., 2 * GROUP_W:3 * GROUP_W])
    o = u[..., 3 * GROUP_W:4 * GROUP_W].astype(F32)
    gates = u[..., 4 * GROUP_W:].astype(F32).reshape(b, L, 2, 2, ML_HEADS).transpose(2, 3, 0, 4, 1)
    log_i = gates[0] + i_bias.astype(F32)[:, None, :, None]
    log_f = jax.nn.log_sigmoid(gates[1] + f_bias.astype(F32)[:, None, :, None])
    flip = lambda t: jnp.flip(t, axis=2)
    h_fwd = mlstm_scan(q, k, v, log_i[0], log_f[0])
    h_bwd = flip(mlstm_scan(flip(q), flip(k), flip(v), flip(log_i[1]), flip(log_f[1])))
    h = rms_norm(h_fwd + h_bwd, norm_g.reshape(ML_HEADS, 1, ML_HEAD_DIM))
    h = h.transpose(0, 2, 1, 3).reshape(b, L, GROUP_W)
    return (jax.nn.sigmoid(o) * h).astype(u.dtype)


def hyena_filters(L, w1, b1, w2, b2, w3, freq):
    t = jnp.linspace(0.0, 1.0, L, dtype=F32)[:, None]
    bands = (HY_EMB - 1) // 2
    f = jnp.linspace(1e-4, bands - 1, bands, dtype=F32)
    ang = (2.0 * math.pi) * (jnp.arange(L, dtype=F32) / L)[:, None] * f[None, :]
    feats = jnp.concatenate([t, jnp.cos(ang), -jnp.sin(ang)], axis=-1)
    freq = freq.astype(F32)
    h = jnp.sin(freq[0] * (feats @ w1.astype(F32) + b1.astype(F32)))
    h = jnp.sin(freq[1] * (h @ w2.astype(F32) + b2.astype(F32)))
    h = (h @ w3.astype(F32)).reshape(L, HY_ORDER, 2, HY_CH)
    deltas = jnp.abs(jnp.linspace(math.log(HY_DECAY_TARGET) / HY_LONG_DECAY_FRAC,
                                  math.log(HY_DECAY_TARGET) / HY_SHORT_DECAY_FRAC, HY_CH, dtype=F32))
    h = h * jnp.exp(-t * deltas[None, :])[:, None, None, :]
    return h * lax.rsqrt(jnp.sum(h * h, axis=(0, 2), keepdims=True) + NORM_EPS)


def two_sided_fft_conv(z, h_fwd, h_bwd):
    L = z.shape[1]
    k = jnp.concatenate([h_fwd, jnp.zeros_like(h_fwd[:1]), jnp.flip(h_bwd[1:], axis=0)], axis=0)
    zf = jnp.fft.rfft(z, n=2 * L, axis=1)
    kf = jnp.fft.rfft(k, n=2 * L, axis=0)
    return jnp.fft.irfft(zf * kf[None], n=2 * L, axis=1)[:, :L]


def hyena_mixer(u, conv_w, conv_b, w1, b1, w2, b2, w3, freq, skip, norm_g):
    L = u.shape[1]
    uc = depthwise_conv(u, conv_w, conv_b).astype(F32)
    v, x1, x2 = jnp.split(uc, 3, axis=-1)
    h = hyena_filters(L, w1, b1, w2, b2, w3, freq)
    skip = skip.astype(F32)
    z = v
    for o, gate in enumerate((x1, x2)):
        z = gate * (two_sided_fft_conv(z, h[:, o, 0], h[:, o, 1]) + skip[o] * z)
    return rms_norm(z, norm_g).astype(u.dtype)


def setup_inputs(seed: int = 0) -> dict:
    key = jax.random.key(seed)
    ks = iter(jax.random.split(key, 40))
    nrm = lambda shape, s: jax.random.normal(next(ks), shape, F32) * s
    gain = lambda shape: 1.0 + nrm(shape, 0.02)
    Lr = DEPTH
    dt0 = jnp.exp(jax.random.uniform(next(ks), (Lr, 2, SSD_HEADS), F32, math.log(1e-3), math.log(1e-1)))
    return {
        "x": nrm((BATCH, SEQ, D_MODEL), 1.0),
        "c": nrm((BATCH, D_MODEL), 1.0),
        "mod_w": nrm((Lr, D_MODEL, 6 * D_MODEL), D_MODEL ** -0.5),
        "mod_b": nrm((Lr, 6 * D_MODEL), 0.02),
        "w_in": nrm((Lr, D_MODEL, P_IN), D_MODEL ** -0.5),
        "ssd_conv_w": nrm((Lr, SSD_CONV, SSD_XBC), SSD_CONV ** -0.5),
        "ssd_conv_b": nrm((Lr, SSD_XBC), 0.02),
        "ssd_dt_bias": dt0 + jnp.log(-jnp.expm1(-dt0)),
        "ssd_a_log": jnp.log(jax.random.uniform(next(ks), (Lr, 2, SSD_HEADS), F32, 1.0, 16.0)),
        "ssd_d": gain((Lr, SSD_HEADS)),
        "ssd_norm_g": gain((Lr, GROUP_W)),
        "na_rpb": nrm((Lr, NA_HEADS, 2 * NA_KH - 1, 2 * NA_KW - 1), 0.02),
        "na_norm_g": gain((Lr, GROUP_W)),
        "ml_i_bias": nrm((Lr, 2, ML_HEADS), 0.1),
        "ml_f_bias": jnp.linspace(3.0, 6.0, ML_HEADS, dtype=F32)[None, None, :] + nrm((Lr, 2, ML_HEADS), 0.1),
        "ml_norm_g": gain((Lr, GROUP_W)),
        "hy_conv_w": nrm((Lr, HY_CONV, HY_IN), HY_CONV ** -0.5),
        "hy_conv_b": nrm((Lr, HY_IN), 0.02),
        "hy_w1": nrm((Lr, HY_EMB, HY_FFN), HY_EMB ** -0.5),
        "hy_b1": nrm((Lr, HY_FFN), 0.1),
        "hy_w2": nrm((Lr, HY_FFN, HY_FFN), HY_FFN ** -0.5),
        "hy_b2": nrm((Lr, HY_FFN), 0.1),
        "hy_w3": nrm((Lr, HY_FFN, HY_ORDER * 2 * HY_CH), HY_FFN ** -0.5),
        "hy_freq": 1.0 + nrm((Lr, 2, HY_FFN), 0.1),
        "hy_skip": nrm((Lr, HY_ORDER, HY_CH), 0.1),
        "hy_norm_g": gain((Lr, GROUP_W)),
        "w_out": nrm((Lr, MIX_W, D_MODEL), MIX_W ** -0.5),
        "ffn_w_gate": nrm((Lr, D_MODEL, FFN_HIDDEN), D_MODEL ** -0.5),
        "ffn_w_up": nrm((Lr, D_MODEL, FFN_HIDDEN), D_MODEL ** -0.5),
        "ffn_w_down": nrm((Lr, FFN_HIDDEN, D_MODEL), FFN_HIDDEN ** -0.5),
        "final_norm_g": gain((D_MODEL,)),
    }


def reference(x, c, mod_w, mod_b, w_in, ssd_conv_w, ssd_conv_b, ssd_dt_bias, ssd_a_log, ssd_d,
              ssd_norm_g, na_rpb, na_norm_g, ml_i_bias, ml_f_bias, ml_norm_g, hy_conv_w, hy_conv_b,
              hy_w1, hy_b1, hy_w2, hy_b2, hy_w3, hy_freq, hy_skip, hy_norm_g, w_out,
              ffn_w_gate, ffn_w_up, ffn_w_down, final_norm_g):
    cond = jax.nn.silu(c)
    for l in range(DEPTH):
        mod = cond @ mod_w[l] + mod_b[l]
        sh1, sc1, g1, sh2, sc2, g2 = [m[:, None, :] for m in jnp.split(mod, 6, axis=-1)]
        h = rms_norm(x) * (1 + sc1) + sh1
        u = h @ w_in[l]
        y_ssd = ssd_mixer(u[..., :OFF_NA], ssd_conv_w[l], ssd_conv_b[l], ssd_dt_bias[l],
                          ssd_a_log[l], ssd_d[l], ssd_norm_g[l])
        y_na = rms_norm(neighborhood_attention(u[..., OFF_NA:OFF_ML], na_rpb[l]), na_norm_g[l])
        y_ml = mlstm_mixer(u[..., OFF_ML:OFF_HY], ml_i_bias[l], ml_f_bias[l], ml_norm_g[l])
        y_hy = hyena_mixer(u[..., OFF_HY:], hy_conv_w[l], hy_conv_b[l], hy_w1[l], hy_b1[l],
                           hy_w2[l], hy_b2[l], hy_w3[l], hy_freq[l], hy_skip[l], hy_norm_g[l])
        y = jnp.concatenate([y_ssd, y_na, y_ml, y_hy], axis=-1) @ w_out[l]
        x = x + g1 * y
        h = rms_norm(x) * (1 + sc2) + sh2
        x = x + g2 * ((jax.nn.silu(h @ ffn_w_gate[l]) * (h @ ffn_w_up[l])) @ ffn_w_down[l])
    return rms_norm(x, final_norm_g)
```

```python
import functools
import math

import numpy as np
import jax
import jax.numpy as jnp
from jax import lax
from jax.experimental import pallas as pl
from jax.experimental.pallas import tpu as pltpu

F32 = jnp.float32
BF16 = jnp.bfloat16

D_MODEL = 1024
SEQ = 2048
GRID_W = 64
GROUP_W = 256
HEAD_DIM = 64
N_HEADS = 4
SSD_STATE = 128
SSD_XBC = 768
CHUNK = 128
N_CHUNKS = SEQ // CHUNK
NA_KH = 8
NA_KW = 16
NA_QROWS = 4
NA_KROWS = 12
HY_EMB = 33
HY_FFN = 64
FFN_HIDDEN = 2816
NORM_EPS = 1e-6
FFT_N = 2 * SEQ
FREQ_BLOCK = 512
N_FREQ_BLOCKS = SEQ // FREQ_BLOCK
HY_BATCH_BLOCK = 2
TOKEN_TILE = 512
VMEM_LIMIT = 56 * 1024 * 1024

COL_ZX = 0
COL_NA = 1024
COL_MLQ = 1792
COL_MLO = 2560
COL_HY = 2816
COL_SM = 3584
P_PAD = 3712
SM_DT = 0
SM_IG = 8
SM_FG = 16

NT_DIMS = (((1,), (1,)), ((), ()))
TN_DIMS = (((0,), (0,)), ((), ()))


def _rms(x):
    return x * lax.rsqrt(jnp.mean(x * x, axis=-1, keepdims=True) + NORM_EPS)


def _silu(x):
    return x * jax.nn.sigmoid(x)


def _dot(a, b):
    return jnp.dot(a, b, preferred_element_type=F32)


def _params(*sem):
    return pltpu.CompilerParams(dimension_semantics=sem, vmem_limit_bytes=VMEM_LIMIT)


def _mod_kernel(c_ref, w_ref, b_ref, o_ref):
    cond = _silu(c_ref[...]).astype(BF16)
    o_ref[0] = _dot(cond, w_ref[0].astype(BF16)) + b_ref[0]


def _modulation(c, mod_w, mod_b):
    depth, d, n = mod_w.shape
    b = c.shape[0]
    nb = n // d
    return pl.pallas_call(
        _mod_kernel,
        grid=(depth, nb),
        in_specs=[pl.BlockSpec((b, d), lambda l, j: (0, 0)),
                  pl.BlockSpec((1, d, d), lambda l, j: (l, 0, j)),
                  pl.BlockSpec((1, 1, d), lambda l, j: (l, 0, j))],
        out_specs=pl.BlockSpec((1, b, d), lambda l, j: (l, 0, j)),
        out_shape=jax.ShapeDtypeStruct((depth, b, n), F32),
        compiler_params=_params("arbitrary", "arbitrary"),
        name="adaln_mod",
    )(c, mod_w, mod_b.reshape(depth, 1, n))


def _in_proj_kernel(x_ref, mod_ref, w_ref, zx_ref, na_ref, mlq_ref, mlo_ref, hy_ref, sm_ref):
    h = (_rms(x_ref[...]) * (1.0 + mod_ref[0, 1:2, :]) + mod_ref[0, 0:1, :]).astype(BF16)
    zx_ref[...] = _dot(h, w_ref[:, COL_ZX:COL_NA])
    na_ref[...] = _dot(h, w_ref[:, COL_NA:COL_MLQ]).astype(BF16)
    mlq_ref[...] = _dot(h, w_ref[:, COL_MLQ:COL_MLO]).astype(BF16)
    mlo_ref[...] = _dot(h, w_ref[:, COL_MLO:COL_HY])
    hy_ref[...] = _dot(h, w_ref[:, COL_HY:COL_SM])
    sm_ref[...] = _dot(h, w_ref[:, COL_SM:P_PAD])


def _in_proj(x2, mod_l, w_perm):
    n_tok, d = x2.shape
    tm = TOKEN_TILE
    tiles_per_seq = SEQ // tm
    widths = (1024, 768, 768, 256, 768, 128)
    dtypes = (F32, BF16, BF16, F32, F32, F32)
    return pl.pallas_call(
        _in_proj_kernel,
        grid=(n_tok // tm,),
        in_specs=[pl.BlockSpec((tm, d), lambda i: (i, 0)),
                  pl.BlockSpec((1, 6, d), lambda i: (i // tiles_per_seq, 0, 0)),
                  pl.BlockSpec((d, P_PAD), lambda i: (0, 0))],
        out_specs=[pl.BlockSpec((tm, w), lambda i: (i, 0)) for w in widths],
        out_shape=[jax.ShapeDtypeStruct((n_tok, w), t) for w, t in zip(widths, dtypes)],
        compiler_params=_params("arbitrary"),
        name="in_proj",
    )(x2, mod_l, w_perm)


def _dwconv3_rows(src_ref, r0, cols, w, bias):
    x = src_ref[0, pl.ds(r0, CHUNK), cols]
    up = src_ref[0, pl.ds(jnp.maximum(r0 - 1, 0), 1), cols]
    dn = src_ref[0, pl.ds(jnp.minimum(r0 + CHUNK, SEQ - 1), 1), cols]
    up = jnp.where(r0 > 0, up, 0.0)
    dn = jnp.where(r0 + CHUNK < SEQ, dn, 0.0)
    row = lax.broadcasted_iota(jnp.int32, (CHUNK, 1), 0)
    prev = jnp.where(row == 0, up, pltpu.roll(x, 1, 0))
    nxt = jnp.where(row == CHUNK - 1, dn, pltpu.roll(x, CHUNK - 1, 0))
    return w[0:1, :] * prev + w[1:2, :] * x + w[2:3, :] * nxt + bias


def _chunk_scans(a, axis):
    n = a.shape[axis]
    shape = [1, 1]
    shape[axis] = n
    pos = lax.broadcasted_iota(jnp.int32, tuple(shape), axis) % CHUNK
    pre, suf = a, a
    k = 1
    while k < CHUNK:
        pre = pre + jnp.where(pos >= k, pltpu.roll(pre, k, axis), 0.0)
        suf = suf + jnp.where(pos < CHUNK - k, pltpu.roll(suf, n - k, axis), 0.0)
        k *= 2
    return pre, suf


def _gate_rows(sm_ref, rows_s, lo, n):
    for c in range(N_CHUNKS):
        t = sm_ref[0, c * CHUNK:(c + 1) * CHUNK, :].T
        rows_s[c] = t[lo:lo + n, :]


def _ssd_kernel(zx_ref, sm_ref, cw_ref, cb_ref, dtb_row_ref, a_row_ref, dtb_col_ref, a_col_ref, dsk_ref, g_ref,
                o_ref, xbc_s, acol_s, dtc_s, rows_s, arow_s, y_s, st_s):
    cw = cw_ref[...]
    cb = cb_ref[...]

    def conv_body(c, _):
        r0 = pl.multiple_of(c * CHUNK, CHUNK)
        xbc_s[pl.ds(r0, CHUNK), :] = _silu(_dwconv3_rows(zx_ref, r0, slice(GROUP_W, GROUP_W + SSD_XBC), cw, cb))
        return 0

    lax.fori_loop(0, N_CHUNKS, conv_body, 0)

    lane = lax.broadcasted_iota(jnp.int32, (1, CHUNK), 1)
    dt_col = jax.nn.softplus(sm_ref[0] + dtb_row_ref[...])
    pre, suf = _chunk_scans(dt_col * (-jnp.exp(a_row_ref[...])), 0)
    dtc_s[...] = dt_col
    acol_s[...] = jnp.where(lane < SM_DT + N_HEADS, pre, suf)

    _gate_rows(sm_ref, rows_s, SM_DT, 8)
    dt_row = jax.nn.softplus(rows_s[...].reshape(N_CHUNKS * 8, CHUNK) + dtb_col_ref[...])
    pre, suf = _chunk_scans(dt_row * (-jnp.exp(a_col_ref[...])), 1)
    rid = lax.broadcasted_iota(jnp.int32, (N_CHUNKS * 8, 1), 0) % 8
    arow_s[...] = jnp.where(rid < N_HEADS, pre, suf).reshape(N_CHUNKS, 8, CHUNK)

    ti = lax.broadcasted_iota(jnp.int32, (CHUNK, CHUNK), 0)
    si = lax.broadcasted_iota(jnp.int32, (CHUNK, CHUNK), 1)

    for d in range(2):
        mask = (si <= ti) if d == 0 else (si >= ti)
        end = CHUNK - 1 if d == 0 else 0
        st_s[...] = jnp.zeros_like(st_s)

        def chunk_body(i, _, d=d, mask=mask, end=end):
            c = i if d == 0 else N_CHUNKS - 1 - i
            r0 = pl.multiple_of(c * CHUNK, CHUNK)
            xs = xbc_s[pl.ds(r0, CHUNK), 0:GROUP_W]
            bm = xbc_s[pl.ds(r0, CHUNK), GROUP_W:2 * GROUP_W].astype(BF16)
            cm = xbc_s[pl.ds(r0, CHUNK), 2 * GROUP_W:3 * GROUP_W].astype(BF16)
            acol = acol_s[pl.ds(r0, CHUNK), :]
            dtc = dtc_s[pl.ds(r0, CHUNK), :]
            arow = arow_s[c]
            ys = []
            for g in range(2):
                cg = cm[:, g * SSD_STATE:(g + 1) * SSD_STATE]
                bg = bm[:, g * SSD_STATE:(g + 1) * SSD_STATE]
                cb_ts = lax.dot_general(cg, bg, NT_DIMS, preferred_element_type=F32)
                for hh in range(2):
                    h = 2 * g + hh
                    j = d * N_HEADS + h
                    ac = acol[:, j:j + 1]
                    ar = arow[j:j + 1, :]
                    a_end = acol[end:end + 1, j:j + 1]
                    decay = jnp.exp(jnp.where(mask, ac - ar, -jnp.inf))
                    xdt = xs[:, h * HEAD_DIM:(h + 1) * HEAD_DIM] * dtc[:, j:j + 1]
                    y_diag = _dot((cb_ts * decay).astype(BF16), xdt.astype(BF16))
                    prev = st_s[h]
                    y_off = lax.dot_general(cg, prev.astype(BF16), NT_DIMS, preferred_element_type=F32) * jnp.exp(ac)
                    x_end = (xdt * jnp.exp(a_end - ac)).astype(BF16)
                    st_s[h] = prev * jnp.exp(a_end) + lax.dot_general(x_end, bg, TN_DIMS, preferred_element_type=F32)
                    ys.append(y_diag + y_off)
            y = jnp.concatenate(ys, axis=1)
            if d == 0:
                y_s[pl.ds(r0, CHUNK), :] = y
            else:
                y_s[pl.ds(r0, CHUNK), :] += y
            return 0

        lax.fori_loop(0, N_CHUNKS, chunk_body, 0)

    y = y_s[...] + dsk_ref[...] * xbc_s[:, 0:GROUP_W]
    y = y * _silu(zx_ref[0, :, 0:GROUP_W])
    o_ref[0] = _rms(y) * g_ref[...]


def _ssd_mixer(zx, sm, conv_w, conv_b, dt_bias, a_log, d_skip, norm_g):
    b = zx.shape[0]
    pad_row = lambda v: jnp.zeros((1, CHUNK), F32).at[0, SM_DT:SM_DT + 8].set(v.reshape(8))
    tile_col = lambda v: jnp.tile(v.reshape(8, 1), (N_CHUNKS, 1))
    full = lambda shape: pl.BlockSpec(shape, lambda i: (0,) * len(shape))
    return pl.pallas_call(
        _ssd_kernel,
        grid=(b,),
        in_specs=[pl.BlockSpec((1, SEQ, 1024), lambda i: (i, 0, 0)),
                  pl.BlockSpec((1, SEQ, CHUNK), lambda i: (i, 0, 0)),
                  full((3, SSD_XBC)), full((1, SSD_XBC)), full((1, CHUNK)), full((1, CHUNK)),
                  full((N_CHUNKS * 8, 1)), full((N_CHUNKS * 8, 1)), full((1, GROUP_W)), full((1, GROUP_W))],
        out_specs=pl.BlockSpec((1, SEQ, GROUP_W), lambda i: (i, 0, 0)),
        out_shape=jax.ShapeDtypeStruct((b, SEQ, GROUP_W), F32),
        scratch_shapes=[pltpu.VMEM((SEQ, SSD_XBC), F32), pltpu.VMEM((SEQ, CHUNK), F32), pltpu.VMEM((SEQ, CHUNK), F32),
                        pltpu.VMEM((N_CHUNKS, 8, CHUNK), F32), pltpu.VMEM((N_CHUNKS, 8, CHUNK), F32),
                        pltpu.VMEM((SEQ, GROUP_W), F32), pltpu.VMEM((N_HEADS, HEAD_DIM, SSD_STATE), F32)],
        compiler_params=_params("arbitrary"),
        name="ssd_mixer",
    )(zx, sm, conv_w, conv_b.reshape(1, SSD_XBC), pad_row(dt_bias), pad_row(a_log), tile_col(dt_bias), tile_col(a_log),
      jnp.repeat(d_skip, HEAD_DIM).reshape(1, GROUP_W), norm_g.reshape(1, GROUP_W))


def _na_kernel(qkv_ref, bias_ref, g_ref, o_ref, acc_s):
    q_rows = NA_QROWS * GRID_W
    k_rows = NA_KROWS * GRID_W
    n_blocks = SEQ // q_rows

    for h in range(N_HEADS):
        qc = slice(h * HEAD_DIM, (h + 1) * HEAD_DIM)
        kc = slice(GROUP_W + h * HEAD_DIM, GROUP_W + (h + 1) * HEAD_DIM)
        vc = slice(2 * GROUP_W + h * HEAD_DIM, 2 * GROUP_W + (h + 1) * HEAD_DIM)

        def block(q0, k0, kind, qc=qc, kc=kc, vc=vc, h=h):
            q = qkv_ref[0, pl.ds(q0, q_rows), qc]
            k = qkv_ref[0, pl.ds(k0, k_rows), kc]
            v = qkv_ref[0, pl.ds(k0, k_rows), vc]
            s = lax.dot_general(q, k, NT_DIMS, preferred_element_type=F32) * (HEAD_DIM ** -0.5) + bias_ref[h, kind]
            p = jnp.exp(s - jnp.max(s, axis=-1, keepdims=True))
            o = _dot(p.astype(BF16), v) / jnp.sum(p, axis=-1, keepdims=True)
            acc_s[pl.ds(q0, q_rows), qc] = o

        block(0, 0, 0)

        def interior(i, _, block=block):
            block(pl.multiple_of(i * q_rows, q_rows), pl.multiple_of((i - 1) * q_rows, q_rows), 1)
            return 0

        lax.fori_loop(1, n_blocks - 1, interior, 0)
        block(SEQ - q_rows, SEQ - k_rows, 2)

    o_ref[0] = _rms(acc_s[...]) * g_ref[...]


def _na_bias_table(rpb):
    rows = SEQ // GRID_W
    qi = np.arange(NA_QROWS * GRID_W)
    ki = np.arange(NA_KROWS * GRID_W)
    qr_l, qcol = qi // GRID_W, qi % GRID_W
    kr_l, kcol = ki // GRID_W, ki % GRID_W
    cs = np.clip(qcol - NA_KW // 2, 0, GRID_W - NA_KW)
    valid_c = (kcol[None, :] >= cs[:, None]) & (kcol[None, :] < cs[:, None] + NA_KW)
    dc = np.clip(kcol[None, :] - qcol[:, None] + NA_KW - 1, 0, 2 * NA_KW - 2)
    drs, valids = [], []
    for r0, ks in ((0, 0), (NA_QROWS, 0), (rows - NA_QROWS, rows - NA_KROWS)):
        qr = r0 + qr_l
        kr = ks + kr_l
        start = np.clip(qr - NA_KH // 2, 0, rows - NA_KH)
        valid_r = (kr[None, :] >= start[:, None]) & (kr[None, :] < start[:, None] + NA_KH)
        drs.append(np.clip(kr[None, :] - qr[:, None] + NA_KH - 1, 0, 2 * NA_KH - 2))
        valids.append(valid_r & valid_c)
    dr = np.stack(drs)
    valid = np.stack(valids)
    bias = rpb[:, dr, np.broadcast_to(dc, dr.shape)]
    return jnp.where(valid[None], bias, -1e30).astype(F32)


def _na_mixer(qkv, rpb, norm_g):
    b = qkv.shape[0]
    table = _na_bias_table(rpb)
    return pl.pallas_call(
        _na_kernel,
        grid=(b,),
        in_specs=[pl.BlockSpec((1, SEQ, 3 * GROUP_W), lambda i: (i, 0, 0)),
                  pl.BlockSpec(table.shape, lambda i: (0, 0, 0, 0)),
                  pl.BlockSpec((1, GROUP_W), lambda i: (0, 0))],
        out_specs=pl.BlockSpec((1, SEQ, GROUP_W), lambda i: (i, 0, 0)),
        out_shape=jax.ShapeDtypeStruct((b, SEQ, GROUP_W), F32),
        scratch_shapes=[pltpu.VMEM((SEQ, GROUP_W), F32)],
        compiler_params=_params("arbitrary"),
        name="na_mixer",
    )(qkv, table, norm_g.reshape(1, GROUP_W))


def _ml_kernel(qkv_ref, og_ref, sm_ref, brow_ref, bcol_ref, g_ref, o_ref, gcol_s, rows_s, lrow_s, h_s, c_s, m_s):
    lane = lax.broadcasted_iota(jnp.int32, (1, CHUNK), 1)
    raw = sm_ref[0] + brow_ref[...]
    pre, suf = _chunk_scans(jax.nn.log_sigmoid(raw), 0)
    fwd_f = (lane >= SM_FG) & (lane < SM_FG + N_HEADS)
    gcol_s[...] = jnp.where(lane < SM_FG, raw, jnp.where(fwd_f, pre, suf))
    _gate_rows(sm_ref, rows_s, SM_IG, 16)
    raw = rows_s[...].reshape(N_CHUNKS * 16, CHUNK) + bcol_ref[...]
    pre, suf = _chunk_scans(jax.nn.log_sigmoid(raw), 1)
    rid = lax.broadcasted_iota(jnp.int32, (N_CHUNKS * 16, 1), 0) % 16
    lrow_s[...] = jnp.where(rid < 8, raw, jnp.where(rid < 8 + N_HEADS, pre, suf)).reshape(N_CHUNKS, 16, CHUNK)

    ti = lax.broadcasted_iota(jnp.int32, (CHUNK, CHUNK), 0)
    si = lax.broadcasted_iota(jnp.int32, (CHUNK, CHUNK), 1)
    one_col = (lax.broadcasted_iota(jnp.int32, (CHUNK, HEAD_DIM), 1) == 0).astype(BF16)

    for d in range(2):
        mask = (si <= ti) if d == 0 else (si >= ti)
        end = CHUNK - 1 if d == 0 else 0
        c_s[...] = jnp.zeros_like(c_s)
        m_s[...] = jnp.zeros_like(m_s)

        def chunk_body(i, _, d=d, mask=mask, end=end):
            c = i if d == 0 else N_CHUNKS - 1 - i
            r0 = pl.multiple_of(c * CHUNK, CHUNK)
            gcol = gcol_s[pl.ds(r0, CHUNK), :]
            lrow = lrow_s[c]
            hs = []
            for h in range(N_HEADS):
                j = d * N_HEADS + h
                q = qkv_ref[0, pl.ds(r0, CHUNK), h * HEAD_DIM:(h + 1) * HEAD_DIM]
                kf = qkv_ref[0, pl.ds(r0, CHUNK), GROUP_W + h * HEAD_DIM:GROUP_W + (h + 1) * HEAD_DIM].astype(F32)
                kf = kf * (HEAD_DIM ** -0.5)
                v = qkv_ref[0, pl.ds(r0, CHUNK), 2 * GROUP_W + h * HEAD_DIM:2 * GROUP_W + (h + 1) * HEAD_DIM]
                v_ext = jnp.concatenate([v, one_col], axis=1)
                li_c = gcol[:, SM_IG + j:SM_IG + j + 1]
                b_c = gcol[:, SM_FG + j:SM_FG + j + 1]
                li_r = lrow[j:j + 1, :]
                b_r = lrow[8 + j:8 + j + 1, :]
                m = m_s[h][0:1, 0:1]
                c_ext = c_s[h]
                g_inter = b_c + m
                g_intra = jnp.where(mask, b_c - b_r + li_r, -jnp.inf)
                m_t = jnp.maximum(g_inter, jnp.max(g_intra, axis=-1, keepdims=True))
                w_inter = jnp.exp(g_inter - m_t)
                s = lax.dot_general(q, kf.astype(BF16), NT_DIMS, preferred_element_type=F32) * jnp.exp(g_intra - m_t)
                q_c = _dot(q, c_ext.astype(BF16))
                num = w_inter * q_c[:, 0:HEAD_DIM] + _dot(s.astype(BF16), v)
                den = w_inter * q_c[:, HEAD_DIM:HEAD_DIM + 1] + jnp.sum(s, axis=-1, keepdims=True)
                hs.append(num / jnp.maximum(jnp.abs(den), jnp.exp(-m_t)))
                total = b_c[end:end + 1, :]
                g_state = total + m
                g_src = total - b_c + li_c
                m_new = jnp.maximum(g_state, jnp.max(g_src, axis=0, keepdims=True))
                k_w = (kf * jnp.exp(g_src - m_new)).astype(BF16)
                c_s[h] = jnp.exp(g_state - m_new) * c_ext + lax.dot_general(k_w, v_ext, TN_DIMS, preferred_element_type=F32)
                m_s[h] = jnp.broadcast_to(m_new, (8, CHUNK))
            hcat = jnp.concatenate(hs, axis=1)
            if d == 0:
                h_s[pl.ds(r0, CHUNK), :] = hcat
            else:
                h_s[pl.ds(r0, CHUNK), :] += hcat
            return 0

        lax.fori_loop(0, N_CHUNKS, chunk_body, 0)

    hsum = h_s[...]
    normed = jnp.concatenate([_rms(hsum[:, h * HEAD_DIM:(h + 1) * HEAD_DIM]) for h in range(N_HEADS)], axis=1)
    o_ref[0] = jax.nn.sigmoid(og_ref[0]) * (normed * g_ref[...])


def _ml_mixer(qkv, og, sm, i_bias, f_bias, norm_g):
    b = qkv.shape[0]
    bias16 = jnp.concatenate([i_bias.reshape(8), f_bias.reshape(8)])
    brow = jnp.zeros((1, CHUNK), F32).at[0, SM_IG:SM_IG + 16].set(bias16)
    bcol = jnp.tile(bias16.reshape(16, 1), (N_CHUNKS, 1))
    full = lambda shape: pl.BlockSpec(shape, lambda i: (0,) * len(shape))
    return pl.pallas_call(
        _ml_kernel,
        grid=(b,),
        in_specs=[pl.BlockSpec((1, SEQ, 3 * GROUP_W), lambda i: (i, 0, 0)),
                  pl.BlockSpec((1, SEQ, GROUP_W), lambda i: (i, 0, 0)),
                  pl.BlockSpec((1, SEQ, CHUNK), lambda i: (i, 0, 0)),
                  full((1, CHUNK)), full((N_CHUNKS * 16, 1)), full((1, GROUP_W))],
        out_specs=pl.BlockSpec((1, SEQ, GROUP_W), lambda i: (i, 0, 0)),
        out_shape=jax.ShapeDtypeStruct((b, SEQ, GROUP_W), F32),
        scratch_shapes=[pltpu.VMEM((SEQ, CHUNK), F32), pltpu.VMEM((N_CHUNKS, 16, CHUNK), F32),
                        pltpu.VMEM((N_CHUNKS, 16, CHUNK), F32), pltpu.VMEM((SEQ, GROUP_W), F32),
                        pltpu.VMEM((N_HEADS, HEAD_DIM, CHUNK), F32), pltpu.VMEM((N_HEADS, 8, CHUNK), F32)],
        compiler_params=_params("arbitrary"),
        name="mlstm_mixer",
    )(qkv, og, sm, brow, bcol, norm_g.reshape(1, GROUP_W))


def _dft_tables():
    f = np.arange(SEQ, dtype=np.int64)[:, None]
    t = np.arange(SEQ, dtype=np.int64)[None, :]
    ang = (2.0 * np.pi / FFT_N) * ((f * t) % FFT_N).astype(np.float64)
    wc = np.cos(ang)
    ws = np.sin(ang)
    ws[0, :] = np.where(np.arange(SEQ) % 2 == 0, 1.0, -1.0)
    w = np.concatenate([wc, ws], axis=0).astype(np.float32)
    return jnp.asarray(w).astype(BF16), jnp.asarray(np.ascontiguousarray(w.T)).astype(BF16)


def _hy_features():
    t = jnp.linspace(0.0, 1.0, SEQ, dtype=F32)[:, None]
    bands = (HY_EMB - 1) // 2
    f = jnp.linspace(1e-4, bands - 1, bands, dtype=F32)
    ang = (2.0 * math.pi) * (jnp.arange(SEQ, dtype=F32) / SEQ)[:, None] * f[None, :]
    feats = jnp.concatenate([t, jnp.cos(ang), -jnp.sin(ang)], axis=-1)
    deltas = jnp.abs(jnp.linspace(math.log(1e-2) / 1.5, math.log(1e-2) / 0.3, GROUP_W, dtype=F32))
    return jnp.pad(feats, ((0, 0), (0, CHUNK - HY_EMB))), t, deltas.reshape(1, GROUP_W)


def _hy_filter_kernel(feats_ref, w1_ref, b1_ref, w2_ref, b2_ref, w3_ref, freq_ref, t_ref, dl_ref, hp_ref, hm_ref):
    h = jnp.sin(freq_ref[0:1, :] * (_dot(feats_ref[...].astype(BF16), w1_ref[...].astype(BF16)) + b1_ref[...]))
    h = jnp.sin(freq_ref[1:2, :] * (_dot(h.astype(BF16), w2_ref[...].astype(BF16)) + b2_ref[...]))
    h = _dot(h.astype(BF16), w3_ref[...].astype(BF16))
    win = jnp.exp(-t_ref[...] * dl_ref[...])
    row = lax.broadcasted_iota(jnp.int32, (SEQ, 1), 0)
    for o in range(2):
        hf = h[:, (2 * o) * GROUP_W:(2 * o + 1) * GROUP_W] * win
        hb = h[:, (2 * o + 1) * GROUP_W:(2 * o + 2) * GROUP_W] * win
        r = lax.rsqrt(jnp.sum(hf * hf + hb * hb, axis=0, keepdims=True) + NORM_EPS)
        hf = hf * r
        hb = jnp.where(row == 0, 0.0, hb * r)
        hp_ref[:, o * GROUP_W:(o + 1) * GROUP_W] = (hf + hb).astype(BF16)
        hm_ref[:, o * GROUP_W:(o + 1) * GROUP_W] = (hf - hb).astype(BF16)


def _hy_spec_kernel(wc_ref, ws_ref, hp_ref, hm_ref, kc_ref, ks_ref):
    rowg = lax.broadcasted_iota(jnp.int32, (FREQ_BLOCK, 1), 0) + pl.program_id(0) * FREQ_BLOCK
    scale = jnp.where(rowg == 0, 1.0 / FFT_N, 2.0 / FFT_N)
    kc_ref[...] = _dot(wc_ref[...], hp_ref[...]) * scale
    nyq = _dot(ws_ref[0:8, :], hp_ref[...])[0:1, :]
    ks_ref[...] = jnp.where(rowg == 0, nyq, _dot(ws_ref[...], hm_ref[...])) * scale


def _hy_spectrum(w, w1, b1, w2, b2, w3, freq):
    feats, t, deltas = _hy_features()
    w1p = jnp.pad(w1, ((0, CHUNK - HY_EMB), (0, 0)))
    hp, hm = pl.pallas_call(
        _hy_filter_kernel,
        out_shape=[jax.ShapeDtypeStruct((SEQ, 2 * GROUP_W), BF16)] * 2,
        compiler_params=pltpu.CompilerParams(vmem_limit_bytes=VMEM_LIMIT),
        name="hyena_filter",
    )(feats, w1p, b1.reshape(1, HY_FFN), w2, b2.reshape(1, HY_FFN), w3, freq, t, deltas)
    nj = N_FREQ_BLOCKS
    return pl.pallas_call(
        _hy_spec_kernel,
        grid=(nj,),
        in_specs=[pl.BlockSpec((FREQ_BLOCK, SEQ), lambda j: (j, 0)),
                  pl.BlockSpec((FREQ_BLOCK, SEQ), lambda j: (j + nj, 0)),
                  pl.BlockSpec((SEQ, 2 * GROUP_W), lambda j: (0, 0)),
                  pl.BlockSpec((SEQ, 2 * GROUP_W), lambda j: (0, 0))],
        out_specs=[pl.BlockSpec((FREQ_BLOCK, 2 * GROUP_W), lambda j: (j, 0))] * 2,
        out_shape=[jax.ShapeDtypeStruct((SEQ, 2 * GROUP_W), F32)] * 2,
        compiler_params=_params("arbitrary"),
        name="hyena_spectrum",
    )(w, w, hp, hm)


def _hy_short_conv_kernel(u_ref, cw_ref, cb_ref, o_ref):
    cw = cw_ref[...]
    cb = cb_ref[...]

    def body(c, _):
        r0 = pl.multiple_of(c * CHUNK, CHUNK)
        o_ref[0, pl.ds(r0, CHUNK), :] = _dwconv3_rows(u_ref, r0, slice(0, 3 * GROUP_W), cw, cb)
        return 0

    lax.fori_loop(0, N_CHUNKS, body, 0)


def _hy_short_conv(u, conv_w, conv_b):
    b = u.shape[0]
    return pl.pallas_call(
        _hy_short_conv_kernel,
        grid=(b,),
        in_specs=[pl.BlockSpec((1, SEQ, 3 * GROUP_W), lambda i: (i, 0, 0)),
                  pl.BlockSpec((3, 3 * GROUP_W), lambda i: (0, 0)),
                  pl.BlockSpec((1, 3 * GROUP_W), lambda i: (0, 0))],
        out_specs=pl.BlockSpec((1, SEQ, 3 * GROUP_W), lambda i: (i, 0, 0)),
        out_shape=jax.ShapeDtypeStruct(u.shape, F32),
        compiler_params=_params("arbitrary"),
        name="hyena_short_conv",
    )(u, conv_w, conv_b.reshape(1, 3 * GROUP_W))


def _hy_conv_kernel(z_ref, gate_ref, wc_ref, ws_ref, wct_ref, wst_ref, kc_ref, ks_ref, skip_ref, g_ref, o_ref, zb_s,
                    *, final_norm):
    j = pl.program_id(1)
    n_b = z_ref.shape[0]

    @pl.when(j == 0)
    def _():
        zb_s[...] = z_ref[...].astype(BF16)
        o_ref[...] = jnp.zeros_like(o_ref)

    row0 = (lax.broadcasted_iota(jnp.int32, (FREQ_BLOCK, 1), 0) + j * FREQ_BLOCK) == 0
    kc = kc_ref[...]
    ks = ks_ref[...]
    for b in range(n_b):
        zb = zb_s[b]
        xc = _dot(wc_ref[...], zb)
        xs = _dot(ws_ref[...], zb)
        xs_ks = xs * ks
        yc = xc * kc - jnp.where(row0, 0.0, xs_ks)
        ys = jnp.where(row0, xs_ks, xc * ks + xs * kc)
        o_ref[b] += _dot(wct_ref[...], yc.astype(BF16)) + _dot(wst_ref[...], ys.astype(BF16))

    @pl.when(j == pl.num_programs(1) - 1)
    def _():
        for b in range(n_b):
            r = gate_ref[b] * (o_ref[b] + skip_ref[...] * z_ref[b])
            if final_norm:
                r = _rms(r) * g_ref[...]
            o_ref[b] = r


def _hy_long_conv(z, z_col, gates, gate_col, w, wt, kc, ks, order, skip, norm_g, final_norm):
    b = z.shape[0]
    bg = HY_BATCH_BLOCK
    nj = N_FREQ_BLOCKS
    return pl.pallas_call(
        functools.partial(_hy_conv_kernel, final_norm=final_norm),
        grid=(b // bg, nj),
        in_specs=[pl.BlockSpec((bg, SEQ, GROUP_W), lambda g, j: (g, 0, z_col)),
                  pl.BlockSpec((bg, SEQ, GROUP_W), lambda g, j: (g, 0, gate_col)),
                  pl.BlockSpec((FREQ_BLOCK, SEQ), lambda g, j: (j, 0)),
                  pl.BlockSpec((FREQ_BLOCK, SEQ), lambda g, j: (j + nj, 0)),
                  pl.BlockSpec((SEQ, FREQ_BLOCK), lambda g, j: (0, j)),
                  pl.BlockSpec((SEQ, FREQ_BLOCK), lambda g, j: (0, j + nj)),
                  pl.BlockSpec((FREQ_BLOCK, GROUP_W), lambda g, j: (j, order)),
                  pl.BlockSpec((FREQ_BLOCK, GROUP_W), lambda g, j: (j, order)),
                  pl.BlockSpec((1, GROUP_W), lambda g, j: (0, 0)),
                  pl.BlockSpec((1, GROUP_W), lambda g, j: (0, 0))],
        out_specs=pl.BlockSpec((bg, SEQ, GROUP_W), lambda g, j: (g, 0, 0)),
        out_shape=jax.ShapeDtypeStruct((b, SEQ, GROUP_W), F32),
        scratch_shapes=[pltpu.VMEM((bg, SEQ, GROUP_W), BF16)],
        compiler_params=_params("arbitrary", "arbitrary"),
        name="hyena_long_conv",
    )(z, gates, w, w, wt, wt, kc, ks, skip.reshape(1, GROUP_W), norm_g.reshape(1, GROUP_W))


def _hy_mixer(u, w, wt, conv_w, conv_b, w1, b1, w2, b2, w3, freq, skip, norm_g):
    kc, ks = _hy_spectrum(w, w1, b1, w2, b2, w3, freq)
    uc = _hy_short_conv(u, conv_w, conv_b)
    z1 = _hy_long_conv(uc, 0, uc, 1, w, wt, kc, ks, 0, skip[0], norm_g, False)
    return _hy_long_conv(z1, 0, uc, 2, w, wt, kc, ks, 1, skip[1], norm_g, True)


def _out_proj_kernel(x_ref, mod_ref, ya_ref, yb_ref, yc_ref, yd_ref, w_ref, o_ref):
    y = jnp.concatenate([ya_ref[...], yb_ref[...], yc_ref[...], yd_ref[...]], axis=1).astype(BF16)
    o_ref[...] = x_ref[...] + mod_ref[0, 2:3, :] * _dot(y, w_ref[...])


def _out_proj(x2, mod_l, ys, w_out):
    n_tok, d = x2.shape
    tm = TOKEN_TILE
    tiles_per_seq = SEQ // tm
    return pl.pallas_call(
        _out_proj_kernel,
        grid=(n_tok // tm,),
        in_specs=[pl.BlockSpec((tm, d), lambda i: (i, 0)),
                  pl.BlockSpec((1, 6, d), lambda i: (i // tiles_per_seq, 0, 0))]
                 + [pl.BlockSpec((tm, GROUP_W), lambda i: (i, 0))] * 4
                 + [pl.BlockSpec((d, d), lambda i: (0, 0))],
        out_specs=pl.BlockSpec((tm, d), lambda i: (i, 0)),
        out_shape=jax.ShapeDtypeStruct((n_tok, d), F32),
        compiler_params=_params("arbitrary"),
        name="out_proj",
    )(x2, mod_l, *ys, w_out)


FFN_SPLITS = ((0, 1024), (1024, 2048), (2048, FFN_HIDDEN))


def _ffn_kernel(x_ref, mod_ref, wg_ref, wu_ref, wd_ref, gf_ref, o_ref, *, final_norm):
    x = x_ref[...]
    h = (_rms(x) * (1.0 + mod_ref[0, 4:5, :]) + mod_ref[0, 3:4, :]).astype(BF16)
    acc = jnp.zeros_like(x)
    for lo, hi in FFN_SPLITS:
        a = _silu(_dot(h, wg_ref[:, lo:hi])) * _dot(h, wu_ref[:, lo:hi])
        acc = acc + _dot(a.astype(BF16), wd_ref[lo:hi, :])
    y = x + mod_ref[0, 5:6, :] * acc
    if final_norm:
        y = _rms(y) * gf_ref[...]
    o_ref[...] = y


def _ffn(x2, mod_l, wg, wu, wd, final_g, final_norm):
    n_tok, d = x2.shape
    tm = TOKEN_TILE
    tiles_per_seq = SEQ // tm
    resident = lambda shape: pl.BlockSpec(shape, lambda i: (0, 0), pipeline_mode=pl.Buffered(1))
    return pl.pallas_call(
        functools.partial(_ffn_kernel, final_norm=final_norm),
        grid=(n_tok // tm,),
        in_specs=[pl.BlockSpec((tm, d), lambda i: (i, 0)),
                  pl.BlockSpec((1, 6, d), lambda i: (i // tiles_per_seq, 0, 0)),
                  resident((d, FFN_HIDDEN)), resident((d, FFN_HIDDEN)), resident((FFN_HIDDEN, d)),
                  pl.BlockSpec((1, d), lambda i: (0, 0))],
        out_specs=pl.BlockSpec((tm, d), lambda i: (i, 0)),
        out_shape=jax.ShapeDtypeStruct((n_tok, d), F32),
        compiler_params=_params("arbitrary"),
        name="swiglu_ffn",
    )(x2, mod_l, wg, wu, wd, final_g.reshape(1, d))


def _permute_w_in(w):
    cols = [w[:, 0:1024], w[:, 1032:1800], w[:, 1800:2568], w[:, 2568:2824], w[:, 2840:3608],
            w[:, 1024:1032], w[:, 2824:2840], jnp.zeros((w.shape[0], P_PAD - 3608), w.dtype)]
    return jnp.concatenate(cols, axis=1).astype(BF16)


def kernel(x, c, mod_w, mod_b, w_in, ssd_conv_w, ssd_conv_b, ssd_dt_bias, ssd_a_log, ssd_d, ssd_norm_g, na_rpb, na_norm_g, ml_i_bias, ml_f_bias, ml_norm_g, hy_conv_w, hy_conv_b, hy_w1, hy_b1, hy_w2, hy_b2, hy_w3, hy_freq, hy_skip, hy_norm_g, w_out, ffn_w_gate, ffn_w_up, ffn_w_down, final_norm_g):
    b, seq, d = x.shape
    assert seq == SEQ and d == D_MODEL and b % HY_BATCH_BLOCK == 0
    depth = mod_w.shape[0]
    mod = _modulation(c, mod_w, mod_b).reshape(depth, b, 6, d)
    dft, dft_t = _dft_tables()
    x2 = x.reshape(b * seq, d)
    for l in range(depth):
        zx, na, mlq, mlo, hy, sm = _in_proj(x2, mod[l], _permute_w_in(w_in[l]))
        seq3 = lambda t: t.reshape(b, seq, t.shape[-1])
        sm3 = seq3(sm)
        y_ssd = _ssd_mixer(seq3(zx), sm3, ssd_conv_w[l], ssd_conv_b[l], ssd_dt_bias[l], ssd_a_log[l], ssd_d[l],
                           ssd_norm_g[l])
        y_na = _na_mixer(seq3(na), na_rpb[l], na_norm_g[l])
        y_ml = _ml_mixer(seq3(mlq), seq3(mlo), sm3, ml_i_bias[l], ml_f_bias[l], ml_norm_g[l])
        y_hy = _hy_mixer(seq3(hy), dft, dft_t, hy_conv_w[l], hy_conv_b[l], hy_w1[l], hy_b1[l], hy_w2[l], hy_b2[l],
                         hy_w3[l], hy_freq[l], hy_skip[l], hy_norm_g[l])
        ys = [t.reshape(b * seq, GROUP_W) for t in (y_ssd, y_na, y_ml, y_hy)]
        x2 = _out_proj(x2, mod[l], ys, w_out[l].astype(BF16))
        x2 = _ffn(x2, mod[l], ffn_w_gate[l].astype(BF16), ffn_w_up[l].astype(BF16), ffn_w_down[l].astype(BF16),
                  final_norm_g, l == depth - 1)
    return x2.reshape(b, seq, d)
```

```python
import functools
import math

import numpy as np
import jax
import jax.numpy as jnp
from jax import lax
from jax.experimental import pallas as pl
from jax.experimental.pallas import tpu as pltpu

F32 = jnp.float32
BF16 = jnp.bfloat16

D_MODEL = 1024
SEQ = 2048
GRID_W = 64
GROUP_W = 256
HEAD_DIM = 64
N_HEADS = 4
SSD_STATE = 128
SSD_XBC = 768
CHUNK = 128
N_CHUNKS = SEQ // CHUNK
NA_KH = 8
NA_KW = 16
NA_QROWS = 4
NA_KROWS = 12
HY_EMB = 33
HY_FFN = 64
FFN_HIDDEN = 2816
NORM_EPS = 1e-6
FFT_N = 2 * SEQ
FREQ_BLOCK = 512
N_FREQ_BLOCKS = SEQ // FREQ_BLOCK
HY_BATCH_BLOCK = 2
TOKEN_TILE = 512
VMEM_LIMIT = 56 * 1024 * 1024

COL_ZX = 0
COL_NA = 1024
COL_MLQ = 1792
COL_MLO = 2560
COL_HY = 2816
COL_SM = 3584
P_PAD = 3712
SM_DT = 0
SM_IG = 8
SM_FG = 16

NT_DIMS = (((1,), (1,)), ((), ()))
TN_DIMS = (((0,), (0,)), ((), ()))


def _rms(x):
    return x * lax.rsqrt(jnp.mean(x * x, axis=-1, keepdims=True) + NORM_EPS)


def _silu(x):
    return x * jax.nn.sigmoid(x)


def _dot(a, b):
    return jnp.dot(a, b, preferred_element_type=F32)


def _params(*sem):
    return pltpu.CompilerParams(dimension_semantics=sem, vmem_limit_bytes=VMEM_LIMIT)


def _mod_kernel(c_ref, w_ref, b_ref, o_ref):
    cond = _silu(c_ref[...]).astype(BF16)
    o_ref[0] = _dot(cond, w_ref[0].astype(BF16)) + b_ref[0]


def _modulation(c, mod_w, mod_b):
    depth, d, n = mod_w.shape
    b = c.shape[0]
    nb = n // d
    return pl.pallas_call(
        _mod_kernel,
        grid=(depth, nb),
        in_specs=[pl.BlockSpec((b, d), lambda l, j: (0, 0)),
                  pl.BlockSpec((1, d, d), lambda l, j: (l, 0, j)),
                  pl.BlockSpec((1, 1, d), lambda l, j: (l, 0, j))],
        out_specs=pl.BlockSpec((1, b, d), lambda l, j: (l, 0, j)),
        out_shape=jax.ShapeDtypeStruct((depth, b, n), F32),
        compiler_params=_params("arbitrary", "arbitrary"),
        name="adaln_mod",
    )(c, mod_w, mod_b.reshape(depth, 1, n))


def _in_proj_kernel(x_ref, mod_ref, w_ref, zx_ref, na_ref, mlq_ref, mlo_ref, hy_ref, sm_ref):
    h = (_rms(x_ref[...]) * (1.0 + mod_ref[0, 1:2, :]) + mod_ref[0, 0:1, :]).astype(BF16)
    zx_ref[...] = _dot(h, w_ref[:, COL_ZX:COL_NA])
    na_ref[...] = _dot(h, w_ref[:, COL_NA:COL_MLQ]).astype(BF16)
    mlq_ref[...] = _dot(h, w_ref[:, COL_MLQ:COL_MLO]).astype(BF16)
    mlo_ref[...] = _dot(h, w_ref[:, COL_MLO:COL_HY])
    hy_ref[...] = _dot(h, w_ref[:, COL_HY:COL_SM])
    sm_ref[...] = _dot(h, w_ref[:, COL_SM:P_PAD])


def _in_proj(x2, mod_l, w_perm):
    n_tok, d = x2.shape
    tm = TOKEN_TILE
    tiles_per_seq = SEQ // tm
    widths = (1024, 768, 768, 256, 768, 128)
    dtypes = (F32, BF16, BF16, F32, F32, F32)
    return pl.pallas_call(
        _in_proj_kernel,
        grid=(n_tok // tm,),
        in_specs=[pl.BlockSpec((tm, d), lambda i: (i, 0)),
                  pl.BlockSpec((1, 6, d), lambda i: (i // tiles_per_seq, 0, 0)),
                  pl.BlockSpec((d, P_PAD), lambda i: (0, 0))],
        out_specs=[pl.BlockSpec((tm, w), lambda i: (i, 0)) for w in widths],
        out_shape=[jax.ShapeDtypeStruct((n_tok, w), t) for w, t in zip(widths, dtypes)],
        compiler_params=_params("arbitrary"),
        name="in_proj",
    )(x2, mod_l, w_perm)


def _dwconv3_rows(src_ref, r0, cols, w, bias):
    x = src_ref[0, pl.ds(r0, CHUNK), cols]
    up = src_ref[0, pl.ds(jnp.maximum(r0 - 1, 0), 1), cols]
    dn = src_ref[0, pl.ds(jnp.minimum(r0 + CHUNK, SEQ - 1), 1), cols]
    up = jnp.where(r0 > 0, up, 0.0)
    dn = jnp.where(r0 + CHUNK < SEQ, dn, 0.0)
    row = lax.broadcasted_iota(jnp.int32, (CHUNK, 1), 0)
    prev = jnp.where(row == 0, up, pltpu.roll(x, 1, 0))
    nxt = jnp.where(row == CHUNK - 1, dn, pltpu.roll(x, CHUNK - 1, 0))
    return w[0:1, :] * prev + w[1:2, :] * x + w[2:3, :] * nxt + bias


def _chunk_scans(a, axis):
    n = a.shape[axis]
    shape = [1, 1]
    shape[axis] = n
    pos = lax.broadcasted_iota(jnp.int32, tuple(shape), axis) % CHUNK
    pre, suf = a, a
    k = 1
    while k < CHUNK:
        pre = pre + jnp.where(pos >= k, pltpu.roll(pre, k, axis), 0.0)
        suf = suf + jnp.where(pos < CHUNK - k, pltpu.roll(suf, n - k, axis), 0.0)
        k *= 2
    return pre, suf


def _gate_rows(sm_ref, rows_s, lo, n):
    for c in range(N_CHUNKS):
        t = sm_ref[0, c * CHUNK:(c + 1) * CHUNK, :].T
        rows_s[c] = t[lo:lo + n, :]


def _ssd_kernel(zx_ref, sm_ref, cw_ref, cb_ref, dtb_row_ref, a_row_ref, dtb_col_ref, a_col_ref, dsk_ref, g_ref,
                o_ref, xs_s, b_s, xst_s, ct_s, acol_s, rows_s, dtrow_s, arow_s, yt_s, st_s):
    cw = cw_ref[...]
    cb = cb_ref[...]

    def conv_body(c, _):
        r0 = pl.multiple_of(c * CHUNK, CHUNK)
        v = _silu(_dwconv3_rows(zx_ref, r0, slice(GROUP_W, GROUP_W + SSD_XBC), cw, cb))
        xs_s[pl.ds(r0, CHUNK), :] = v[:, 0:GROUP_W]
        b_s[pl.ds(r0, CHUNK), :] = v[:, GROUP_W:2 * GROUP_W].astype(BF16)
        xst_s[c] = v[:, 0:GROUP_W].T
        ct_s[c] = v[:, 2 * GROUP_W:3 * GROUP_W].T.astype(BF16)
        return 0

    lax.fori_loop(0, N_CHUNKS, conv_body, 0)

    lane = lax.broadcasted_iota(jnp.int32, (1, CHUNK), 1)
    dt_col = jax.nn.softplus(sm_ref[0] + dtb_row_ref[...])
    pre, suf = _chunk_scans(dt_col * (-jnp.exp(a_row_ref[...])), 0)
    acol_s[...] = jnp.where(lane < SM_DT + N_HEADS, pre, suf)

    _gate_rows(sm_ref, rows_s, SM_DT, 8)
    dt_row = jax.nn.softplus(rows_s[...].reshape(N_CHUNKS * 8, CHUNK) + dtb_col_ref[...])
    pre, suf = _chunk_scans(dt_row * (-jnp.exp(a_col_ref[...])), 1)
    rid = lax.broadcasted_iota(jnp.int32, (N_CHUNKS * 8, 1), 0) % 8
    dtrow_s[...] = dt_row.reshape(N_CHUNKS, 8, CHUNK)
    arow_s[...] = jnp.where(rid < N_HEADS, pre, suf).reshape(N_CHUNKS, 8, CHUNK)

    si = lax.broadcasted_iota(jnp.int32, (CHUNK, CHUNK), 0)
    ti = lax.broadcasted_iota(jnp.int32, (CHUNK, CHUNK), 1)

    for d in range(2):
        mask = (si <= ti) if d == 0 else (si >= ti)
        end = CHUNK - 1 if d == 0 else 0
        st_s[...] = jnp.zeros_like(st_s)

        def chunk_body(i, _, d=d, mask=mask, end=end):
            c = i if d == 0 else N_CHUNKS - 1 - i
            r0 = pl.multiple_of(c * CHUNK, CHUNK)
            bm = b_s[pl.ds(r0, CHUNK), :]
            acol = acol_s[pl.ds(r0, CHUNK), :]
            arow = arow_s[c]
            dtr = dtrow_s[c]
            ys = []
            for g in range(2):
                bg = bm[:, g * SSD_STATE:(g + 1) * SSD_STATE]
                cg_t = ct_s[c, g * SSD_STATE:(g + 1) * SSD_STATE, :]
                scores = _dot(bg, cg_t)
                for hh in range(2):
                    h = 2 * g + hh
                    j = d * N_HEADS + h
                    ar = arow[j:j + 1, :]
                    a_end = ar[:, end:end + 1]
                    decay = jnp.exp(jnp.where(mask, ar - acol[:, j:j + 1], -jnp.inf))
                    xdt_t = xst_s[c, h * HEAD_DIM:(h + 1) * HEAD_DIM, :] * dtr[j:j + 1, :]
                    prev = st_s[h]
                    y = _dot(xdt_t.astype(BF16), (scores * decay).astype(BF16))
                    y = y + _dot(prev.astype(BF16), cg_t) * jnp.exp(ar)
                    x_end = (xdt_t * jnp.exp(a_end - ar)).astype(BF16)
                    st_s[h] = prev * jnp.exp(a_end) + _dot(x_end, bg)
                    ys.append(y)
            y_t = jnp.concatenate(ys, axis=0)
            if d == 0:
                yt_s[c] = y_t
            else:
                yt_s[c] += y_t
            return 0

        lax.fori_loop(0, N_CHUNKS, chunk_body, 0, unroll=4)

    dsk = dsk_ref[...]
    gain = g_ref[...]

    def out_body(c, _):
        r0 = pl.multiple_of(c * CHUNK, CHUNK)
        y = yt_s[c].T + dsk * xs_s[pl.ds(r0, CHUNK), :]
        y = y * _silu(zx_ref[0, pl.ds(r0, CHUNK), 0:GROUP_W])
        o_ref[0, pl.ds(r0, CHUNK), :] = _rms(y) * gain
        return 0

    lax.fori_loop(0, N_CHUNKS, out_body, 0)


def _ssd_mixer(zx, sm, conv_w, conv_b, dt_bias, a_log, d_skip, norm_g):
    b = zx.shape[0]
    pad_row = lambda v: jnp.zeros((1, CHUNK), F32).at[0, SM_DT:SM_DT + 8].set(v.reshape(8))
    tile_col = lambda v: jnp.tile(v.reshape(8, 1), (N_CHUNKS, 1))
    full = lambda shape: pl.BlockSpec(shape, lambda i: (0,) * len(shape))
    return pl.pallas_call(
        _ssd_kernel,
        grid=(b,),
        in_specs=[pl.BlockSpec((1, SEQ, 1024), lambda i: (i, 0, 0)),
                  pl.BlockSpec((1, SEQ, CHUNK), lambda i: (i, 0, 0)),
                  full((3, SSD_XBC)), full((1, SSD_XBC)), full((1, CHUNK)), full((1, CHUNK)),
                  full((N_CHUNKS * 8, 1)), full((N_CHUNKS * 8, 1)), full((1, GROUP_W)), full((1, GROUP_W))],
        out_specs=pl.BlockSpec((1, SEQ, GROUP_W), lambda i: (i, 0, 0)),
        out_shape=jax.ShapeDtypeStruct((b, SEQ, GROUP_W), F32),
        scratch_shapes=[pltpu.VMEM((SEQ, GROUP_W), F32), pltpu.VMEM((SEQ, GROUP_W), BF16),
                        pltpu.VMEM((N_CHUNKS, GROUP_W, CHUNK), F32), pltpu.VMEM((N_CHUNKS, GROUP_W, CHUNK), BF16),
                        pltpu.VMEM((SEQ, CHUNK), F32), pltpu.VMEM((N_CHUNKS, 8, CHUNK), F32),
                        pltpu.VMEM((N_CHUNKS, 8, CHUNK), F32), pltpu.VMEM((N_CHUNKS, 8, CHUNK), F32),
                        pltpu.VMEM((N_CHUNKS, GROUP_W, CHUNK), F32), pltpu.VMEM((N_HEADS, HEAD_DIM, SSD_STATE), F32)],
        compiler_params=_params("arbitrary"),
        name="ssd_mixer",
    )(zx, sm, conv_w, conv_b.reshape(1, SSD_XBC), pad_row(dt_bias), pad_row(a_log), tile_col(dt_bias), tile_col(a_log),
      jnp.repeat(d_skip, HEAD_DIM).reshape(1, GROUP_W), norm_g.reshape(1, GROUP_W))


def _na_kernel(qkv_ref, bias_ref, g_ref, o_ref, acc_s):
    q_rows = NA_QROWS * GRID_W
    k_rows = NA_KROWS * GRID_W
    n_blocks = SEQ // q_rows

    for h in range(N_HEADS):
        qc = slice(h * HEAD_DIM, (h + 1) * HEAD_DIM)
        kc = slice(GROUP_W + h * HEAD_DIM, GROUP_W + (h + 1) * HEAD_DIM)
        vc = slice(2 * GROUP_W + h * HEAD_DIM, 2 * GROUP_W + (h + 1) * HEAD_DIM)

        def block(q0, k0, kind, qc=qc, kc=kc, vc=vc, h=h):
            q = qkv_ref[0, pl.ds(q0, q_rows), qc]
            k = qkv_ref[0, pl.ds(k0, k_rows), kc]
            v = qkv_ref[0, pl.ds(k0, k_rows), vc]
            s = lax.dot_general(q, k, NT_DIMS, preferred_element_type=F32) * (HEAD_DIM ** -0.5) + bias_ref[h, kind]
            p = jnp.exp(s - jnp.max(s, axis=-1, keepdims=True))
            o = _dot(p.astype(BF16), v) / jnp.sum(p, axis=-1, keepdims=True)
            acc_s[pl.ds(q0, q_rows), qc] = o

        block(0, 0, 0)

        def interior(i, _, block=block):
            block(pl.multiple_of(i * q_rows, q_rows), pl.multiple_of((i - 1) * q_rows, q_rows), 1)
            return 0

        lax.fori_loop(1, n_blocks - 1, interior, 0)
        block(SEQ - q_rows, SEQ - k_rows, 2)

    o_ref[0] = _rms(acc_s[...]) * g_ref[...]


def _na_bias_table(rpb):
    rows = SEQ // GRID_W
    col = np.arange(GRID_W)
    cs = np.clip(col - NA_KW // 2, 0, GRID_W - NA_KW)
    valid_c = (col[None, :] >= cs[:, None]) & (col[None, :] < cs[:, None] + NA_KW)
    dc = col[None, :] - col[:, None] + NA_KW - 1
    sel_c = (valid_c[None] & (dc[None] == np.arange(2 * NA_KW - 1)[:, None, None])).astype(np.float32)
    sel_r = np.zeros((3, NA_QROWS, NA_KROWS, 2 * NA_KH - 1), np.float32)
    for kind, (r0, ks) in enumerate(((0, 0), (NA_QROWS, 0), (rows - NA_QROWS, rows - NA_KROWS))):
        for i in range(NA_QROWS):
            start = min(max(r0 + i - NA_KH // 2, 0), rows - NA_KH)
            for kr in range(start, start + NA_KH):
                sel_r[kind, i, kr - ks, kr - (r0 + i) + NA_KH - 1] = 1.0
    valid = sel_r.sum(-1)[:, :, None, :, None] * valid_c[None, None, :, None, :]
    bias = jnp.einsum("hab,tija,bqk->htiqjk", rpb, sel_r, sel_c, precision=lax.Precision.HIGHEST)
    bias = jnp.where(valid[None] > 0, bias, -1e30)
    return bias.reshape(N_HEADS, 3, NA_QROWS * GRID_W, NA_KROWS * GRID_W).astype(F32)


def _na_mixer(qkv, rpb, norm_g):
    b = qkv.shape[0]
    table = _na_bias_table(rpb)
    return pl.pallas_call(
        _na_kernel,
        grid=(b,),
        in_specs=[pl.BlockSpec((1, SEQ, 3 * GROUP_W), lambda i: (i, 0, 0)),
                  pl.BlockSpec(table.shape, lambda i: (0, 0, 0, 0)),
                  pl.BlockSpec((1, GROUP_W), lambda i: (0, 0))],
        out_specs=pl.BlockSpec((1, SEQ, GROUP_W), lambda i: (i, 0, 0)),
        out_shape=jax.ShapeDtypeStruct((b, SEQ, GROUP_W), F32),
        scratch_shapes=[pltpu.VMEM((SEQ, GROUP_W), F32)],
        compiler_params=_params("arbitrary"),
        name="na_mixer",
    )(qkv, table, norm_g.reshape(1, GROUP_W))


ML_STATE_ROWS = 80


def _ml_kernel(qkv_ref, og_ref, sm_ref, brow_ref, bcol_ref, g_ref, o_ref, bcol_s, rows_s, lrow_s, qt_s, vt_s, ht_s,
               c_s, m_s):
    lane = lax.broadcasted_iota(jnp.int32, (1, CHUNK), 1)
    raw = sm_ref[0] + brow_ref[...]
    pre, suf = _chunk_scans(jax.nn.log_sigmoid(raw), 0)
    scanned = jnp.where((lane >= SM_FG) & (lane < SM_FG + N_HEADS), pre, suf)
    bcol_s[...] = raw - pltpu.roll(scanned, CHUNK - (SM_FG - SM_IG), 1)
    _gate_rows(sm_ref, rows_s, SM_IG, 16)
    raw = rows_s[...].reshape(N_CHUNKS * 16, CHUNK) + bcol_ref[...]
    pre, suf = _chunk_scans(jax.nn.log_sigmoid(raw), 1)
    rid = lax.broadcasted_iota(jnp.int32, (N_CHUNKS * 16, 1), 0) % 16
    scanned = jnp.where(rid < 8 + N_HEADS, pre, suf)
    beta = raw - pltpu.roll(scanned, N_CHUNKS * 16 - 8, 0)
    lrow_s[...] = jnp.where(rid < 8, beta, scanned).reshape(N_CHUNKS, 16, CHUNK)

    for c in range(N_CHUNKS):
        rows = slice(c * CHUNK, (c + 1) * CHUNK)
        qt_s[c] = qkv_ref[0, rows, 0:GROUP_W].astype(F32).T.astype(BF16)
        vt_s[c] = qkv_ref[0, rows, 2 * GROUP_W:3 * GROUP_W].astype(F32).T.astype(BF16)

    si = lax.broadcasted_iota(jnp.int32, (CHUNK, CHUNK), 0)
    ti = lax.broadcasted_iota(jnp.int32, (CHUNK, CHUNK), 1)
    pad = ML_STATE_ROWS - HEAD_DIM
    one_rows = (lax.broadcasted_iota(jnp.int32, (pad, CHUNK), 0) == 0).astype(BF16)
    k_scale = HEAD_DIM ** -0.5

    for d in range(2):
        mask = (si <= ti) if d == 0 else (si >= ti)
        end = CHUNK - 1 if d == 0 else 0
        c_s[...] = jnp.zeros_like(c_s)
        m_s[...] = jnp.zeros_like(m_s)

        def chunk_body(i, _, d=d, mask=mask, end=end):
            c = i if d == 0 else N_CHUNKS - 1 - i
            r0 = pl.multiple_of(c * CHUNK, CHUNK)
            bcol = bcol_s[pl.ds(r0, CHUNK), :]
            lrow = lrow_s[c]
            hs = []
            for h in range(N_HEADS):
                j = d * N_HEADS + h
                k = qkv_ref[0, pl.ds(r0, CHUNK), GROUP_W + h * HEAD_DIM:GROUP_W + (h + 1) * HEAD_DIM]
                q_t = qt_s[c, h * HEAD_DIM:(h + 1) * HEAD_DIM, :]
                v_t = vt_s[c, h * HEAD_DIM:(h + 1) * HEAD_DIM, :]
                beta_r = lrow[j:j + 1, :]
                b_r = lrow[8 + j:8 + j + 1, :]
                m_prev = m_s[h][0:1, 0:1]
                c_ext = c_s[h]
                beta_m = jnp.where(mask, bcol[:, SM_IG + j:SM_IG + j + 1], -jnp.inf)
                mu = jnp.maximum(m_prev, jnp.max(beta_m, axis=0, keepdims=True))
                w_inter = jnp.exp(m_prev - mu)
                s_t = _dot(k, q_t) * (k_scale * jnp.exp(beta_m - mu))
                inter = _dot(c_ext.astype(BF16), q_t)
                num = w_inter * inter[0:HEAD_DIM, :] + _dot(v_t, s_t.astype(BF16))
                den = w_inter * inter[HEAD_DIM:HEAD_DIM + 1, :] + jnp.sum(s_t, axis=0, keepdims=True)
                hs.append(num / jnp.maximum(jnp.abs(den), jnp.exp(-(b_r + mu))))
                total = b_r[:, end:end + 1]
                top = jnp.maximum(m_prev, jnp.max(beta_r, axis=1, keepdims=True))
                w_src = jnp.exp(beta_r - top) * k_scale
                v_ext = jnp.concatenate([v_t, one_rows], axis=0).astype(F32)
                c_s[h] = jnp.exp(m_prev - top) * c_ext + _dot((v_ext * w_src).astype(BF16), k)
                m_s[h] = jnp.broadcast_to(total + top, (8, CHUNK))
            h_t = jnp.concatenate(hs, axis=0)
            if d == 0:
                ht_s[c] = h_t
            else:
                ht_s[c] += h_t
            return 0

        lax.fori_loop(0, N_CHUNKS, chunk_body, 0, unroll=4)

    gain = g_ref[...]

    def out_body(c, _):
        r0 = pl.multiple_of(c * CHUNK, CHUNK)
        h_t = ht_s[c]
        normed = []
        for h in range(N_HEADS):
            x = h_t[h * HEAD_DIM:(h + 1) * HEAD_DIM, :]
            normed.append(x * lax.rsqrt(jnp.mean(x * x, axis=0, keepdims=True) + NORM_EPS))
        y = jnp.concatenate(normed, axis=0).T * gain
        o_ref[0, pl.ds(r0, CHUNK), :] = jax.nn.sigmoid(og_ref[0, pl.ds(r0, CHUNK), :]) * y
        return 0

    lax.fori_loop(0, N_CHUNKS, out_body, 0)


def _ml_mixer(qkv, og, sm, i_bias, f_bias, norm_g):
    b = qkv.shape[0]
    bias16 = jnp.concatenate([i_bias.reshape(8), f_bias.reshape(8)])
    brow = jnp.zeros((1, CHUNK), F32).at[0, SM_IG:SM_IG + 16].set(bias16)
    bcol = jnp.tile(bias16.reshape(16, 1), (N_CHUNKS, 1))
    full = lambda shape: pl.BlockSpec(shape, lambda i: (0,) * len(shape))
    return pl.pallas_call(
        _ml_kernel,
        grid=(b,),
        in_specs=[pl.BlockSpec((1, SEQ, 3 * GROUP_W), lambda i: (i, 0, 0)),
                  pl.BlockSpec((1, SEQ, GROUP_W), lambda i: (i, 0, 0)),
                  pl.BlockSpec((1, SEQ, CHUNK), lambda i: (i, 0, 0)),
                  full((1, CHUNK)), full((N_CHUNKS * 16, 1)), full((1, GROUP_W))],
        out_specs=pl.BlockSpec((1, SEQ, GROUP_W), lambda i: (i, 0, 0)),
        out_shape=jax.ShapeDtypeStruct((b, SEQ, GROUP_W), F32),
        scratch_shapes=[pltpu.VMEM((SEQ, CHUNK), F32), pltpu.VMEM((N_CHUNKS, 16, CHUNK), F32),
                        pltpu.VMEM((N_CHUNKS, 16, CHUNK), F32), pltpu.VMEM((N_CHUNKS, GROUP_W, CHUNK), BF16),
                        pltpu.VMEM((N_CHUNKS, GROUP_W, CHUNK), BF16), pltpu.VMEM((N_CHUNKS, GROUP_W, CHUNK), F32),
                        pltpu.VMEM((N_HEADS, ML_STATE_ROWS, HEAD_DIM), F32), pltpu.VMEM((N_HEADS, 8, CHUNK), F32)],
        compiler_params=_params("arbitrary"),
        name="mlstm_mixer",
    )(qkv, og, sm, brow, bcol, norm_g.reshape(1, GROUP_W))


def _dft_tables():
    f = np.arange(SEQ, dtype=np.int64)[:, None]
    t = np.arange(SEQ, dtype=np.int64)[None, :]
    ang = (2.0 * np.pi / FFT_N) * ((f * t) % FFT_N).astype(np.float64)
    wc = np.cos(ang)
    ws = np.sin(ang)
    ws[0, :] = np.where(np.arange(SEQ) % 2 == 0, 1.0, -1.0)
    w = np.concatenate([wc, ws], axis=0).astype(np.float32)
    return jnp.asarray(w).astype(BF16), jnp.asarray(np.ascontiguousarray(w.T)).astype(BF16)


def _hy_features():
    t = jnp.linspace(0.0, 1.0, SEQ, dtype=F32)[:, None]
    bands = (HY_EMB - 1) // 2
    f = jnp.linspace(1e-4, bands - 1, bands, dtype=F32)
    ang = (2.0 * math.pi) * (jnp.arange(SEQ, dtype=F32) / SEQ)[:, None] * f[None, :]
    feats = jnp.concatenate([t, jnp.cos(ang), -jnp.sin(ang)], axis=-1)
    deltas = jnp.abs(jnp.linspace(math.log(1e-2) / 1.5, math.log(1e-2) / 0.3, GROUP_W, dtype=F32))
    return jnp.pad(feats, ((0, 0), (0, CHUNK - HY_EMB))), t, deltas.reshape(1, GROUP_W)


def _hy_filter_kernel(feats_ref, w1_ref, b1_ref, w2_ref, b2_ref, w3_ref, freq_ref, t_ref, dl_ref, hp_ref, hm_ref):
    h = jnp.sin(freq_ref[0:1, :] * (_dot(feats_ref[...].astype(BF16), w1_ref[...].astype(BF16)) + b1_ref[...]))
    h = jnp.sin(freq_ref[1:2, :] * (_dot(h.astype(BF16), w2_ref[...].astype(BF16)) + b2_ref[...]))
    h = _dot(h.astype(BF16), w3_ref[...].astype(BF16))
    win = jnp.exp(-t_ref[...] * dl_ref[...])
    row = lax.broadcasted_iota(jnp.int32, (SEQ, 1), 0)
    for o in range(2):
        hf = h[:, (2 * o) * GROUP_W:(2 * o + 1) * GROUP_W] * win
        hb = h[:, (2 * o + 1) * GROUP_W:(2 * o + 2) * GROUP_W] * win
        r = lax.rsqrt(jnp.sum(hf * hf + hb * hb, axis=0, keepdims=True) + NORM_EPS)
        hf = hf * r
        hb = jnp.where(row == 0, 0.0, hb * r)
        hp_ref[:, o * GROUP_W:(o + 1) * GROUP_W] = (hf + hb).astype(BF16)
        hm_ref[:, o * GROUP_W:(o + 1) * GROUP_W] = (hf - hb).astype(BF16)


def _hy_spec_kernel(wc_ref, ws_ref, hp_ref, hm_ref, kc_ref, ks_ref):
    rowg = lax.broadcasted_iota(jnp.int32, (FREQ_BLOCK, 1), 0) + pl.program_id(0) * FREQ_BLOCK
    scale = jnp.where(rowg == 0, 1.0 / FFT_N, 2.0 / FFT_N)
    kc_ref[...] = _dot(wc_ref[...], hp_ref[...]) * scale
    nyq = _dot(ws_ref[0:8, :], hp_ref[...])[0:1, :]
    ks_ref[...] = jnp.where(rowg == 0, nyq, _dot(ws_ref[...], hm_ref[...])) * scale


def _hy_spectrum(w, w1, b1, w2, b2, w3, freq):
    feats, t, deltas = _hy_features()
    w1p = jnp.pad(w1, ((0, CHUNK - HY_EMB), (0, 0)))
    hp, hm = pl.pallas_call(
        _hy_filter_kernel,
        out_shape=[jax.ShapeDtypeStruct((SEQ, 2 * GROUP_W), BF16)] * 2,
        compiler_params=pltpu.CompilerParams(vmem_limit_bytes=VMEM_LIMIT),
        name="hyena_filter",
    )(feats, w1p, b1.reshape(1, HY_FFN), w2, b2.reshape(1, HY_FFN), w3, freq, t, deltas)
    nj = N_FREQ_BLOCKS
    return pl.pallas_call(
        _hy_spec_kernel,
        grid=(nj,),
        in_specs=[pl.BlockSpec((FREQ_BLOCK, SEQ), lambda j: (j, 0)),
                  pl.BlockSpec((FREQ_BLOCK, SEQ), lambda j: (j + nj, 0)),
                  pl.BlockSpec((SEQ, 2 * GROUP_W), lambda j: (0, 0)),
                  pl.BlockSpec((SEQ, 2 * GROUP_W), lambda j: (0, 0))],
        out_specs=[pl.BlockSpec((FREQ_BLOCK, 2 * GROUP_W), lambda j: (j, 0))] * 2,
        out_shape=[jax.ShapeDtypeStruct((SEQ, 2 * GROUP_W), F32)] * 2,
        compiler_params=_params("arbitrary"),
        name="hyena_spectrum",
    )(w, w, hp, hm)


def _hy_short_conv_kernel(u_ref, cw_ref, cb_ref, o_ref):
    cw = cw_ref[...]
    cb = cb_ref[...]

    def body(c, _):
        r0 = pl.multiple_of(c * CHUNK, CHUNK)
        o_ref[0, pl.ds(r0, CHUNK), :] = _dwconv3_rows(u_ref, r0, slice(0, 3 * GROUP_W), cw, cb)
        return 0

    lax.fori_loop(0, N_CHUNKS, body, 0)


def _hy_short_conv(u, conv_w, conv_b):
    b = u.shape[0]
    return pl.pallas_call(
        _hy_short_conv_kernel,
        grid=(b,),
        in_specs=[pl.BlockSpec((1, SEQ, 3 * GROUP_W), lambda i: (i, 0, 0)),
                  pl.BlockSpec((3, 3 * GROUP_W), lambda i: (0, 0)),
                  pl.BlockSpec((1, 3 * GROUP_W), lambda i: (0, 0))],
        out_specs=pl.BlockSpec((1, SEQ, 3 * GROUP_W), lambda i: (i, 0, 0)),
        out_shape=jax.ShapeDtypeStruct(u.shape, F32),
        compiler_params=_params("arbitrary"),
        name="hyena_short_conv",
    )(u, conv_w, conv_b.reshape(1, 3 * GROUP_W))


def _hy_conv_kernel(z_ref, gate_ref, wc_ref, ws_ref, wct_ref, wst_ref, kc_ref, ks_ref, skip_ref, g_ref, o_ref, zb_s,
                    *, final_norm):
    j = pl.program_id(1)
    n_b = z_ref.shape[0]

    @pl.when(j == 0)
    def _():
        zb_s[...] = z_ref[...].astype(BF16)
        o_ref[...] = jnp.zeros_like(o_ref)

    row0 = (lax.broadcasted_iota(jnp.int32, (FREQ_BLOCK, 1), 0) + j * FREQ_BLOCK) == 0
    kc = kc_ref[...]
    ks = ks_ref[...]
    for b in range(n_b):
        zb = zb_s[b]
        xc = _dot(wc_ref[...], zb)
        xs = _dot(ws_ref[...], zb)
        xs_ks = xs * ks
        yc = xc * kc - jnp.where(row0, 0.0, xs_ks)
        ys = jnp.where(row0, xs_ks, xc * ks + xs * kc)
        o_ref[b] += _dot(wct_ref[...], yc.astype(BF16)) + _dot(wst_ref[...], ys.astype(BF16))

    @pl.when(j == pl.num_programs(1) - 1)
    def _():
        for b in range(n_b):
            r = gate_ref[b] * (o_ref[b] + skip_ref[...] * z_ref[b])
            if final_norm:
                r = _rms(r) * g_ref[...]
            o_ref[b] = r


def _hy_long_conv(z, z_col, gates, gate_col, w, wt, kc, ks, order, skip, norm_g, final_norm):
    b = z.shape[0]
    bg = HY_BATCH_BLOCK
    nj = N_FREQ_BLOCKS
    return pl.pallas_call(
        functools.partial(_hy_conv_kernel, final_norm=final_norm),
        grid=(b // bg, nj),
        in_specs=[pl.BlockSpec((bg, SEQ, GROUP_W), lambda g, j: (g, 0, z_col)),
                  pl.BlockSpec((bg, SEQ, GROUP_W), lambda g, j: (g, 0, gate_col)),
                  pl.BlockSpec((FREQ_BLOCK, SEQ), lambda g, j: (j, 0)),
                  pl.BlockSpec((FREQ_BLOCK, SEQ), lambda g, j: (j + nj, 0)),
                  pl.BlockSpec((SEQ, FREQ_BLOCK), lambda g, j: (0, j)),
                  pl.BlockSpec((SEQ, FREQ_BLOCK), lambda g, j: (0, j + nj)),
                  pl.BlockSpec((FREQ_BLOCK, GROUP_W), lambda g, j: (j, order)),
                  pl.BlockSpec((FREQ_BLOCK, GROUP_W), lambda g, j: (j, order)),
                  pl.BlockSpec((1, GROUP_W), lambda g, j: (0, 0)),
                  pl.BlockSpec((1, GROUP_W), lambda g, j: (0, 0))],
        out_specs=pl.BlockSpec((bg, SEQ, GROUP_W), lambda g, j: (g, 0, 0)),
        out_shape=jax.ShapeDtypeStruct((b, SEQ, GROUP_W), F32),
        scratch_shapes=[pltpu.VMEM((bg, SEQ, GROUP_W), BF16)],
        compiler_params=_params("arbitrary", "arbitrary"),
        name="hyena_long_conv",
    )(z, gates, w, w, wt, wt, kc, ks, skip.reshape(1, GROUP_W), norm_g.reshape(1, GROUP_W))


def _hy_mixer(u, w, wt, conv_w, conv_b, w1, b1, w2, b2, w3, freq, skip, norm_g):
    kc, ks = _hy_spectrum(w, w1, b1, w2, b2, w3, freq)
    uc = _hy_short_conv(u, conv_w, conv_b)
    z1 = _hy_long_conv(uc, 0, uc, 1, w, wt, kc, ks, 0, skip[0], norm_g, False)
    return _hy_long_conv(z1, 0, uc, 2, w, wt, kc, ks, 1, skip[1], norm_g, True)


def _out_proj_kernel(x_ref, mod_ref, ya_ref, yb_ref, yc_ref, yd_ref, w_ref, o_ref):
    y = jnp.concatenate([ya_ref[...], yb_ref[...], yc_ref[...], yd_ref[...]], axis=1).astype(BF16)
    o_ref[...] = x_ref[...] + mod_ref[0, 2:3, :] * _dot(y, w_ref[...])


def _out_proj(x2, mod_l, ys, w_out):
    n_tok, d = x2.shape
    tm = TOKEN_TILE
    tiles_per_seq = SEQ // tm
    return pl.pallas_call(
        _out_proj_kernel,
        grid=(n_tok // tm,),
        in_specs=[pl.BlockSpec((tm, d), lambda i: (i, 0)),
                  pl.BlockSpec((1, 6, d), lambda i: (i // tiles_per_seq, 0, 0))]
                 + [pl.BlockSpec((tm, GROUP_W), lambda i: (i, 0))] * 4
                 + [pl.BlockSpec((d, d), lambda i: (0, 0))],
        out_specs=pl.BlockSpec((tm, d), lambda i: (i, 0)),
        out_shape=jax.ShapeDtypeStruct((n_tok, d), F32),
        compiler_params=_params("arbitrary"),
        name="out_proj",
    )(x2, mod_l, *ys, w_out)


FFN_SPLITS = ((0, 1024), (1024, 2048), (2048, FFN_HIDDEN))


def _ffn_kernel(x_ref, mod_ref, wg_ref, wu_ref, wd_ref, gf_ref, o_ref, *, final_norm):
    x = x_ref[...]
    h = (_rms(x) * (1.0 + mod_ref[0, 4:5, :]) + mod_ref[0, 3:4, :]).astype(BF16)
    acc = jnp.zeros_like(x)
    for lo, hi in FFN_SPLITS:
        a = _silu(_dot(h, wg_ref[:, lo:hi])) * _dot(h, wu_ref[:, lo:hi])
        acc = acc + _dot(a.astype(BF16), wd_ref[lo:hi, :])
    y = x + mod_ref[0, 5:6, :] * acc
    if final_norm:
        y = _rms(y) * gf_ref[...]
    o_ref[...] = y


def _ffn(x2, mod_l, wg, wu, wd, final_g, final_norm):
    n_tok, d = x2.shape
    tm = TOKEN_TILE
    tiles_per_seq = SEQ // tm
    resident = lambda shape: pl.BlockSpec(shape, lambda i: (0, 0), pipeline_mode=pl.Buffered(1))
    return pl.pallas_call(
        functools.partial(_ffn_kernel, final_norm=final_norm),
        grid=(n_tok // tm,),
        in_specs=[pl.BlockSpec((tm, d), lambda i: (i, 0)),
                  pl.BlockSpec((1, 6, d), lambda i: (i // tiles_per_seq, 0, 0)),
                  resident((d, FFN_HIDDEN)), resident((d, FFN_HIDDEN)), resident((FFN_HIDDEN, d)),
                  pl.BlockSpec((1, d), lambda i: (0, 0))],
        out_specs=pl.BlockSpec((tm, d), lambda i: (i, 0)),
        out_shape=jax.ShapeDtypeStruct((n_tok, d), F32),
        compiler_params=_params("arbitrary"),
        name="swiglu_ffn",
    )(x2, mod_l, wg, wu, wd, final_g.reshape(1, d))


def _permute_w_in(w):
    cols = [w[:, 0:1024], w[:, 1032:1800], w[:, 1800:2568], w[:, 2568:2824], w[:, 2840:3608],
            w[:, 1024:1032], w[:, 2824:2840], jnp.zeros((w.shape[0], P_PAD - 3608), w.dtype)]
    return jnp.concatenate(cols, axis=1).astype(BF16)


def kernel(x, c, mod_w, mod_b, w_in, ssd_conv_w, ssd_conv_b, ssd_dt_bias, ssd_a_log, ssd_d, ssd_norm_g, na_rpb, na_norm_g, ml_i_bias, ml_f_bias, ml_norm_g, hy_conv_w, hy_conv_b, hy_w1, hy_b1, hy_w2, hy_b2, hy_w3, hy_freq, hy_skip, hy_norm_g, w_out, ffn_w_gate, ffn_w_up, ffn_w_down, final_norm_g):
    b, seq, d = x.shape
    assert seq == SEQ and d == D_MODEL and b % HY_BATCH_BLOCK == 0
    depth = mod_w.shape[0]
    mod = _modulation(c, mod_w, mod_b).reshape(depth, b, 6, d)
    dft, dft_t = _dft_tables()
    x2 = x.reshape(b * seq, d)
    for l in range(depth):
        zx, na, mlq, mlo, hy, sm = _in_proj(x2, mod[l], _permute_w_in(w_in[l]))
        seq3 = lambda t: t.reshape(b, seq, t.shape[-1])
        sm3 = seq3(sm)
        y_ssd = _ssd_mixer(seq3(zx), sm3, ssd_conv_w[l], ssd_conv_b[l], ssd_dt_bias[l], ssd_a_log[l], ssd_d[l],
                           ssd_norm_g[l])
        y_na = _na_mixer(seq3(na), na_rpb[l], na_norm_g[l])
        y_ml = _ml_mixer(seq3(mlq), seq3(mlo), sm3, ml_i_bias[l], ml_f_bias[l], ml_norm_g[l])
        y_hy = _hy_mixer(seq3(hy), dft, dft_t, hy_conv_w[l], hy_conv_b[l], hy_w1[l], hy_b1[l], hy_w2[l], hy_b2[l],
                         hy_w3[l], hy_freq[l], hy_skip[l], hy_norm_g[l])
        ys = [t.reshape(b * seq, GROUP_W) for t in (y_ssd, y_na, y_ml, y_hy)]
        x2 = _out_proj(x2, mod[l], ys, w_out[l].astype(BF16))
        x2 = _ffn(x2, mod[l], ffn_w_gate[l].astype(BF16), ffn_w_up[l].astype(BF16), ffn_w_down[l].astype(BF16),
                  final_norm_g, l == depth - 1)
    return x2.reshape(b, seq, d)
```

```python
import functools
import math

import numpy as np
import jax
import jax.numpy as jnp
from jax import lax
from jax.experimental import pallas as pl
from jax.experimental.pallas import tpu as pltpu

F32 = jnp.float32
BF16 = jnp.bfloat16

D_MODEL = 1024
SEQ = 2048
GRID_W = 64
GROUP_W = 256
HEAD_DIM = 64
N_HEADS = 4
SSD_STATE = 128
SSD_XBC = 768
CHUNK = 128
N_CHUNKS = SEQ // CHUNK
NA_KH = 8
NA_KW = 16
NA_QROWS = 4
NA_KROWS = 12
HY_EMB = 33
HY_FFN = 64
FFN_HIDDEN = 2816
NORM_EPS = 1e-6
FFT_N = 2 * SEQ
FREQ_BLOCK = 512
N_FREQ_BLOCKS = SEQ // FREQ_BLOCK
HY_BATCH_BLOCK = 2
TOKEN_TILE = 512
VMEM_LIMIT = 56 * 1024 * 1024

WIDE_W = 2176
WIDE_HY_BLOCK = 0
WIDE_MLO_BLOCK = 3
WIDE_ZX_BLOCK = 1
WIDE_SM_BLOCK = 16
QKV_W = 1536
QKV_NA_BLOCK = 0
QKV_ML_BLOCK = 1
P_PAD = WIDE_W + QKV_W
SM_DT = 0
SM_IG = 8
SM_FG = 16

NT_DIMS = (((1,), (1,)), ((), ()))
TN_DIMS = (((0,), (0,)), ((), ()))


def _rms(x):
    return x * lax.rsqrt(jnp.mean(x * x, axis=-1, keepdims=True) + NORM_EPS)


def _silu(x):
    return x * jax.nn.sigmoid(x)


def _dot(a, b):
    return jnp.dot(a, b, preferred_element_type=F32)


def _params(*sem):
    return pltpu.CompilerParams(dimension_semantics=sem, vmem_limit_bytes=VMEM_LIMIT)


def _mod_kernel(c_ref, w_ref, b_ref, o_ref):
    cond = _silu(c_ref[...]).astype(BF16)
    o_ref[0] = _dot(cond, w_ref[0].astype(BF16)) + b_ref[0]


def _modulation(c, mod_w, mod_b):
    depth, d, n = mod_w.shape
    b = c.shape[0]
    nb = n // d
    return pl.pallas_call(
        _mod_kernel,
        grid=(depth, nb),
        in_specs=[pl.BlockSpec((b, d), lambda l, j: (0, 0)),
                  pl.BlockSpec((1, d, d), lambda l, j: (l, 0, j)),
                  pl.BlockSpec((1, 1, d), lambda l, j: (l, 0, j))],
        out_specs=pl.BlockSpec((1, b, d), lambda l, j: (l, 0, j)),
        out_shape=jax.ShapeDtypeStruct((depth, b, n), F32),
        compiler_params=_params("arbitrary", "arbitrary"),
        name="adaln_mod",
    )(c, mod_w, mod_b.reshape(depth, 1, n))


def _in_proj_kernel(x_ref, mod_ref, w_ref, wide_ref, qkv_ref):
    h = (_rms(x_ref[...]) * (1.0 + mod_ref[0, 1:2, :]) + mod_ref[0, 0:1, :]).astype(BF16)
    wide_ref[...] = _dot(h, w_ref[:, 0:WIDE_W])
    qkv_ref[...] = _dot(h, w_ref[:, WIDE_W:P_PAD]).astype(BF16)


def _in_proj(x2, mod_l, w_perm):
    n_tok, d = x2.shape
    tm = TOKEN_TILE
    tiles_per_seq = SEQ // tm
    return pl.pallas_call(
        _in_proj_kernel,
        grid=(n_tok // tm,),
        in_specs=[pl.BlockSpec((tm, d), lambda i: (i, 0)),
                  pl.BlockSpec((1, 6, d), lambda i: (i // tiles_per_seq, 0, 0)),
                  pl.BlockSpec((d, P_PAD), lambda i: (0, 0), pipeline_mode=pl.Buffered(1))],
        out_specs=[pl.BlockSpec((tm, WIDE_W), lambda i: (i, 0)), pl.BlockSpec((tm, QKV_W), lambda i: (i, 0))],
        out_shape=[jax.ShapeDtypeStruct((n_tok, WIDE_W), F32), jax.ShapeDtypeStruct((n_tok, QKV_W), BF16)],
        compiler_params=_params("arbitrary"),
        name="in_proj",
    )(x2, mod_l, w_perm)


def _dwconv3_rows(src_ref, r0, cols, w, bias):
    x = src_ref[0, pl.ds(r0, CHUNK), cols]
    up = src_ref[0, pl.ds(jnp.maximum(r0 - 1, 0), 1), cols]
    dn = src_ref[0, pl.ds(jnp.minimum(r0 + CHUNK, SEQ - 1), 1), cols]
    up = jnp.where(r0 > 0, up, 0.0)
    dn = jnp.where(r0 + CHUNK < SEQ, dn, 0.0)
    row = lax.broadcasted_iota(jnp.int32, (CHUNK, 1), 0)
    prev = jnp.where(row == 0, up, pltpu.roll(x, 1, 0))
    nxt = jnp.where(row == CHUNK - 1, dn, pltpu.roll(x, CHUNK - 1, 0))
    return w[0:1, :] * prev + w[1:2, :] * x + w[2:3, :] * nxt + bias


def _chunk_scans(a, axis):
    n = a.shape[axis]
    shape = [1, 1]
    shape[axis] = n
    pos = lax.broadcasted_iota(jnp.int32, tuple(shape), axis) % CHUNK
    pre, suf = a, a
    k = 1
    while k < CHUNK:
        pre = pre + jnp.where(pos >= k, pltpu.roll(pre, k, axis), 0.0)
        suf = suf + jnp.where(pos < CHUNK - k, pltpu.roll(suf, n - k, axis), 0.0)
        k *= 2
    return pre, suf


def _gate_rows(sm_ref, rows_s, lo, n):
    for c in range(N_CHUNKS):
        t = sm_ref[0, c * CHUNK:(c + 1) * CHUNK, :].T
        rows_s[c] = t[lo:lo + n, :]


def _ssd_kernel(zx_ref, sm_ref, cw_ref, cb_ref, dtb_row_ref, a_row_ref, dtb_col_ref, a_col_ref, dsk_ref, g_ref,
                o_ref, xs_s, b_s, xst_s, ct_s, acol_s, rows_s, dtrow_s, arow_s, yt_s, st_s):
    cw = cw_ref[...]
    cb = cb_ref[...]

    def conv_body(c, _):
        r0 = pl.multiple_of(c * CHUNK, CHUNK)
        v = _silu(_dwconv3_rows(zx_ref, r0, slice(GROUP_W, GROUP_W + SSD_XBC), cw, cb))
        xs_s[pl.ds(r0, CHUNK), :] = v[:, 0:GROUP_W]
        b_s[pl.ds(r0, CHUNK), :] = v[:, GROUP_W:2 * GROUP_W].astype(BF16)
        xst_s[c] = v[:, 0:GROUP_W].T
        ct_s[c] = v[:, 2 * GROUP_W:3 * GROUP_W].T.astype(BF16)
        return 0

    lax.fori_loop(0, N_CHUNKS, conv_body, 0)

    lane = lax.broadcasted_iota(jnp.int32, (1, CHUNK), 1)
    dt_col = jax.nn.softplus(sm_ref[0] + dtb_row_ref[...])
    pre, suf = _chunk_scans(dt_col * (-jnp.exp(a_row_ref[...])), 0)
    acol_s[...] = jnp.where(lane < SM_DT + N_HEADS, pre, suf)

    _gate_rows(sm_ref, rows_s, SM_DT, 8)
    dt_row = jax.nn.softplus(rows_s[...].reshape(N_CHUNKS * 8, CHUNK) + dtb_col_ref[...])
    pre, suf = _chunk_scans(dt_row * (-jnp.exp(a_col_ref[...])), 1)
    rid = lax.broadcasted_iota(jnp.int32, (N_CHUNKS * 8, 1), 0) % 8
    dtrow_s[...] = dt_row.reshape(N_CHUNKS, 8, CHUNK)
    arow_s[...] = jnp.where(rid < N_HEADS, pre, suf).reshape(N_CHUNKS, 8, CHUNK)

    si = lax.broadcasted_iota(jnp.int32, (CHUNK, CHUNK), 0)
    ti = lax.broadcasted_iota(jnp.int32, (CHUNK, CHUNK), 1)

    for d in range(2):
        mask = (si <= ti) if d == 0 else (si >= ti)
        end = CHUNK - 1 if d == 0 else 0
        st_s[...] = jnp.zeros_like(st_s)

        def chunk_body(i, _, d=d, mask=mask, end=end):
            c = i if d == 0 else N_CHUNKS - 1 - i
            r0 = pl.multiple_of(c * CHUNK, CHUNK)
            bm = b_s[pl.ds(r0, CHUNK), :]
            acol = acol_s[pl.ds(r0, CHUNK), :]
            arow = arow_s[c]
            dtr = dtrow_s[c]
            ys = []
            for g in range(2):
                bg = bm[:, g * SSD_STATE:(g + 1) * SSD_STATE]
                cg_t = ct_s[c, g * SSD_STATE:(g + 1) * SSD_STATE, :]
                scores = _dot(bg, cg_t)
                for hh in range(2):
                    h = 2 * g + hh
                    j = d * N_HEADS + h
                    ar = arow[j:j + 1, :]
                    a_end = ar[:, end:end + 1]
                    decay = jnp.exp(jnp.where(mask, ar - acol[:, j:j + 1], -jnp.inf))
                    xdt_t = xst_s[c, h * HEAD_DIM:(h + 1) * HEAD_DIM, :] * dtr[j:j + 1, :]
                    prev = st_s[h]
                    y = _dot(xdt_t.astype(BF16), (scores * decay).astype(BF16))
                    y = y + _dot(prev.astype(BF16), cg_t) * jnp.exp(ar)
                    x_end = (xdt_t * jnp.exp(a_end - ar)).astype(BF16)
                    st_s[h] = prev * jnp.exp(a_end) + _dot(x_end, bg)
                    ys.append(y)
            y_t = jnp.concatenate(ys, axis=0)
            if d == 0:
                yt_s[c] = y_t
            else:
                yt_s[c] += y_t
            return 0

        lax.fori_loop(0, N_CHUNKS, chunk_body, 0, unroll=4)

    dsk = dsk_ref[...]
    gain = g_ref[...]

    def out_body(c, _):
        r0 = pl.multiple_of(c * CHUNK, CHUNK)
        y = yt_s[c].T + dsk * xs_s[pl.ds(r0, CHUNK), :]
        y = y * _silu(zx_ref[0, pl.ds(r0, CHUNK), 0:GROUP_W])
        o_ref[0, pl.ds(r0, CHUNK), :] = _rms(y) * gain
        return 0

    lax.fori_loop(0, N_CHUNKS, out_body, 0)


def _ssd_mixer(zx, sm, conv_w, conv_b, dt_bias, a_log, d_skip, norm_g):
    b = zx.shape[0]
    pad_row = lambda v: jnp.zeros((1, CHUNK), F32).at[0, SM_DT:SM_DT + 8].set(v.reshape(8))
    tile_col = lambda v: jnp.tile(v.reshape(8, 1), (N_CHUNKS, 1))
    full = lambda shape: pl.BlockSpec(shape, lambda i: (0,) * len(shape))
    return pl.pallas_call(
        _ssd_kernel,
        grid=(b,),
        in_specs=[pl.BlockSpec((1, SEQ, 1024), lambda i: (i, 0, WIDE_ZX_BLOCK)),
                  pl.BlockSpec((1, SEQ, CHUNK), lambda i: (i, 0, WIDE_SM_BLOCK)),
                  full((3, SSD_XBC)), full((1, SSD_XBC)), full((1, CHUNK)), full((1, CHUNK)),
                  full((N_CHUNKS * 8, 1)), full((N_CHUNKS * 8, 1)), full((1, GROUP_W)), full((1, GROUP_W))],
        out_specs=pl.BlockSpec((1, SEQ, GROUP_W), lambda i: (i, 0, 0)),
        out_shape=jax.ShapeDtypeStruct((b, SEQ, GROUP_W), F32),
        scratch_shapes=[pltpu.VMEM((SEQ, GROUP_W), F32), pltpu.VMEM((SEQ, GROUP_W), BF16),
                        pltpu.VMEM((N_CHUNKS, GROUP_W, CHUNK), F32), pltpu.VMEM((N_CHUNKS, GROUP_W, CHUNK), BF16),
                        pltpu.VMEM((SEQ, CHUNK), F32), pltpu.VMEM((N_CHUNKS, 8, CHUNK), F32),
                        pltpu.VMEM((N_CHUNKS, 8, CHUNK), F32), pltpu.VMEM((N_CHUNKS, 8, CHUNK), F32),
                        pltpu.VMEM((N_CHUNKS, GROUP_W, CHUNK), F32), pltpu.VMEM((N_HEADS, HEAD_DIM, SSD_STATE), F32)],
        compiler_params=_params("arbitrary"),
        name="ssd_mixer",
    )(zx, sm, conv_w, conv_b.reshape(1, SSD_XBC), pad_row(dt_bias), pad_row(a_log), tile_col(dt_bias), tile_col(a_log),
      jnp.repeat(d_skip, HEAD_DIM).reshape(1, GROUP_W), norm_g.reshape(1, GROUP_W))


def _na_kernel(qkv_ref, bias_ref, g_ref, o_ref):
    q_rows = NA_QROWS * GRID_W
    k_rows = NA_KROWS * GRID_W
    n_blocks = SEQ // q_rows
    pair_w = 2 * HEAD_DIM
    first = lax.broadcasted_iota(jnp.int32, (1, pair_w), 1) < HEAD_DIM
    gain = g_ref[...]

    def block(q0, k0, kind):
        outs = []
        for pair in range(N_HEADS // 2):
            lo = pair * pair_w
            q2 = qkv_ref[0, pl.ds(q0, q_rows), lo:lo + pair_w] * (HEAD_DIM ** -0.5)
            k2 = qkv_ref[0, pl.ds(k0, k_rows), GROUP_W + lo:GROUP_W + lo + pair_w]
            v2 = qkv_ref[0, pl.ds(k0, k_rows), 2 * GROUP_W + lo:2 * GROUP_W + lo + pair_w]
            o2 = []
            for hh in range(2):
                qm = jnp.where(first if hh == 0 else ~first, q2, 0.0).astype(BF16)
                s = lax.dot_general(qm, k2, NT_DIMS, preferred_element_type=F32) + bias_ref[2 * pair + hh, kind]
                p = jnp.exp(s - jnp.max(s, axis=-1, keepdims=True))
                o2.append(_dot(p.astype(BF16), v2) / jnp.sum(p, axis=-1, keepdims=True))
            outs.append(jnp.where(first, o2[0], o2[1]))
        o_ref[0, pl.ds(q0, q_rows), :] = _rms(jnp.concatenate(outs, axis=1)) * gain

    block(0, 0, 0)

    def interior(i, _):
        block(pl.multiple_of(i * q_rows, q_rows), pl.multiple_of((i - 1) * q_rows, q_rows), 1)
        return 0

    lax.fori_loop(1, n_blocks - 1, interior, 0, unroll=2)
    block(SEQ - q_rows, SEQ - k_rows, 2)


def _na_bias_table(rpb):
    rows = SEQ // GRID_W
    col = np.arange(GRID_W)
    cs = np.clip(col - NA_KW // 2, 0, GRID_W - NA_KW)
    valid_c = (col[None, :] >= cs[:, None]) & (col[None, :] < cs[:, None] + NA_KW)
    dc = col[None, :] - col[:, None] + NA_KW - 1
    sel_c = (valid_c[None] & (dc[None] == np.arange(2 * NA_KW - 1)[:, None, None])).astype(np.float32)
    sel_r = np.zeros((3, NA_QROWS, NA_KROWS, 2 * NA_KH - 1), np.float32)
    for kind, (r0, ks) in enumerate(((0, 0), (NA_QROWS, 0), (rows - NA_QROWS, rows - NA_KROWS))):
        for i in range(NA_QROWS):
            start = min(max(r0 + i - NA_KH // 2, 0), rows - NA_KH)
            for kr in range(start, start + NA_KH):
                sel_r[kind, i, kr - ks, kr - (r0 + i) + NA_KH - 1] = 1.0
    valid = sel_r.sum(-1)[:, :, None, :, None] * valid_c[None, None, :, None, :]
    bias = jnp.einsum("hab,tija,bqk->htiqjk", rpb, sel_r, sel_c, precision=lax.Precision.HIGHEST)
    bias = jnp.where(valid[None] > 0, bias, -1e30)
    return bias.reshape(N_HEADS, 3, NA_QROWS * GRID_W, NA_KROWS * GRID_W).astype(F32)


def _na_mixer(qkv, rpb, norm_g):
    b = qkv.shape[0]
    table = _na_bias_table(rpb)
    return pl.pallas_call(
        _na_kernel,
        grid=(b,),
        in_specs=[pl.BlockSpec((1, SEQ, 3 * GROUP_W), lambda i: (i, 0, QKV_NA_BLOCK)),
                  pl.BlockSpec(table.shape, lambda i: (0, 0, 0, 0)),
                  pl.BlockSpec((1, GROUP_W), lambda i: (0, 0))],
        out_specs=pl.BlockSpec((1, SEQ, GROUP_W), lambda i: (i, 0, 0)),
        out_shape=jax.ShapeDtypeStruct((b, SEQ, GROUP_W), F32),
        compiler_params=_params("arbitrary"),
        name="na_mixer",
    )(qkv, table, norm_g.reshape(1, GROUP_W))


ML_STATE_ROWS = 80


def _ml_kernel(qkv_ref, og_ref, sm_ref, brow_ref, bcol_ref, g_ref, o_ref, bcol_s, rows_s, lrow_s, qt_s, vt_s, ht_s,
               c_s, m_s):
    lane = lax.broadcasted_iota(jnp.int32, (1, CHUNK), 1)
    raw = sm_ref[0] + brow_ref[...]
    pre, suf = _chunk_scans(jax.nn.log_sigmoid(raw), 0)
    scanned = jnp.where((lane >= SM_FG) & (lane < SM_FG + N_HEADS), pre, suf)
    bcol_s[...] = raw - pltpu.roll(scanned, CHUNK - (SM_FG - SM_IG), 1)
    _gate_rows(sm_ref, rows_s, SM_IG, 16)
    raw = rows_s[...].reshape(N_CHUNKS * 16, CHUNK) + bcol_ref[...]
    pre, suf = _chunk_scans(jax.nn.log_sigmoid(raw), 1)
    rid = lax.broadcasted_iota(jnp.int32, (N_CHUNKS * 16, 1), 0) % 16
    scanned = jnp.where(rid < 8 + N_HEADS, pre, suf)
    beta = raw - pltpu.roll(scanned, N_CHUNKS * 16 - 8, 0)
    lrow_s[...] = jnp.where(rid < 8, beta, scanned).reshape(N_CHUNKS, 16, CHUNK)

    for c in range(N_CHUNKS):
        rows = slice(c * CHUNK, (c + 1) * CHUNK)
        qt_s[c] = qkv_ref[0, rows, 0:GROUP_W].astype(F32).T.astype(BF16)
        vt_s[c] = qkv_ref[0, rows, 2 * GROUP_W:3 * GROUP_W].astype(F32).T.astype(BF16)

    si = lax.broadcasted_iota(jnp.int32, (CHUNK, CHUNK), 0)
    ti = lax.broadcasted_iota(jnp.int32, (CHUNK, CHUNK), 1)
    pad = ML_STATE_ROWS - HEAD_DIM
    one_rows = (lax.broadcasted_iota(jnp.int32, (pad, CHUNK), 0) == 0).astype(BF16)
    k_scale = HEAD_DIM ** -0.5

    for d in range(2):
        mask = (si <= ti) if d == 0 else (si >= ti)
        end = CHUNK - 1 if d == 0 else 0
        c_s[...] = jnp.zeros_like(c_s)
        m_s[...] = jnp.zeros_like(m_s)

        def chunk_body(i, _, d=d, mask=mask, end=end):
            c = i if d == 0 else N_CHUNKS - 1 - i
            r0 = pl.multiple_of(c * CHUNK, CHUNK)
            bcol = bcol_s[pl.ds(r0, CHUNK), :]
            lrow = lrow_s[c]
            hs = []
            for h in range(N_HEADS):
                j = d * N_HEADS + h
                k = qkv_ref[0, pl.ds(r0, CHUNK), GROUP_W + h * HEAD_DIM:GROUP_W + (h + 1) * HEAD_DIM]
                q_t = qt_s[c, h * HEAD_DIM:(h + 1) * HEAD_DIM, :]
                v_t = vt_s[c, h * HEAD_DIM:(h + 1) * HEAD_DIM, :]
                beta_r = lrow[j:j + 1, :]
                b_r = lrow[8 + j:8 + j + 1, :]
                m_prev = m_s[h][0:1, 0:1]
                c_ext = c_s[h]
                beta_m = jnp.where(mask, bcol[:, SM_IG + j:SM_IG + j + 1], -jnp.inf)
                mu = jnp.maximum(m_prev, jnp.max(beta_m, axis=0, keepdims=True))
                w_inter = jnp.exp(m_prev - mu)
                s_t = _dot(k, q_t) * (k_scale * jnp.exp(beta_m - mu))
                inter = _dot(c_ext.astype(BF16), q_t)
                num = w_inter * inter[0:HEAD_DIM, :] + _dot(v_t, s_t.astype(BF16))
                den = w_inter * inter[HEAD_DIM:HEAD_DIM + 1, :] + jnp.sum(s_t, axis=0, keepdims=True)
                hs.append(num / jnp.maximum(jnp.abs(den), jnp.exp(-(b_r + mu))))
                total = b_r[:, end:end + 1]
                top = jnp.maximum(m_prev, jnp.max(beta_r, axis=1, keepdims=True))
                w_src = jnp.exp(beta_r - top) * k_scale
                v_ext = jnp.concatenate([v_t, one_rows], axis=0).astype(F32)
                c_s[h] = jnp.exp(m_prev - top) * c_ext + _dot((v_ext * w_src).astype(BF16), k)
                m_s[h] = jnp.broadcast_to(total + top, (8, CHUNK))
            h_t = jnp.concatenate(hs, axis=0)
            if d == 0:
                ht_s[c] = h_t
            else:
                ht_s[c] += h_t
            return 0

        lax.fori_loop(0, N_CHUNKS, chunk_body, 0, unroll=4)

    gain = g_ref[...]

    def out_body(c, _):
        r0 = pl.multiple_of(c * CHUNK, CHUNK)
        h_t = ht_s[c]
        normed = []
        for h in range(N_HEADS):
            x = h_t[h * HEAD_DIM:(h + 1) * HEAD_DIM, :]
            normed.append(x * lax.rsqrt(jnp.mean(x * x, axis=0, keepdims=True) + NORM_EPS))
        y = jnp.concatenate(normed, axis=0).T * gain
        o_ref[0, pl.ds(r0, CHUNK), :] = jax.nn.sigmoid(og_ref[0, pl.ds(r0, CHUNK), :]) * y
        return 0

    lax.fori_loop(0, N_CHUNKS, out_body, 0)


def _ml_mixer(qkv, og, sm, i_bias, f_bias, norm_g):
    b = qkv.shape[0]
    bias16 = jnp.concatenate([i_bias.reshape(8), f_bias.reshape(8)])
    brow = jnp.zeros((1, CHUNK), F32).at[0, SM_IG:SM_IG + 16].set(bias16)
    bcol = jnp.tile(bias16.reshape(16, 1), (N_CHUNKS, 1))
    full = lambda shape: pl.BlockSpec(shape, lambda i: (0,) * len(shape))
    return pl.pallas_call(
        _ml_kernel,
        grid=(b,),
        in_specs=[pl.BlockSpec((1, SEQ, 3 * GROUP_W), lambda i: (i, 0, QKV_ML_BLOCK)),
                  pl.BlockSpec((1, SEQ, GROUP_W), lambda i: (i, 0, WIDE_MLO_BLOCK)),
                  pl.BlockSpec((1, SEQ, CHUNK), lambda i: (i, 0, WIDE_SM_BLOCK)),
                  full((1, CHUNK)), full((N_CHUNKS * 16, 1)), full((1, GROUP_W))],
        out_specs=pl.BlockSpec((1, SEQ, GROUP_W), lambda i: (i, 0, 0)),
        out_shape=jax.ShapeDtypeStruct((b, SEQ, GROUP_W), F32),
        scratch_shapes=[pltpu.VMEM((SEQ, CHUNK), F32), pltpu.VMEM((N_CHUNKS, 16, CHUNK), F32),
                        pltpu.VMEM((N_CHUNKS, 16, CHUNK), F32), pltpu.VMEM((N_CHUNKS, GROUP_W, CHUNK), BF16),
                        pltpu.VMEM((N_CHUNKS, GROUP_W, CHUNK), BF16), pltpu.VMEM((N_CHUNKS, GROUP_W, CHUNK), F32),
                        pltpu.VMEM((N_HEADS, ML_STATE_ROWS, HEAD_DIM), F32), pltpu.VMEM((N_HEADS, 8, CHUNK), F32)],
        compiler_params=_params("arbitrary"),
        name="mlstm_mixer",
    )(qkv, og, sm, brow, bcol, norm_g.reshape(1, GROUP_W))


def _dft_tables():
    f = np.arange(SEQ, dtype=np.int64)[:, None]
    t = np.arange(SEQ, dtype=np.int64)[None, :]
    ang = (2.0 * np.pi / FFT_N) * ((f * t) % FFT_N).astype(np.float64)
    wc = np.cos(ang)
    ws = np.sin(ang)
    ws[0, :] = np.where(np.arange(SEQ) % 2 == 0, 1.0, -1.0)
    w = np.stack([wc, ws]).reshape(2, N_FREQ_BLOCKS, FREQ_BLOCK, SEQ).transpose(1, 0, 2, 3)
    w = np.ascontiguousarray(w.reshape(FFT_N, SEQ)).astype(np.float32)
    return jnp.asarray(w).astype(BF16), jnp.asarray(np.ascontiguousarray(w.T)).astype(BF16)


def _hy_features():
    t = jnp.linspace(0.0, 1.0, SEQ, dtype=F32)[:, None]
    bands = (HY_EMB - 1) // 2
    f = jnp.linspace(1e-4, bands - 1, bands, dtype=F32)
    ang = (2.0 * math.pi) * (jnp.arange(SEQ, dtype=F32) / SEQ)[:, None] * f[None, :]
    feats = jnp.concatenate([t, jnp.cos(ang), -jnp.sin(ang)], axis=-1)
    deltas = jnp.abs(jnp.linspace(math.log(1e-2) / 1.5, math.log(1e-2) / 0.3, GROUP_W, dtype=F32))
    return jnp.pad(feats, ((0, 0), (0, CHUNK - HY_EMB))), t, deltas.reshape(1, GROUP_W)


def _hy_filter_kernel(feats_ref, w1_ref, b1_ref, w2_ref, b2_ref, w3_ref, freq_ref, t_ref, dl_ref, hp_ref, hm_ref):
    h = jnp.sin(freq_ref[0:1, :] * (_dot(feats_ref[...].astype(BF16), w1_ref[...].astype(BF16)) + b1_ref[...]))
    h = jnp.sin(freq_ref[1:2, :] * (_dot(h.astype(BF16), w2_ref[...].astype(BF16)) + b2_ref[...]))
    h = _dot(h.astype(BF16), w3_ref[...].astype(BF16))
    win = jnp.exp(-t_ref[...] * dl_ref[...])
    row = lax.broadcasted_iota(jnp.int32, (SEQ, 1), 0)
    for o in range(2):
        hf = h[:, (2 * o) * GROUP_W:(2 * o + 1) * GROUP_W] * win
        hb = h[:, (2 * o + 1) * GROUP_W:(2 * o + 2) * GROUP_W] * win
        r = lax.rsqrt(jnp.sum(hf * hf + hb * hb, axis=0, keepdims=True) + NORM_EPS)
        hf = hf * r
        hb = jnp.where(row == 0, 0.0, hb * r)
        hp_ref[:, o * GROUP_W:(o + 1) * GROUP_W] = (hf + hb).astype(BF16)
        hm_ref[:, o * GROUP_W:(o + 1) * GROUP_W] = (hf - hb).astype(BF16)


def _hy_spec_kernel(wc_ref, ws_ref, hp_ref, hm_ref, kc_ref, ks_ref):
    rowg = lax.broadcasted_iota(jnp.int32, (FREQ_BLOCK, 1), 0) + pl.program_id(0) * FREQ_BLOCK
    scale = jnp.where(rowg == 0, 1.0 / FFT_N, 2.0 / FFT_N)
    kc_ref[...] = _dot(wc_ref[...], hp_ref[...]) * scale
    nyq = _dot(ws_ref[0:8, :], hp_ref[...])[0:1, :]
    ks_ref[...] = jnp.where(rowg == 0, nyq, _dot(ws_ref[...], hm_ref[...])) * scale


def _hy_spectrum(w, w1, b1, w2, b2, w3, freq):
    feats, t, deltas = _hy_features()
    w1p = jnp.pad(w1, ((0, CHUNK - HY_EMB), (0, 0)))
    hp, hm = pl.pallas_call(
        _hy_filter_kernel,
        out_shape=[jax.ShapeDtypeStruct((SEQ, 2 * GROUP_W), BF16)] * 2,
        compiler_params=pltpu.CompilerParams(vmem_limit_bytes=VMEM_LIMIT),
        name="hyena_filter",
    )(feats, w1p, b1.reshape(1, HY_FFN), w2, b2.reshape(1, HY_FFN), w3, freq, t, deltas)
    nj = N_FREQ_BLOCKS
    return pl.pallas_call(
        _hy_spec_kernel,
        grid=(nj,),
        in_specs=[pl.BlockSpec((FREQ_BLOCK, SEQ), lambda j: (2 * j, 0)),
                  pl.BlockSpec((FREQ_BLOCK, SEQ), lambda j: (2 * j + 1, 0)),
                  pl.BlockSpec((SEQ, 2 * GROUP_W), lambda j: (0, 0)),
                  pl.BlockSpec((SEQ, 2 * GROUP_W), lambda j: (0, 0))],
        out_specs=[pl.BlockSpec((FREQ_BLOCK, 2 * GROUP_W), lambda j: (j, 0))] * 2,
        out_shape=[jax.ShapeDtypeStruct((SEQ, 2 * GROUP_W), F32)] * 2,
        compiler_params=_params("arbitrary"),
        name="hyena_spectrum",
    )(w, w, hp, hm)


def _hy_short_conv_kernel(u_ref, cw_ref, cb_ref, o_ref):
    cw = cw_ref[...]
    cb = cb_ref[...]

    def body(c, _):
        r0 = pl.multiple_of(c * CHUNK, CHUNK)
        o_ref[0, pl.ds(r0, CHUNK), :] = _dwconv3_rows(u_ref, r0, slice(0, 3 * GROUP_W), cw, cb)
        return 0

    lax.fori_loop(0, N_CHUNKS, body, 0)


def _hy_short_conv(u, conv_w, conv_b):
    b = u.shape[0]
    return pl.pallas_call(
        _hy_short_conv_kernel,
        grid=(b,),
        in_specs=[pl.BlockSpec((1, SEQ, 3 * GROUP_W), lambda i: (i, 0, WIDE_HY_BLOCK)),
                  pl.BlockSpec((3, 3 * GROUP_W), lambda i: (0, 0)),
                  pl.BlockSpec((1, 3 * GROUP_W), lambda i: (0, 0))],
        out_specs=pl.BlockSpec((1, SEQ, 3 * GROUP_W), lambda i: (i, 0, 0)),
        out_shape=jax.ShapeDtypeStruct((b, SEQ, 3 * GROUP_W), F32),
        compiler_params=_params("arbitrary"),
        name="hyena_short_conv",
    )(u, conv_w, conv_b.reshape(1, 3 * GROUP_W))


def _hy_conv_kernel(z_ref, gate_ref, w_ref, wt_ref, kc_ref, ks_ref, skip_ref, g_ref, o_ref, zb_s, acc_s, *, final_norm):
    j = pl.program_id(1)
    n_b = z_ref.shape[0]

    @pl.when(j == 0)
    def _():
        for b in range(n_b):
            zb_s[:, b * GROUP_W:(b + 1) * GROUP_W] = z_ref[b].astype(BF16)
        acc_s[...] = jnp.zeros_like(acc_s)

    row0 = (lax.broadcasted_iota(jnp.int32, (FREQ_BLOCK, 1), 0) + j * FREQ_BLOCK) == 0
    kc = jnp.concatenate([kc_ref[...]] * n_b, axis=1)
    ks = jnp.concatenate([ks_ref[...]] * n_b, axis=1)
    x = _dot(w_ref[...], zb_s[...])
    xc = x[0:FREQ_BLOCK, :]
    xs = x[FREQ_BLOCK:, :]
    xs_ks = xs * ks
    yc = xc * kc - jnp.where(row0, 0.0, xs_ks)
    ys = jnp.where(row0, xs_ks, xc * ks + xs * kc)
    acc_s[...] += _dot(wt_ref[...], jnp.concatenate([yc, ys], axis=0).astype(BF16))

    @pl.when(j == pl.num_programs(1) - 1)
    def _():
        for b in range(n_b):
            r = gate_ref[b] * (acc_s[:, b * GROUP_W:(b + 1) * GROUP_W] + skip_ref[...] * z_ref[b])
            if final_norm:
                r = _rms(r) * g_ref[...]
            o_ref[b] = r


def _hy_long_conv(z, z_col, gates, gate_col, w, wt, kc, ks, order, skip, norm_g, final_norm):
    b = z.shape[0]
    bg = HY_BATCH_BLOCK
    nj = N_FREQ_BLOCKS
    return pl.pallas_call(
        functools.partial(_hy_conv_kernel, final_norm=final_norm),
        grid=(b // bg, nj),
        in_specs=[pl.BlockSpec((bg, SEQ, GROUP_W), lambda g, j: (g, 0, z_col)),
                  pl.BlockSpec((bg, SEQ, GROUP_W), lambda g, j: (g, 0, gate_col)),
                  pl.BlockSpec((2 * FREQ_BLOCK, SEQ), lambda g, j: (j, 0)),
                  pl.BlockSpec((SEQ, 2 * FREQ_BLOCK), lambda g, j: (0, j)),
                  pl.BlockSpec((FREQ_BLOCK, GROUP_W), lambda g, j: (j, order)),
                  pl.BlockSpec((FREQ_BLOCK, GROUP_W), lambda g, j: (j, order)),
                  pl.BlockSpec((1, GROUP_W), lambda g, j: (0, 0)),
                  pl.BlockSpec((1, GROUP_W), lambda g, j: (0, 0))],
        out_specs=pl.BlockSpec((bg, SEQ, GROUP_W), lambda g, j: (g, 0, 0)),
        out_shape=jax.ShapeDtypeStruct((b, SEQ, GROUP_W), F32),
        scratch_shapes=[pltpu.VMEM((SEQ, bg * GROUP_W), BF16), pltpu.VMEM((SEQ, bg * GROUP_W), F32)],
        compiler_params=_params("arbitrary", "arbitrary"),
        name="hyena_long_conv",
    )(z, gates, w, wt, kc, ks, skip.reshape(1, GROUP_W), norm_g.reshape(1, GROUP_W))


def _hy_mixer(u, w, wt, conv_w, conv_b, w1, b1, w2, b2, w3, freq, skip, norm_g):
    kc, ks = _hy_spectrum(w, w1, b1, w2, b2, w3, freq)
    uc = _hy_short_conv(u, conv_w, conv_b)
    z1 = _hy_long_conv(uc, 0, uc, 1, w, wt, kc, ks, 0, skip[0], norm_g, False)
    return _hy_long_conv(z1, 0, uc, 2, w, wt, kc, ks, 1, skip[1], norm_g, True)


FFN_SPLITS = ((0, 1024), (1024, 2048), (2048, FFN_HIDDEN))


def _ffn_kernel(x_ref, mod_ref, ya_ref, yb_ref, yc_ref, yd_ref, wo_ref, wg_ref, wu_ref, wd_ref, gf_ref, o_ref, *,
                final_norm):
    y_mix = jnp.concatenate([ya_ref[...], yb_ref[...], yc_ref[...], yd_ref[...]], axis=1).astype(BF16)
    x = x_ref[...] + mod_ref[0, 2:3, :] * _dot(y_mix, wo_ref[...])
    h = (_rms(x) * (1.0 + mod_ref[0, 4:5, :]) + mod_ref[0, 3:4, :]).astype(BF16)
    acc = jnp.zeros_like(x)
    for lo, hi in FFN_SPLITS:
        a = _silu(_dot(h, wg_ref[:, lo:hi])) * _dot(h, wu_ref[:, lo:hi])
        acc = acc + _dot(a.astype(BF16), wd_ref[lo:hi, :])
    y = x + mod_ref[0, 5:6, :] * acc
    if final_norm:
        y = _rms(y) * gf_ref[...]
    o_ref[...] = y


def _out_ffn(x2, mod_l, ys, w_out, wg, wu, wd, final_g, final_norm):
    n_tok, d = x2.shape
    tm = TOKEN_TILE
    tiles_per_seq = SEQ // tm
    resident = lambda shape: pl.BlockSpec(shape, lambda i: (0, 0), pipeline_mode=pl.Buffered(1))
    return pl.pallas_call(
        functools.partial(_ffn_kernel, final_norm=final_norm),
        grid=(n_tok // tm,),
        in_specs=[pl.BlockSpec((tm, d), lambda i: (i, 0)),
                  pl.BlockSpec((1, 6, d), lambda i: (i // tiles_per_seq, 0, 0))]
                 + [pl.BlockSpec((tm, GROUP_W), lambda i: (i, 0))] * 4
                 + [resident((d, d)), resident((d, FFN_HIDDEN)), resident((d, FFN_HIDDEN)), resident((FFN_HIDDEN, d)),
                    pl.BlockSpec((1, d), lambda i: (0, 0))],
        out_specs=pl.BlockSpec((tm, d), lambda i: (i, 0)),
        out_shape=jax.ShapeDtypeStruct((n_tok, d), F32),
        compiler_params=_params("arbitrary"),
        name="out_proj_swiglu",
    )(x2, mod_l, *ys, w_out, wg, wu, wd, final_g.reshape(1, d))


def _permute_w_in(w):
    cols = [w[:, 2840:3608], w[:, 2568:2824], w[:, 0:1024], w[:, 1024:1032], w[:, 2824:2840],
            jnp.zeros((w.shape[0], P_PAD - 3608), w.dtype), w[:, 1032:1800], w[:, 1800:2568]]
    return jnp.concatenate(cols, axis=1).astype(BF16)


def kernel(x, c, mod_w, mod_b, w_in, ssd_conv_w, ssd_conv_b, ssd_dt_bias, ssd_a_log, ssd_d, ssd_norm_g, na_rpb, na_norm_g, ml_i_bias, ml_f_bias, ml_norm_g, hy_conv_w, hy_conv_b, hy_w1, hy_b1, hy_w2, hy_b2, hy_w3, hy_freq, hy_skip, hy_norm_g, w_out, ffn_w_gate, ffn_w_up, ffn_w_down, final_norm_g):
    b, seq, d = x.shape
    assert seq == SEQ and d == D_MODEL and b % HY_BATCH_BLOCK == 0
    depth = mod_w.shape[0]
    mod = _modulation(c, mod_w, mod_b).reshape(depth, b, 6, d)
    dft, dft_t = _dft_tables()
    x2 = x.reshape(b * seq, d)
    for l in range(depth):
        wide, qkv = _in_proj(x2, mod[l], _permute_w_in(w_in[l]))
        wide = wide.reshape(b, seq, WIDE_W)
        qkv = qkv.reshape(b, seq, QKV_W)
        y_ssd = _ssd_mixer(wide, wide, ssd_conv_w[l], ssd_conv_b[l], ssd_dt_bias[l], ssd_a_log[l], ssd_d[l],
                           ssd_norm_g[l])
        y_na = _na_mixer(qkv, na_rpb[l], na_norm_g[l])
        y_ml = _ml_mixer(qkv, wide, wide, ml_i_bias[l], ml_f_bias[l], ml_norm_g[l])
        y_hy = _hy_mixer(wide, dft, dft_t, hy_conv_w[l], hy_conv_b[l], hy_w1[l], hy_b1[l], hy_w2[l], hy_b2[l],
                         hy_w3[l], hy_freq[l], hy_skip[l], hy_norm_g[l])
        ys = [t.reshape(b * seq, GROUP_W) for t in (y_ssd, y_na, y_ml, y_hy)]
        x2 = _out_ffn(x2, mod[l], ys, w_out[l].astype(BF16), ffn_w_gate[l].astype(BF16), ffn_w_up[l].astype(BF16),
                      ffn_w_down[l].astype(BF16), final_norm_g, l == depth - 1)
    return x2.reshape(b, seq, d)
```

```python
import functools
import math

import numpy as np
import jax
import jax.numpy as jnp
from jax import lax
from jax.experimental import pallas as pl
from jax.experimental.pallas import tpu as pltpu

F32 = jnp.float32
BF16 = jnp.bfloat16

D_MODEL = 1024
SEQ = 2048
GRID_W = 64
GROUP_W = 256
HEAD_DIM = 64
N_HEADS = 4
SSD_STATE = 128
SSD_XBC = 768
CHUNK = 128
N_CHUNKS = SEQ // CHUNK
NA_KH = 8
NA_KW = 16
NA_QROWS = 4
NA_KROWS = 12
HY_EMB = 33
HY_FFN = 64
FFN_HIDDEN = 2816
NORM_EPS = 1e-6
FFT_N = 2 * SEQ
FREQ_BLOCK = 512
N_FREQ_BLOCKS = SEQ // FREQ_BLOCK
HY_BATCH_BLOCK = 2
TOKEN_TILE = 512
VMEM_LIMIT = 56 * 1024 * 1024

WIDE_W = 2176
WIDE_HY_BLOCK = 0
WIDE_MLO_BLOCK = 3
WIDE_ZX_BLOCK = 1
WIDE_SM_BLOCK = 16
QKV_W = 1536
QKV_NA_BLOCK = 0
QKV_ML_BLOCK = 1
P_PAD = WIDE_W + QKV_W
SM_DT = 0
SM_IG = 8
SM_FG = 16

NT_DIMS = (((1,), (1,)), ((), ()))
TN_DIMS = (((0,), (0,)), ((), ()))


def _rms(x):
    return x * lax.rsqrt(jnp.mean(x * x, axis=-1, keepdims=True) + NORM_EPS)


def _silu(x):
    return x * jax.nn.sigmoid(x)


def _dot(a, b):
    return jnp.dot(a, b, preferred_element_type=F32)


def _params(*sem):
    return pltpu.CompilerParams(dimension_semantics=sem, vmem_limit_bytes=VMEM_LIMIT)


def _mod_kernel(c_ref, w_ref, b_ref, o_ref):
    cond = _silu(c_ref[...]).astype(BF16)
    o_ref[0] = _dot(cond, w_ref[0].astype(BF16)) + b_ref[0]


def _modulation(c, mod_w, mod_b):
    depth, d, n = mod_w.shape
    b = c.shape[0]
    nb = n // d
    return pl.pallas_call(
        _mod_kernel,
        grid=(depth, nb),
        in_specs=[pl.BlockSpec((b, d), lambda l, j: (0, 0)),
                  pl.BlockSpec((1, d, d), lambda l, j: (l, 0, j)),
                  pl.BlockSpec((1, 1, d), lambda l, j: (l, 0, j))],
        out_specs=pl.BlockSpec((1, b, d), lambda l, j: (l, 0, j)),
        out_shape=jax.ShapeDtypeStruct((depth, b, n), F32),
        compiler_params=_params("arbitrary", "arbitrary"),
        name="adaln_mod",
    )(c, mod_w, mod_b.reshape(depth, 1, n))


def _in_proj_kernel(x_ref, mod_ref, w_ref, wide_ref, qkv_ref):
    h = (_rms(x_ref[...]) * (1.0 + mod_ref[0, 1:2, :]) + mod_ref[0, 0:1, :]).astype(BF16)
    wide_ref[...] = _dot(h, w_ref[:, 0:WIDE_W])
    qkv_ref[...] = _dot(h, w_ref[:, WIDE_W:P_PAD]).astype(BF16)


def _in_proj(x2, mod_l, w_perm):
    n_tok, d = x2.shape
    tm = TOKEN_TILE
    tiles_per_seq = SEQ // tm
    return pl.pallas_call(
        _in_proj_kernel,
        grid=(n_tok // tm,),
        in_specs=[pl.BlockSpec((tm, d), lambda i: (i, 0)),
                  pl.BlockSpec((1, 6, d), lambda i: (i // tiles_per_seq, 0, 0)),
                  pl.BlockSpec((d, P_PAD), lambda i: (0, 0), pipeline_mode=pl.Buffered(1))],
        out_specs=[pl.BlockSpec((tm, WIDE_W), lambda i: (i, 0)), pl.BlockSpec((tm, QKV_W), lambda i: (i, 0))],
        out_shape=[jax.ShapeDtypeStruct((n_tok, WIDE_W), F32), jax.ShapeDtypeStruct((n_tok, QKV_W), BF16)],
        compiler_params=_params("arbitrary"),
        name="in_proj",
    )(x2, mod_l, w_perm)


def _dwconv3_rows(src_ref, b, r0, cols, w, bias):
    x = src_ref[b, pl.ds(r0, CHUNK), cols]
    up = src_ref[b, pl.ds(jnp.maximum(r0 - 1, 0), 1), cols]
    dn = src_ref[b, pl.ds(jnp.minimum(r0 + CHUNK, SEQ - 1), 1), cols]
    up = jnp.where(r0 > 0, up, 0.0)
    dn = jnp.where(r0 + CHUNK < SEQ, dn, 0.0)
    row = lax.broadcasted_iota(jnp.int32, (CHUNK, 1), 0)
    prev = jnp.where(row == 0, up, pltpu.roll(x, 1, 0))
    nxt = jnp.where(row == CHUNK - 1, dn, pltpu.roll(x, CHUNK - 1, 0))
    return w[0:1, :] * prev + w[1:2, :] * x + w[2:3, :] * nxt + bias


def _chunk_scans(a, axis):
    n = a.shape[axis]
    shape = [1, 1]
    shape[axis] = n
    pos = lax.broadcasted_iota(jnp.int32, tuple(shape), axis) % CHUNK
    pre, suf = a, a
    k = 1
    while k < CHUNK:
        pre = pre + jnp.where(pos >= k, pltpu.roll(pre, k, axis), 0.0)
        suf = suf + jnp.where(pos < CHUNK - k, pltpu.roll(suf, n - k, axis), 0.0)
        k *= 2
    return pre, suf


def _gate_rows(sm_ref, rows_s, lo, n):
    for c in range(N_CHUNKS):
        t = sm_ref[0, c * CHUNK:(c + 1) * CHUNK, :].T
        rows_s[c] = t[lo:lo + n, :]


def _ssd_kernel(zx_ref, sm_ref, cw_ref, cb_ref, dtb_col_ref, a_col_ref, dsk_ref, g_ref,
                o_ref, xs_s, b_s, xst_s, ct_s, acol_s, rows_s, dtrow_s, arow_s, ytf_s, ytb_s, st_s):
    cw = cw_ref[...]
    cb = cb_ref[...]

    def conv_body(c, _):
        r0 = pl.multiple_of(c * CHUNK, CHUNK)
        v = _silu(_dwconv3_rows(zx_ref, 0, r0, slice(GROUP_W, GROUP_W + SSD_XBC), cw, cb))
        xs_s[pl.ds(r0, CHUNK), :] = v[:, 0:GROUP_W]
        b_s[pl.ds(r0, CHUNK), :] = v[:, GROUP_W:2 * GROUP_W].astype(BF16)
        xst_s[c] = v[:, 0:GROUP_W].T
        ct_s[c] = v[:, 2 * GROUP_W:3 * GROUP_W].T.astype(BF16)
        return 0

    lax.fori_loop(0, N_CHUNKS, conv_body, 0)

    _gate_rows(sm_ref, rows_s, SM_DT, 8)
    dt_row = jax.nn.softplus(rows_s[...].reshape(N_CHUNKS * 8, CHUNK) + dtb_col_ref[...])
    pre, suf = _chunk_scans(dt_row * (-jnp.exp(a_col_ref[...])), 1)
    rid = lax.broadcasted_iota(jnp.int32, (N_CHUNKS * 8, 1), 0) % 8
    dtrow_s[...] = dt_row.reshape(N_CHUNKS, 8, CHUNK)
    arow_s[...] = jnp.where(rid < N_HEADS, pre, suf).reshape(N_CHUNKS, 8, CHUNK)
    pad_rows = jnp.zeros((CHUNK - 8, CHUNK), F32)
    for c in range(N_CHUNKS):
        acol_s[c * CHUNK:(c + 1) * CHUNK, :] = jnp.concatenate([arow_s[c], pad_rows], axis=0).T

    si = lax.broadcasted_iota(jnp.int32, (CHUNK, CHUNK), 0)
    ti = lax.broadcasted_iota(jnp.int32, (CHUNK, CHUNK), 1)
    first = lax.broadcasted_iota(jnp.int32, (2 * HEAD_DIM, 1), 0) < HEAD_DIM
    st_s[...] = jnp.zeros_like(st_s)

    def chunk_body(i, _):
        units = [(d, g) for d in range(2) for g in range(2)]
        chunk = {0: i, 1: N_CHUNKS - 1 - i}
        rows = {d: pl.multiple_of(chunk[d] * CHUNK, CHUNK) for d in range(2)}
        bgs, cgs, prevs, scores, y_offs = {}, {}, {}, {}, {}
        for d, g in units:
            bgs[d, g] = b_s[pl.ds(rows[d], CHUNK), g * SSD_STATE:(g + 1) * SSD_STATE]
            cgs[d, g] = ct_s[chunk[d], g * SSD_STATE:(g + 1) * SSD_STATE, :]
            prevs[d, g] = st_s[d, g]
            scores[d, g] = _dot(bgs[d, g], cgs[d, g])
            y_offs[d, g] = _dot(prevs[d, g].astype(BF16), cgs[d, g])
        weighted, xdts, x_ends, carries, grow = {}, {}, {}, {}, {}
        for d, g in units:
            mask = (si <= ti) if d == 0 else (si >= ti)
            end = CHUNK - 1 if d == 0 else 0
            acol = acol_s[pl.ds(rows[d], CHUNK), :]
            arow = arow_s[chunk[d]]
            dtr = dtrow_s[chunk[d]]
            for hh in range(2):
                h = 2 * g + hh
                j = d * N_HEADS + h
                ar = arow[j:j + 1, :]
                a_end = ar[:, end:end + 1]
                decay = jnp.exp(jnp.where(mask, ar - acol[:, j:j + 1], -jnp.inf))
                xdt_t = xst_s[chunk[d], h * HEAD_DIM:(h + 1) * HEAD_DIM, :] * dtr[j:j + 1, :]
                weighted[d, h] = (scores[d, g] * decay).astype(BF16)
                xdts[d, h] = xdt_t.astype(BF16)
                x_ends[d, h] = (xdt_t * jnp.exp(a_end - ar)).astype(BF16)
                carries[d, h] = jnp.exp(a_end)
                grow[d, h] = jnp.exp(ar)
        for d in range(2):
            ys = []
            for h in range(N_HEADS):
                g, hh = divmod(h, 2)
                y_off = y_offs[d, g][hh * HEAD_DIM:(hh + 1) * HEAD_DIM, :]
                ys.append(_dot(xdts[d, h], weighted[d, h]) + y_off * grow[d, h])
            y_t = jnp.concatenate(ys, axis=0)
            if d == 0:
                ytf_s[chunk[d]] = y_t
            else:
                ytb_s[chunk[d]] = y_t
        for d, g in units:
            keep = jnp.where(first, carries[d, 2 * g], carries[d, 2 * g + 1])
            x_end = jnp.concatenate([x_ends[d, 2 * g], x_ends[d, 2 * g + 1]], axis=0)
            st_s[d, g] = prevs[d, g] * keep + _dot(x_end, bgs[d, g])
        return 0

    lax.fori_loop(0, N_CHUNKS, chunk_body, 0, unroll=2)

    dsk = dsk_ref[...]
    gain = g_ref[...]

    def out_body(c, _):
        r0 = pl.multiple_of(c * CHUNK, CHUNK)
        y = (ytf_s[c] + ytb_s[c]).T + dsk * xs_s[pl.ds(r0, CHUNK), :]
        y = y * _silu(zx_ref[0, pl.ds(r0, CHUNK), 0:GROUP_W])
        o_ref[0, pl.ds(r0, CHUNK), :] = _rms(y) * gain
        return 0

    lax.fori_loop(0, N_CHUNKS, out_body, 0)


def _ssd_mixer(zx, sm, conv_w, conv_b, dt_bias, a_log, d_skip, norm_g):
    b = zx.shape[0]
    tile_col = lambda v: jnp.tile(v.reshape(8, 1), (N_CHUNKS, 1))
    full = lambda shape: pl.BlockSpec(shape, lambda i: (0,) * len(shape))
    return pl.pallas_call(
        _ssd_kernel,
        grid=(b,),
        in_specs=[pl.BlockSpec((1, SEQ, 1024), lambda i: (i, 0, WIDE_ZX_BLOCK)),
                  pl.BlockSpec((1, SEQ, CHUNK), lambda i: (i, 0, WIDE_SM_BLOCK)),
                  full((3, SSD_XBC)), full((1, SSD_XBC)),
                  full((N_CHUNKS * 8, 1)), full((N_CHUNKS * 8, 1)), full((1, GROUP_W)), full((1, GROUP_W))],
        out_specs=pl.BlockSpec((1, SEQ, GROUP_W), lambda i: (i, 0, 0)),
        out_shape=jax.ShapeDtypeStruct((b, SEQ, GROUP_W), F32),
        scratch_shapes=[pltpu.VMEM((SEQ, GROUP_W), F32), pltpu.VMEM((SEQ, GROUP_W), BF16),
                        pltpu.VMEM((N_CHUNKS, GROUP_W, CHUNK), F32), pltpu.VMEM((N_CHUNKS, GROUP_W, CHUNK), BF16),
                        pltpu.VMEM((SEQ, CHUNK), F32), pltpu.VMEM((N_CHUNKS, 8, CHUNK), F32),
                        pltpu.VMEM((N_CHUNKS, 8, CHUNK), F32), pltpu.VMEM((N_CHUNKS, 8, CHUNK), F32),
                        pltpu.VMEM((N_CHUNKS, GROUP_W, CHUNK), F32), pltpu.VMEM((N_CHUNKS, GROUP_W, CHUNK), F32),
                        pltpu.VMEM((2, 2, 2 * HEAD_DIM, SSD_STATE), F32)],
        compiler_params=_params("arbitrary"),
        name="ssd_mixer",
    )(zx, sm, conv_w, conv_b.reshape(1, SSD_XBC), tile_col(dt_bias), tile_col(a_log),
      jnp.repeat(d_skip, HEAD_DIM).reshape(1, GROUP_W), norm_g.reshape(1, GROUP_W))


def _na_kernel(qkv_ref, bias_ref, g_ref, o_ref):
    q_rows = NA_QROWS * GRID_W
    k_rows = NA_KROWS * GRID_W
    n_blocks = SEQ // q_rows
    pair_w = 2 * HEAD_DIM
    first = lax.broadcasted_iota(jnp.int32, (1, pair_w), 1) < HEAD_DIM
    gain = g_ref[...]

    def block(q0, k0, kind):
        scores, values = [], []
        for pair in range(N_HEADS // 2):
            lo = pair * pair_w
            q2 = qkv_ref[0, pl.ds(q0, q_rows), lo:lo + pair_w] * (HEAD_DIM ** -0.5)
            k2 = qkv_ref[0, pl.ds(k0, k_rows), GROUP_W + lo:GROUP_W + lo + pair_w]
            values.append(qkv_ref[0, pl.ds(k0, k_rows), 2 * GROUP_W + lo:2 * GROUP_W + lo + pair_w])
            for hh in range(2):
                qm = jnp.where(first if hh == 0 else ~first, q2, 0.0).astype(BF16)
                scores.append(lax.dot_general(qm, k2, NT_DIMS, preferred_element_type=F32))
        probs, sums = [], []
        for h in range(N_HEADS):
            s = scores[h] + bias_ref[h, kind]
            p = jnp.exp(s - jnp.max(s, axis=-1, keepdims=True))
            probs.append(p.astype(BF16))
            sums.append(jnp.sum(p, axis=-1, keepdims=True))
        outs = []
        for pair in range(N_HEADS // 2):
            o2 = [_dot(probs[2 * pair + hh], values[pair]) / sums[2 * pair + hh] for hh in range(2)]
            outs.append(jnp.where(first, o2[0], o2[1]))
        o_ref[0, pl.ds(q0, q_rows), :] = _rms(jnp.concatenate(outs, axis=1)) * gain

    block(0, 0, 0)

    def interior(i, _):
        block(pl.multiple_of(i * q_rows, q_rows), pl.multiple_of((i - 1) * q_rows, q_rows), 1)
        return 0

    lax.fori_loop(1, n_blocks - 1, interior, 0, unroll=2)
    block(SEQ - q_rows, SEQ - k_rows, 2)


def _na_bias_table(rpb):
    rows = SEQ // GRID_W
    col = np.arange(GRID_W)
    cs = np.clip(col - NA_KW // 2, 0, GRID_W - NA_KW)
    valid_c = (col[None, :] >= cs[:, None]) & (col[None, :] < cs[:, None] + NA_KW)
    dc = col[None, :] - col[:, None] + NA_KW - 1
    sel_c = (valid_c[None] & (dc[None] == np.arange(2 * NA_KW - 1)[:, None, None])).astype(np.float32)
    sel_r = np.zeros((3, NA_QROWS, NA_KROWS, 2 * NA_KH - 1), np.float32)
    for kind, (r0, ks) in enumerate(((0, 0), (NA_QROWS, 0), (rows - NA_QROWS, rows - NA_KROWS))):
        for i in range(NA_QROWS):
            start = min(max(r0 + i - NA_KH // 2, 0), rows - NA_KH)
            for kr in range(start, start + NA_KH):
                sel_r[kind, i, kr - ks, kr - (r0 + i) + NA_KH - 1] = 1.0
    sel_c = np.tile(sel_c, (1, 1, NA_KROWS))
    sel_r = np.repeat(sel_r, GRID_W, axis=2)
    valid = sel_r.sum(-1)[:, :, None, :] * np.tile(valid_c, (1, NA_KROWS))[None, None]
    by_col = jnp.einsum("hab,bqK->haqK", rpb, sel_c, precision=lax.Precision.HIGHEST)
    bias = sum(by_col[:, a][:, None, None] * sel_r[None, :, :, None, :, a] for a in range(2 * NA_KH - 1))
    bias = jnp.where(valid[None] > 0, bias, -1e30)
    return bias.reshape(N_HEADS, 3, NA_QROWS * GRID_W, NA_KROWS * GRID_W).astype(F32)


def _na_mixer(qkv, rpb, norm_g):
    b = qkv.shape[0]
    table = _na_bias_table(rpb)
    return pl.pallas_call(
        _na_kernel,
        grid=(b,),
        in_specs=[pl.BlockSpec((1, SEQ, 3 * GROUP_W), lambda i: (i, 0, QKV_NA_BLOCK)),
                  pl.BlockSpec(table.shape, lambda i: (0, 0, 0, 0)),
                  pl.BlockSpec((1, GROUP_W), lambda i: (0, 0))],
        out_specs=pl.BlockSpec((1, SEQ, GROUP_W), lambda i: (i, 0, 0)),
        out_shape=jax.ShapeDtypeStruct((b, SEQ, GROUP_W), F32),
        compiler_params=_params("arbitrary"),
        name="na_mixer",
    )(qkv, table, norm_g.reshape(1, GROUP_W))


ML_STATE_ROWS = 80


def _ml_kernel(qkv_ref, og_ref, sm_ref, bcol_ref, g_ref, o_ref, bcol_s, rows_s, lrow_s, qt_s, vt_s, htf_s, htb_s,
               c_s, m_s):
    _gate_rows(sm_ref, rows_s, SM_IG, 16)
    raw = rows_s[...].reshape(N_CHUNKS * 16, CHUNK) + bcol_ref[...]
    pre, suf = _chunk_scans(jax.nn.log_sigmoid(raw), 1)
    rid = lax.broadcasted_iota(jnp.int32, (N_CHUNKS * 16, 1), 0) % 16
    scanned = jnp.where(rid < 8 + N_HEADS, pre, suf)
    beta = raw - pltpu.roll(scanned, N_CHUNKS * 16 - 8, 0)
    lrow_s[...] = jnp.where(rid < 8, beta, scanned).reshape(N_CHUNKS, 16, CHUNK)

    pad_rows = jnp.zeros((CHUNK - 8, CHUNK), F32)
    for c in range(N_CHUNKS):
        rows = slice(c * CHUNK, (c + 1) * CHUNK)
        qt_s[c] = qkv_ref[0, rows, 0:GROUP_W].astype(F32).T.astype(BF16)
        vt_s[c] = qkv_ref[0, rows, 2 * GROUP_W:3 * GROUP_W].astype(F32).T.astype(BF16)
        bcol_s[rows, :] = jnp.concatenate([lrow_s[c, 0:8, :], pad_rows], axis=0).T

    si = lax.broadcasted_iota(jnp.int32, (CHUNK, CHUNK), 0)
    ti = lax.broadcasted_iota(jnp.int32, (CHUNK, CHUNK), 1)
    pad = ML_STATE_ROWS - HEAD_DIM
    one_rows = (lax.broadcasted_iota(jnp.int32, (pad, CHUNK), 0) == 0).astype(BF16)
    k_scale = HEAD_DIM ** -0.5
    c_s[...] = jnp.zeros_like(c_s)
    m_s[...] = jnp.zeros_like(m_s)

    def chunk_body(i, _):
        chunk = {0: i, 1: N_CHUNKS - 1 - i}
        rows = {d: pl.multiple_of(chunk[d] * CHUNK, CHUNK) for d in range(2)}
        for d in range(2):
            phases([(d, h) for h in range(N_HEADS)], chunk, rows)
        return 0

    def phases(units, chunk, rows):
        ks, qts, vts, cexts, raw_s, inters = {}, {}, {}, {}, {}, {}
        for d, h in units:
            j = d * N_HEADS + h
            k = qkv_ref[0, pl.ds(rows[d], CHUNK), GROUP_W + h * HEAD_DIM:GROUP_W + (h + 1) * HEAD_DIM]
            ks[d, h] = k * k_scale
            qts[d, h] = qt_s[chunk[d], h * HEAD_DIM:(h + 1) * HEAD_DIM, :]
            vts[d, h] = vt_s[chunk[d], h * HEAD_DIM:(h + 1) * HEAD_DIM, :]
            cexts[d, h] = c_s[j]
            raw_s[d, h] = _dot(ks[d, h], qts[d, h])
            inters[d, h] = _dot(cexts[d, h].astype(BF16), qts[d, h])
        s_ts, w_inters, floors, sources, keeps = {}, {}, {}, {}, {}
        for d, h in units:
            j = d * N_HEADS + h
            mask = (si <= ti) if d == 0 else (si >= ti)
            end = CHUNK - 1 if d == 0 else 0
            lrow = lrow_s[chunk[d]]
            beta_r = lrow[j:j + 1, :]
            b_r = lrow[8 + j:8 + j + 1, :]
            m_prev = m_s[j][0:1, 0:1]
            beta_m = jnp.where(mask, bcol_s[pl.ds(rows[d], CHUNK), j:j + 1], -jnp.inf)
            mu = jnp.maximum(m_prev, jnp.max(beta_m, axis=0, keepdims=True))
            s_ts[d, h] = raw_s[d, h] * jnp.exp(beta_m - mu)
            w_inters[d, h] = jnp.exp(m_prev - mu)
            floors[d, h] = jnp.exp(-(b_r + mu))
            top = jnp.maximum(m_prev, jnp.max(beta_r, axis=1, keepdims=True))
            v_ext = jnp.concatenate([vts[d, h], one_rows], axis=0).astype(F32)
            sources[d, h] = (v_ext * jnp.exp(beta_r - top)).astype(BF16)
            keeps[d, h] = jnp.exp(m_prev - top)
            m_s[j] = jnp.broadcast_to(b_r[:, end:end + 1] + top, (8, CHUNK))
        hs = []
        for d, h in units:
            s_t = s_ts[d, h]
            inter = inters[d, h]
            num = w_inters[d, h] * inter[0:HEAD_DIM, :] + _dot(vts[d, h], s_t.astype(BF16))
            den = w_inters[d, h] * inter[HEAD_DIM:HEAD_DIM + 1, :] + jnp.sum(s_t, axis=0, keepdims=True)
            hs.append(num / jnp.maximum(jnp.abs(den), floors[d, h]))
        d = units[0][0]
        h_t = jnp.concatenate(hs, axis=0)
        if d == 0:
            htf_s[chunk[d]] = h_t
        else:
            htb_s[chunk[d]] = h_t
        for d, h in units:
            c_s[d * N_HEADS + h] = keeps[d, h] * cexts[d, h] + _dot(sources[d, h], ks[d, h])

    lax.fori_loop(0, N_CHUNKS, chunk_body, 0, unroll=2)

    gain = g_ref[...]

    def out_body(c, _):
        r0 = pl.multiple_of(c * CHUNK, CHUNK)
        h_t = htf_s[c] + htb_s[c]
        normed = []
        for h in range(N_HEADS):
            x = h_t[h * HEAD_DIM:(h + 1) * HEAD_DIM, :]
            normed.append(x * lax.rsqrt(jnp.mean(x * x, axis=0, keepdims=True) + NORM_EPS))
        y = jnp.concatenate(normed, axis=0).T * gain
        o_ref[0, pl.ds(r0, CHUNK), :] = jax.nn.sigmoid(og_ref[0, pl.ds(r0, CHUNK), :]) * y
        return 0

    lax.fori_loop(0, N_CHUNKS, out_body, 0)


def _ml_mixer(qkv, og, sm, i_bias, f_bias, norm_g):
    b = qkv.shape[0]
    bias16 = jnp.concatenate([i_bias.reshape(8), f_bias.reshape(8)])
    bcol = jnp.tile(bias16.reshape(16, 1), (N_CHUNKS, 1))
    full = lambda shape: pl.BlockSpec(shape, lambda i: (0,) * len(shape))
    return pl.pallas_call(
        _ml_kernel,
        grid=(b,),
        in_specs=[pl.BlockSpec((1, SEQ, 3 * GROUP_W), lambda i: (i, 0, QKV_ML_BLOCK)),
                  pl.BlockSpec((1, SEQ, GROUP_W), lambda i: (i, 0, WIDE_MLO_BLOCK)),
                  pl.BlockSpec((1, SEQ, CHUNK), lambda i: (i, 0, WIDE_SM_BLOCK)),
                  full((N_CHUNKS * 16, 1)), full((1, GROUP_W))],
        out_specs=pl.BlockSpec((1, SEQ, GROUP_W), lambda i: (i, 0, 0)),
        out_shape=jax.ShapeDtypeStruct((b, SEQ, GROUP_W), F32),
        scratch_shapes=[pltpu.VMEM((SEQ, CHUNK), F32), pltpu.VMEM((N_CHUNKS, 16, CHUNK), F32),
                        pltpu.VMEM((N_CHUNKS, 16, CHUNK), F32), pltpu.VMEM((N_CHUNKS, GROUP_W, CHUNK), BF16),
                        pltpu.VMEM((N_CHUNKS, GROUP_W, CHUNK), BF16), pltpu.VMEM((N_CHUNKS, GROUP_W, CHUNK), F32),
                        pltpu.VMEM((N_CHUNKS, GROUP_W, CHUNK), F32),
                        pltpu.VMEM((2 * N_HEADS, ML_STATE_ROWS, HEAD_DIM), F32),
                        pltpu.VMEM((2 * N_HEADS, 8, CHUNK), F32)],
        compiler_params=_params("arbitrary"),
        name="mlstm_mixer",
    )(qkv, og, sm, bcol, norm_g.reshape(1, GROUP_W))


def _dft_tables():
    f = np.arange(SEQ, dtype=np.int64)[:, None]
    t = np.arange(SEQ, dtype=np.int64)[None, :]
    ang = (2.0 * np.pi / FFT_N) * ((f * t) % FFT_N).astype(np.float64)
    wc = np.cos(ang)
    ws = np.sin(ang)
    ws[0, :] = np.where(np.arange(SEQ) % 2 == 0, 1.0, -1.0)
    w = np.stack([wc, ws]).reshape(2, N_FREQ_BLOCKS, FREQ_BLOCK, SEQ).transpose(1, 0, 2, 3)
    w = np.ascontiguousarray(w.reshape(FFT_N, SEQ)).astype(np.float32)
    return jnp.asarray(w).astype(BF16), jnp.asarray(np.ascontiguousarray(w.T)).astype(BF16)


def _hy_features():
    t = jnp.linspace(0.0, 1.0, SEQ, dtype=F32)[:, None]
    bands = (HY_EMB - 1) // 2
    f = jnp.linspace(1e-4, bands - 1, bands, dtype=F32)
    ang = (2.0 * math.pi) * (jnp.arange(SEQ, dtype=F32) / SEQ)[:, None] * f[None, :]
    feats = jnp.concatenate([t, jnp.cos(ang), -jnp.sin(ang)], axis=-1)
    deltas = jnp.abs(jnp.linspace(math.log(1e-2) / 1.5, math.log(1e-2) / 0.3, GROUP_W, dtype=F32))
    return jnp.pad(feats, ((0, 0), (0, CHUNK - HY_EMB))), t, deltas.reshape(1, GROUP_W)


def _hy_filter_kernel(feats_ref, w1_ref, b1_ref, w2_ref, b2_ref, w3_ref, freq_ref, t_ref, dl_ref, hp_ref, hm_ref):
    h = jnp.sin(freq_ref[0:1, :] * (_dot(feats_ref[...].astype(BF16), w1_ref[...].astype(BF16)) + b1_ref[...]))
    h = jnp.sin(freq_ref[1:2, :] * (_dot(h.astype(BF16), w2_ref[...].astype(BF16)) + b2_ref[...]))
    h = _dot(h.astype(BF16), w3_ref[...].astype(BF16))
    win = jnp.exp(-t_ref[...] * dl_ref[...])
    row = lax.broadcasted_iota(jnp.int32, (SEQ, 1), 0)
    for o in range(2):
        hf = h[:, (2 * o) * GROUP_W:(2 * o + 1) * GROUP_W] * win
        hb = h[:, (2 * o + 1) * GROUP_W:(2 * o + 2) * GROUP_W] * win
        r = lax.rsqrt(jnp.sum(hf * hf + hb * hb, axis=0, keepdims=True) + NORM_EPS)
        hf = hf * r
        hb = jnp.where(row == 0, 0.0, hb * r)
        hp_ref[:, o * GROUP_W:(o + 1) * GROUP_W] = (hf + hb).astype(BF16)
        hm_ref[:, o * GROUP_W:(o + 1) * GROUP_W] = (hf - hb).astype(BF16)


def _hy_spec_kernel(wc_ref, ws_ref, hp_ref, hm_ref, kc_ref, ks_ref):
    rowg = lax.broadcasted_iota(jnp.int32, (FREQ_BLOCK, 1), 0) + pl.program_id(0) * FREQ_BLOCK
    scale = jnp.where(rowg == 0, 1.0 / FFT_N, 2.0 / FFT_N)
    kc_ref[...] = _dot(wc_ref[...], hp_ref[...]) * scale
    nyq = _dot(ws_ref[0:8, :], hp_ref[...])[0:1, :]
    ks_ref[...] = jnp.where(rowg == 0, nyq, _dot(ws_ref[...], hm_ref[...])) * scale


def _hy_spectrum(w, w1, b1, w2, b2, w3, freq):
    feats, t, deltas = _hy_features()
    w1p = jnp.pad(w1, ((0, CHUNK - HY_EMB), (0, 0)))
    hp, hm = pl.pallas_call(
        _hy_filter_kernel,
        out_shape=[jax.ShapeDtypeStruct((SEQ, 2 * GROUP_W), BF16)] * 2,
        compiler_params=pltpu.CompilerParams(vmem_limit_bytes=VMEM_LIMIT),
        name="hyena_filter",
    )(feats, w1p, b1.reshape(1, HY_FFN), w2, b2.reshape(1, HY_FFN), w3, freq, t, deltas)
    nj = N_FREQ_BLOCKS
    return pl.pallas_call(
        _hy_spec_kernel,
        grid=(nj,),
        in_specs=[pl.BlockSpec((FREQ_BLOCK, SEQ), lambda j: (2 * j, 0)),
                  pl.BlockSpec((FREQ_BLOCK, SEQ), lambda j: (2 * j + 1, 0)),
                  pl.BlockSpec((SEQ, 2 * GROUP_W), lambda j: (0, 0)),
                  pl.BlockSpec((SEQ, 2 * GROUP_W), lambda j: (0, 0))],
        out_specs=[pl.BlockSpec((FREQ_BLOCK, 2 * GROUP_W), lambda j: (j, 0))] * 2,
        out_shape=[jax.ShapeDtypeStruct((SEQ, 2 * GROUP_W), F32)] * 2,
        compiler_params=_params("arbitrary"),
        name="hyena_spectrum",
    )(w, w, hp, hm)


def _hy_conv_kernel(z_ref, gate_ref, cwz_ref, cbz_ref, cwg_ref, cbg_ref, w_ref, wt_ref, kc_ref, ks_ref, skip_ref, g_ref,
                    o_ref, zb_s, acc_s, *, conv_z, final_norm):
    j = pl.program_id(1)
    n_b = z_ref.shape[0]
    cols = slice(0, GROUP_W)

    def z_rows(b, r0):
        if conv_z:
            return _dwconv3_rows(z_ref, b, r0, cols, cwz_ref[...], cbz_ref[...])
        return z_ref[b, pl.ds(r0, CHUNK), :]

    @pl.when(j == 0)
    def _():
        for b in range(n_b):
            def fill(c, _, b=b):
                r0 = pl.multiple_of(c * CHUNK, CHUNK)
                zb_s[pl.ds(r0, CHUNK), b * GROUP_W:(b + 1) * GROUP_W] = z_rows(b, r0).astype(BF16)
                return 0

            lax.fori_loop(0, N_CHUNKS, fill, 0)
        acc_s[...] = jnp.zeros_like(acc_s)

    row0 = (lax.broadcasted_iota(jnp.int32, (FREQ_BLOCK, 1), 0) + j * FREQ_BLOCK) == 0
    kc = jnp.concatenate([kc_ref[...]] * n_b, axis=1)
    ks = jnp.concatenate([ks_ref[...]] * n_b, axis=1)
    x = _dot(w_ref[...], zb_s[...])
    xc = x[0:FREQ_BLOCK, :]
    xs = x[FREQ_BLOCK:, :]
    xs_ks = xs * ks
    yc = xc * kc - jnp.where(row0, 0.0, xs_ks)
    ys = jnp.where(row0, xs_ks, xc * ks + xs * kc)
    acc_s[...] += _dot(wt_ref[...], jnp.concatenate([yc, ys], axis=0).astype(BF16))

    @pl.when(j == pl.num_programs(1) - 1)
    def _():
        for b in range(n_b):
            def finish(c, _, b=b):
                r0 = pl.multiple_of(c * CHUNK, CHUNK)
                gate = _dwconv3_rows(gate_ref, b, r0, cols, cwg_ref[...], cbg_ref[...])
                y = acc_s[pl.ds(r0, CHUNK), b * GROUP_W:(b + 1) * GROUP_W]
                r = gate * (y + skip_ref[...] * z_rows(b, r0))
                if final_norm:
                    r = _rms(r) * g_ref[...]
                o_ref[b, pl.ds(r0, CHUNK), :] = r
                return 0

            lax.fori_loop(0, N_CHUNKS, finish, 0)


def _hy_long_conv(z, z_col, gates, gate_col, conv_w, conv_b, conv_z, w, wt, kc, ks, order, skip, norm_g, final_norm):
    b = z.shape[0]
    bg = HY_BATCH_BLOCK
    nj = N_FREQ_BLOCKS
    taps = lambda col: (conv_w[:, col * GROUP_W:(col + 1) * GROUP_W],
                        conv_b[col * GROUP_W:(col + 1) * GROUP_W].reshape(1, GROUP_W))
    small = lambda shape: pl.BlockSpec(shape, lambda g, j: (0, 0))
    return pl.pallas_call(
        functools.partial(_hy_conv_kernel, conv_z=conv_z, final_norm=final_norm),
        grid=(b // bg, nj),
        in_specs=[pl.BlockSpec((bg, SEQ, GROUP_W), lambda g, j: (g, 0, z_col)),
                  pl.BlockSpec((bg, SEQ, GROUP_W), lambda g, j: (g, 0, gate_col)),
                  small((3, GROUP_W)), small((1, GROUP_W)), small((3, GROUP_W)), small((1, GROUP_W)),
                  pl.BlockSpec((2 * FREQ_BLOCK, SEQ), lambda g, j: (j, 0)),
                  pl.BlockSpec((SEQ, 2 * FREQ_BLOCK), lambda g, j: (0, j)),
                  pl.BlockSpec((FREQ_BLOCK, GROUP_W), lambda g, j: (j, order)),
                  pl.BlockSpec((FREQ_BLOCK, GROUP_W), lambda g, j: (j, order)),
                  pl.BlockSpec((1, GROUP_W), lambda g, j: (0, 0)),
                  pl.BlockSpec((1, GROUP_W), lambda g, j: (0, 0))],
        out_specs=pl.BlockSpec((bg, SEQ, GROUP_W), lambda g, j: (g, 0, 0)),
        out_shape=jax.ShapeDtypeStruct((b, SEQ, GROUP_W), F32),
        scratch_shapes=[pltpu.VMEM((SEQ, bg * GROUP_W), BF16), pltpu.VMEM((SEQ, bg * GROUP_W), F32)],
        compiler_params=_params("arbitrary", "arbitrary"),
        name="hyena_long_conv",
    )(z, gates, *taps(0), *taps(order + 1), w, wt, kc, ks, skip.reshape(1, GROUP_W), norm_g.reshape(1, GROUP_W))


def _hy_mixer(u, w, wt, conv_w, conv_b, w1, b1, w2, b2, w3, freq, skip, norm_g):
    kc, ks = _hy_spectrum(w, w1, b1, w2, b2, w3, freq)
    first = WIDE_HY_BLOCK * 3
    z1 = _hy_long_conv(u, first, u, first + 1, conv_w, conv_b, True, w, wt, kc, ks, 0, skip[0], norm_g, False)
    return _hy_long_conv(z1, 0, u, first + 2, conv_w, conv_b, False, w, wt, kc, ks, 1, skip[1], norm_g, True)


FFN_SPLITS = ((0, 1024), (1024, 2048), (2048, FFN_HIDDEN))


def _ffn_kernel(x_ref, mod_ref, ya_ref, yb_ref, yc_ref, yd_ref, wo_ref, wg_ref, wu_ref, wd_ref, gf_ref, o_ref, *,
                final_norm):
    y_mix = jnp.concatenate([ya_ref[...], yb_ref[...], yc_ref[...], yd_ref[...]], axis=1).astype(BF16)
    x = x_ref[...] + mod_ref[0, 2:3, :] * _dot(y_mix, wo_ref[...])
    h = (_rms(x) * (1.0 + mod_ref[0, 4:5, :]) + mod_ref[0, 3:4, :]).astype(BF16)
    acc = jnp.zeros_like(x)
    for lo, hi in FFN_SPLITS:
        a = _silu(_dot(h, wg_ref[:, lo:hi])) * _dot(h, wu_ref[:, lo:hi])
        acc = acc + _dot(a.astype(BF16), wd_ref[lo:hi, :])
    y = x + mod_ref[0, 5:6, :] * acc
    if final_norm:
        y = _rms(y) * gf_ref[...]
    o_ref[...] = y


def _out_ffn(x2, mod_l, ys, w_out, wg, wu, wd, final_g, final_norm):
    n_tok, d = x2.shape
    tm = TOKEN_TILE
    tiles_per_seq = SEQ // tm
    resident = lambda shape: pl.BlockSpec(shape, lambda i: (0, 0), pipeline_mode=pl.Buffered(1))
    return pl.pallas_call(
        functools.partial(_ffn_kernel, final_norm=final_norm),
        grid=(n_tok // tm,),
        in_specs=[pl.BlockSpec((tm, d), lambda i: (i, 0)),
                  pl.BlockSpec((1, 6, d), lambda i: (i // tiles_per_seq, 0, 0))]
                 + [pl.BlockSpec((tm, GROUP_W), lambda i: (i, 0))] * 4
                 + [resident((d, d)), resident((d, FFN_HIDDEN)), resident((d, FFN_HIDDEN)), resident((FFN_HIDDEN, d)),
                    pl.BlockSpec((1, d), lambda i: (0, 0))],
        out_specs=pl.BlockSpec((tm, d), lambda i: (i, 0)),
        out_shape=jax.ShapeDtypeStruct((n_tok, d), F32),
        compiler_params=_params("arbitrary"),
        name="out_proj_swiglu",
    )(x2, mod_l, *ys, w_out, wg, wu, wd, final_g.reshape(1, d))


def _permute_w_in(w):
    cols = [w[:, 2840:3608], w[:, 2568:2824], w[:, 0:1024], w[:, 1024:1032], w[:, 2824:2840],
            jnp.zeros((w.shape[0], P_PAD - 3608), w.dtype), w[:, 1032:1800], w[:, 1800:2568]]
    return jnp.concatenate(cols, axis=1).astype(BF16)


def kernel(x, c, mod_w, mod_b, w_in, ssd_conv_w, ssd_conv_b, ssd_dt_bias, ssd_a_log, ssd_d, ssd_norm_g, na_rpb, na_norm_g, ml_i_bias, ml_f_bias, ml_norm_g, hy_conv_w, hy_conv_b, hy_w1, hy_b1, hy_w2, hy_b2, hy_w3, hy_freq, hy_skip, hy_norm_g, w_out, ffn_w_gate, ffn_w_up, ffn_w_down, final_norm_g):
    b, seq, d = x.shape
    assert seq == SEQ and d == D_MODEL and b % HY_BATCH_BLOCK == 0
    depth = mod_w.shape[0]
    mod = _modulation(c, mod_w, mod_b).reshape(depth, b, 6, d)
    dft, dft_t = _dft_tables()
    x2 = x.reshape(b * seq, d)
    for l in range(depth):
        wide, qkv = _in_proj(x2, mod[l], _permute_w_in(w_in[l]))
        wide = wide.reshape(b, seq, WIDE_W)
        qkv = qkv.reshape(b, seq, QKV_W)
        y_ssd = _ssd_mixer(wide, wide, ssd_conv_w[l], ssd_conv_b[l], ssd_dt_bias[l], ssd_a_log[l], ssd_d[l],
                           ssd_norm_g[l])
        y_na = _na_mixer(qkv, na_rpb[l], na_norm_g[l])
        y_ml = _ml_mixer(qkv, wide, wide, ml_i_bias[l], ml_f_bias[l], ml_norm_g[l])
        y_hy = _hy_mixer(wide, dft, dft_t, hy_conv_w[l], hy_conv_b[l], hy_w1[l], hy_b1[l], hy_w2[l], hy_b2[l],
                         hy_w3[l], hy_freq[l], hy_skip[l], hy_norm_g[l])
        ys = [t.reshape(b * seq, GROUP_W) for t in (y_ssd, y_na, y_ml, y_hy)]
        x2 = _out_ffn(x2, mod[l], ys, w_out[l].astype(BF16), ffn_w_gate[l].astype(BF16), ffn_w_up[l].astype(BF16),
                      ffn_w_down[l].astype(BF16), final_norm_g, l == depth - 1)
    return x2.reshape(b, seq, d)
```

```python
import functools
import math

import numpy as np
import jax
import jax.numpy as jnp
from jax import lax
from jax.experimental import pallas as pl
from jax.experimental.pallas import tpu as pltpu

F32 = jnp.float32
BF16 = jnp.bfloat16

D_MODEL = 1024
SEQ = 2048
GRID_W = 64
GROUP_W = 256
HEAD_DIM = 64
N_HEADS = 4
SSD_STATE = 128
SSD_XBC = 768
CHUNK = 128
N_CHUNKS = SEQ // CHUNK
NA_KH = 8
NA_KW = 16
NA_ROW_GROUP = 4
HY_EMB = 33
HY_FFN = 64
FFN_HIDDEN = 2816
NORM_EPS = 1e-6
FFT_N = 2 * SEQ
FREQ_BLOCK = 512
N_FREQ_BLOCKS = SEQ // FREQ_BLOCK
HY_BATCH_BLOCK = 2
TOKEN_TILE = 512
VMEM_LIMIT = 56 * 1024 * 1024

WIDE_W = 2176
WIDE_HY_BLOCK = 0
WIDE_MLO_BLOCK = 3
WIDE_ZX_BLOCK = 1
WIDE_SM_BLOCK = 16
QKV_W = 1536
QKV_NA_BLOCK = 0
QKV_ML_BLOCK = 1
P_PAD = WIDE_W + QKV_W
SM_DT = 0
SM_IG = 8
SM_FG = 16

NT_DIMS = (((1,), (1,)), ((), ()))
TN_DIMS = (((0,), (0,)), ((), ()))


def _rms(x):
    return x * lax.rsqrt(jnp.mean(x * x, axis=-1, keepdims=True) + NORM_EPS)


def _silu(x):
    return x * jax.nn.sigmoid(x)


def _dot(a, b):
    return jnp.dot(a, b, preferred_element_type=F32)


def _params(*sem):
    return pltpu.CompilerParams(dimension_semantics=sem, vmem_limit_bytes=VMEM_LIMIT)


def _mod_kernel(c_ref, w_ref, b_ref, o_ref):
    cond = _silu(c_ref[...]).astype(BF16)
    o_ref[0] = _dot(cond, w_ref[0].astype(BF16)) + b_ref[0]


def _modulation(c, mod_w, mod_b):
    depth, d, n = mod_w.shape
    b = c.shape[0]
    nb = n // d
    return pl.pallas_call(
        _mod_kernel,
        grid=(depth, nb),
        in_specs=[pl.BlockSpec((b, d), lambda l, j: (0, 0)),
                  pl.BlockSpec((1, d, d), lambda l, j: (l, 0, j)),
                  pl.BlockSpec((1, 1, d), lambda l, j: (l, 0, j))],
        out_specs=pl.BlockSpec((1, b, d), lambda l, j: (l, 0, j)),
        out_shape=jax.ShapeDtypeStruct((depth, b, n), F32),
        compiler_params=_params("arbitrary", "arbitrary"),
        name="adaln_mod",
    )(c, mod_w, mod_b.reshape(depth, 1, n))


def _in_proj_kernel(x_ref, mod_ref, w_ref, wide_ref, qkv_ref):
    h = (_rms(x_ref[...]) * (1.0 + mod_ref[0, 1:2, :]) + mod_ref[0, 0:1, :]).astype(BF16)
    wide_ref[...] = _dot(h, w_ref[:, 0:WIDE_W])
    qkv_ref[...] = _dot(h, w_ref[:, WIDE_W:P_PAD]).astype(BF16)


def _in_proj(x2, mod_l, w_perm):
    n_tok, d = x2.shape
    tm = TOKEN_TILE
    tiles_per_seq = SEQ // tm
    return pl.pallas_call(
        _in_proj_kernel,
        grid=(n_tok // tm,),
        in_specs=[pl.BlockSpec((tm, d), lambda i: (i, 0)),
                  pl.BlockSpec((1, 6, d), lambda i: (i // tiles_per_seq, 0, 0)),
                  pl.BlockSpec((d, P_PAD), lambda i: (0, 0), pipeline_mode=pl.Buffered(1))],
        out_specs=[pl.BlockSpec((tm, WIDE_W), lambda i: (i, 0)), pl.BlockSpec((tm, QKV_W), lambda i: (i, 0))],
        out_shape=[jax.ShapeDtypeStruct((n_tok, WIDE_W), F32), jax.ShapeDtypeStruct((n_tok, QKV_W), BF16)],
        compiler_params=_params("arbitrary"),
        name="in_proj",
    )(x2, mod_l, w_perm)


def _dwconv3_rows(src_ref, b, r0, cols, w, bias):
    x = src_ref[b, pl.ds(r0, CHUNK), cols]
    up = src_ref[b, pl.ds(jnp.maximum(r0 - 1, 0), 1), cols]
    dn = src_ref[b, pl.ds(jnp.minimum(r0 + CHUNK, SEQ - 1), 1), cols]
    up = jnp.where(r0 > 0, up, 0.0)
    dn = jnp.where(r0 + CHUNK < SEQ, dn, 0.0)
    row = lax.broadcasted_iota(jnp.int32, (CHUNK, 1), 0)
    prev = jnp.where(row == 0, up, pltpu.roll(x, 1, 0))
    nxt = jnp.where(row == CHUNK - 1, dn, pltpu.roll(x, CHUNK - 1, 0))
    return w[0:1, :] * prev + w[1:2, :] * x + w[2:3, :] * nxt + bias


def _chunk_scans(a, axis):
    n = a.shape[axis]
    shape = [1, 1]
    shape[axis] = n
    pos = lax.broadcasted_iota(jnp.int32, tuple(shape), axis) % CHUNK
    pre, suf = a, a
    k = 1
    while k < CHUNK:
        pre = pre + jnp.where(pos >= k, pltpu.roll(pre, k, axis), 0.0)
        suf = suf + jnp.where(pos < CHUNK - k, pltpu.roll(suf, n - k, axis), 0.0)
        k *= 2
    return pre, suf


def _gate_rows(sm_ref, rows_s, lo, n):
    for c in range(N_CHUNKS):
        t = sm_ref[0, c * CHUNK:(c + 1) * CHUNK, :].T
        rows_s[c] = t[lo:lo + n, :]


def _ssd_kernel(zx_ref, sm_ref, cw_ref, cb_ref, dtb_col_ref, a_col_ref, dsk_ref, g_ref,
                o_ref, xs_s, b_s, xst_s, ct_s, acol_s, rows_s, dtrow_s, arow_s, ytf_s, ytb_s, st_s):
    cw = cw_ref[...]
    cb = cb_ref[...]

    def conv_body(c, _):
        r0 = pl.multiple_of(c * CHUNK, CHUNK)
        v = _silu(_dwconv3_rows(zx_ref, 0, r0, slice(GROUP_W, GROUP_W + SSD_XBC), cw, cb))
        xs_s[pl.ds(r0, CHUNK), :] = v[:, 0:GROUP_W]
        b_s[pl.ds(r0, CHUNK), :] = v[:, GROUP_W:2 * GROUP_W].astype(BF16)
        xst_s[c] = v[:, 0:GROUP_W].T
        ct_s[c] = v[:, 2 * GROUP_W:3 * GROUP_W].T.astype(BF16)
        return 0

    lax.fori_loop(0, N_CHUNKS, conv_body, 0)

    _gate_rows(sm_ref, rows_s, SM_DT, 8)
    dt_row = jax.nn.softplus(rows_s[...].reshape(N_CHUNKS * 8, CHUNK) + dtb_col_ref[...])
    pre, suf = _chunk_scans(dt_row * (-jnp.exp(a_col_ref[...])), 1)
    rid = lax.broadcasted_iota(jnp.int32, (N_CHUNKS * 8, 1), 0) % 8
    dtrow_s[...] = dt_row.reshape(N_CHUNKS, 8, CHUNK)
    arow_s[...] = jnp.where(rid < N_HEADS, pre, suf).reshape(N_CHUNKS, 8, CHUNK)
    pad_rows = jnp.zeros((CHUNK - 8, CHUNK), F32)
    for c in range(N_CHUNKS):
        acol_s[c * CHUNK:(c + 1) * CHUNK, :] = jnp.concatenate([arow_s[c], pad_rows], axis=0).T

    si = lax.broadcasted_iota(jnp.int32, (CHUNK, CHUNK), 0)
    ti = lax.broadcasted_iota(jnp.int32, (CHUNK, CHUNK), 1)
    first = lax.broadcasted_iota(jnp.int32, (2 * HEAD_DIM, 1), 0) < HEAD_DIM
    st_s[...] = jnp.zeros_like(st_s)

    def chunk_body(i, _):
        units = [(d, g) for d in range(2) for g in range(2)]
        chunk = {0: i, 1: N_CHUNKS - 1 - i}
        rows = {d: pl.multiple_of(chunk[d] * CHUNK, CHUNK) for d in range(2)}
        bgs, cgs, prevs, scores, y_offs = {}, {}, {}, {}, {}
        for d, g in units:
            bgs[d, g] = b_s[pl.ds(rows[d], CHUNK), g * SSD_STATE:(g + 1) * SSD_STATE]
            cgs[d, g] = ct_s[chunk[d], g * SSD_STATE:(g + 1) * SSD_STATE, :]
            prevs[d, g] = st_s[d, g]
            scores[d, g] = _dot(bgs[d, g], cgs[d, g])
            y_offs[d, g] = _dot(prevs[d, g].astype(BF16), cgs[d, g])
        weighted, xdts, x_ends, carries, grow = {}, {}, {}, {}, {}
        for d, g in units:
            mask = (si <= ti) if d == 0 else (si >= ti)
            end = CHUNK - 1 if d == 0 else 0
            acol = acol_s[pl.ds(rows[d], CHUNK), :]
            arow = arow_s[chunk[d]]
            dtr = dtrow_s[chunk[d]]
            for hh in range(2):
                h = 2 * g + hh
                j = d * N_HEADS + h
                ar = arow[j:j + 1, :]
                a_end = ar[:, end:end + 1]
                decay = jnp.exp(jnp.where(mask, ar - acol[:, j:j + 1], -jnp.inf))
                xdt_t = xst_s[chunk[d], h * HEAD_DIM:(h + 1) * HEAD_DIM, :] * dtr[j:j + 1, :]
                weighted[d, h] = (scores[d, g] * decay).astype(BF16)
                xdts[d, h] = xdt_t.astype(BF16)
                x_ends[d, h] = (xdt_t * jnp.exp(a_end - ar)).astype(BF16)
                carries[d, h] = jnp.exp(a_end)
                grow[d, h] = jnp.exp(ar)
        for d in range(2):
            ys = []
            for h in range(N_HEADS):
                g, hh = divmod(h, 2)
                y_off = y_offs[d, g][hh * HEAD_DIM:(hh + 1) * HEAD_DIM, :]
                ys.append(_dot(xdts[d, h], weighted[d, h]) + y_off * grow[d, h])
            y_t = jnp.concatenate(ys, axis=0)
            if d == 0:
                ytf_s[chunk[d]] = y_t
            else:
                ytb_s[chunk[d]] = y_t
        for d, g in units:
            keep = jnp.where(first, carries[d, 2 * g], carries[d, 2 * g + 1])
            x_end = jnp.concatenate([x_ends[d, 2 * g], x_ends[d, 2 * g + 1]], axis=0)
            st_s[d, g] = prevs[d, g] * keep + _dot(x_end, bgs[d, g])
        return 0

    lax.fori_loop(0, N_CHUNKS, chunk_body, 0, unroll=2)

    dsk = dsk_ref[...]
    gain = g_ref[...]

    def out_body(c, _):
        r0 = pl.multiple_of(c * CHUNK, CHUNK)
        y = (ytf_s[c] + ytb_s[c]).T + dsk * xs_s[pl.ds(r0, CHUNK), :]
        y = y * _silu(zx_ref[0, pl.ds(r0, CHUNK), 0:GROUP_W])
        o_ref[0, pl.ds(r0, CHUNK), :] = _rms(y) * gain
        return 0

    lax.fori_loop(0, N_CHUNKS, out_body, 0)


def _ssd_mixer(zx, sm, conv_w, conv_b, dt_bias, a_log, d_skip, norm_g):
    b = zx.shape[0]
    tile_col = lambda v: jnp.tile(v.reshape(8, 1), (N_CHUNKS, 1))
    full = lambda shape: pl.BlockSpec(shape, lambda i: (0,) * len(shape))
    return pl.pallas_call(
        _ssd_kernel,
        grid=(b,),
        in_specs=[pl.BlockSpec((1, SEQ, 1024), lambda i: (i, 0, WIDE_ZX_BLOCK)),
                  pl.BlockSpec((1, SEQ, CHUNK), lambda i: (i, 0, WIDE_SM_BLOCK)),
                  full((3, SSD_XBC)), full((1, SSD_XBC)),
                  full((N_CHUNKS * 8, 1)), full((N_CHUNKS * 8, 1)), full((1, GROUP_W)), full((1, GROUP_W))],
        out_specs=pl.BlockSpec((1, SEQ, GROUP_W), lambda i: (i, 0, 0)),
        out_shape=jax.ShapeDtypeStruct((b, SEQ, GROUP_W), F32),
        scratch_shapes=[pltpu.VMEM((SEQ, GROUP_W), F32), pltpu.VMEM((SEQ, GROUP_W), BF16),
                        pltpu.VMEM((N_CHUNKS, GROUP_W, CHUNK), F32), pltpu.VMEM((N_CHUNKS, GROUP_W, CHUNK), BF16),
                        pltpu.VMEM((SEQ, CHUNK), F32), pltpu.VMEM((N_CHUNKS, 8, CHUNK), F32),
                        pltpu.VMEM((N_CHUNKS, 8, CHUNK), F32), pltpu.VMEM((N_CHUNKS, 8, CHUNK), F32),
                        pltpu.VMEM((N_CHUNKS, GROUP_W, CHUNK), F32), pltpu.VMEM((N_CHUNKS, GROUP_W, CHUNK), F32),
                        pltpu.VMEM((2, 2, 2 * HEAD_DIM, SSD_STATE), F32)],
        compiler_params=_params("arbitrary"),
        name="ssd_mixer",
    )(zx, sm, conv_w, conv_b.reshape(1, SSD_XBC), tile_col(dt_bias), tile_col(a_log),
      jnp.repeat(d_skip, HEAD_DIM).reshape(1, GROUP_W), norm_g.reshape(1, GROUP_W))


def _na_kernel(qkv_ref, bias_ref, g_ref, o_ref):
    n_rows = SEQ // GRID_W
    k_tok = NA_KH * GRID_W
    pair_w = 2 * HEAD_DIM
    first = lax.broadcasted_iota(jnp.int32, (1, pair_w), 1) < HEAD_DIM
    gain = g_ref[...]

    def row_group(g, _):
        units = [(i, pair) for i in range(NA_ROW_GROUP) for pair in range(N_HEADS // 2)]
        q0, k0, dr0, scores, values = {}, {}, {}, {}, {}
        for i in range(NA_ROW_GROUP):
            r = g * NA_ROW_GROUP + i
            start = jnp.clip(r - NA_KH // 2, 0, n_rows - NA_KH)
            dr0[i] = start - r + NA_KH - 1
            q0[i] = pl.multiple_of(r * GRID_W, GRID_W)
            k0[i] = pl.multiple_of(start * GRID_W, GRID_W)
        for i, pair in units:
            lo = pair * pair_w
            q2 = qkv_ref[0, pl.ds(q0[i], GRID_W), lo:lo + pair_w] * (HEAD_DIM ** -0.5)
            k2 = qkv_ref[0, pl.ds(k0[i], k_tok), GROUP_W + lo:GROUP_W + lo + pair_w]
            values[i, pair] = qkv_ref[0, pl.ds(k0[i], k_tok), 2 * GROUP_W + lo:2 * GROUP_W + lo + pair_w]
            qm = jnp.concatenate([jnp.where(first, q2, 0.0), jnp.where(first, 0.0, q2)], axis=0).astype(BF16)
            scores[i, pair] = lax.dot_general(qm, k2, NT_DIMS, preferred_element_type=F32)
        probs, sums = {}, {}
        for i, pair in units:
            bias = jnp.concatenate(
                [jnp.concatenate([bias_ref[2 * pair + hh, dr0[i] + 2 * m] for m in range(NA_KH // 2)], axis=1)
                 for hh in range(2)], axis=0)
            s = scores[i, pair] + bias
            p = jnp.exp(s - jnp.max(s, axis=-1, keepdims=True))
            probs[i, pair] = p.astype(BF16)
            sums[i, pair] = jnp.sum(p, axis=-1, keepdims=True)
        for i in range(NA_ROW_GROUP):
            outs = []
            for pair in range(N_HEADS // 2):
                o2 = _dot(probs[i, pair], values[i, pair]) / sums[i, pair]
                outs.append(jnp.where(first, o2[0:GRID_W, :], o2[GRID_W:, :]))
            o_ref[0, pl.ds(q0[i], GRID_W), :] = _rms(jnp.concatenate(outs, axis=1)) * gain
        return 0

    lax.fori_loop(0, n_rows // NA_ROW_GROUP, row_group, 0, unroll=2)


def _na_bias_tiles(rpb):
    col = np.arange(GRID_W)
    cs = np.clip(col - NA_KW // 2, 0, GRID_W - NA_KW)
    valid_c = (col[None, :] >= cs[:, None]) & (col[None, :] < cs[:, None] + NA_KW)
    dc = col[None, :] - col[:, None] + NA_KW - 1
    sel_c = (valid_c[None] & (dc[None] == np.arange(2 * NA_KW - 1)[:, None, None])).astype(np.float32)
    by_row = jnp.einsum("hab,bqk->haqk", rpb, sel_c, precision=lax.Precision.HIGHEST)
    by_row = jnp.where(valid_c[None, None], by_row, -1e30)
    return jnp.concatenate([by_row[:, :-1], by_row[:, 1:]], axis=-1).astype(F32)


def _na_mixer(qkv, rpb, norm_g):
    b = qkv.shape[0]
    table = _na_bias_tiles(rpb)
    return pl.pallas_call(
        _na_kernel,
        grid=(b,),
        in_specs=[pl.BlockSpec((1, SEQ, 3 * GROUP_W), lambda i: (i, 0, QKV_NA_BLOCK)),
                  pl.BlockSpec(table.shape, lambda i: (0, 0, 0, 0)),
                  pl.BlockSpec((1, GROUP_W), lambda i: (0, 0))],
        out_specs=pl.BlockSpec((1, SEQ, GROUP_W), lambda i: (i, 0, 0)),
        out_shape=jax.ShapeDtypeStruct((b, SEQ, GROUP_W), F32),
        compiler_params=_params("arbitrary"),
        name="na_mixer",
    )(qkv, table, norm_g.reshape(1, GROUP_W))


ML_STATE_ROWS = 80


def _ml_kernel(qkv_ref, og_ref, sm_ref, bcol_ref, g_ref, o_ref, bcol_s, rows_s, lrow_s, qt_s, vt_s, htf_s, htb_s,
               c_s, m_s):
    _gate_rows(sm_ref, rows_s, SM_IG, 16)
    raw = rows_s[...].reshape(N_CHUNKS * 16, CHUNK) + bcol_ref[...]
    pre, suf = _chunk_scans(jax.nn.log_sigmoid(raw), 1)
    rid = lax.broadcasted_iota(jnp.int32, (N_CHUNKS * 16, 1), 0) % 16
    scanned = jnp.where(rid < 8 + N_HEADS, pre, suf)
    beta = raw - pltpu.roll(scanned, N_CHUNKS * 16 - 8, 0)
    lrow_s[...] = jnp.where(rid < 8, beta, scanned).reshape(N_CHUNKS, 16, CHUNK)

    pad_rows = jnp.zeros((CHUNK - 8, CHUNK), F32)
    for c in range(N_CHUNKS):
        rows = slice(c * CHUNK, (c + 1) * CHUNK)
        qt_s[c] = qkv_ref[0, rows, 0:GROUP_W].astype(F32).T.astype(BF16)
        vt_s[c] = qkv_ref[0, rows, 2 * GROUP_W:3 * GROUP_W].astype(F32).T.astype(BF16)
        bcol_s[rows, :] = jnp.concatenate([lrow_s[c, 0:8, :], pad_rows], axis=0).T

    si = lax.broadcasted_iota(jnp.int32, (CHUNK, CHUNK), 0)
    ti = lax.broadcasted_iota(jnp.int32, (CHUNK, CHUNK), 1)
    pad = ML_STATE_ROWS - HEAD_DIM
    one_rows = (lax.broadcasted_iota(jnp.int32, (pad, CHUNK), 0) == 0).astype(BF16)
    k_scale = HEAD_DIM ** -0.5
    c_s[...] = jnp.zeros_like(c_s)
    m_s[...] = jnp.zeros_like(m_s)

    def chunk_body(i, _):
        chunk = {0: i, 1: N_CHUNKS - 1 - i}
        rows = {d: pl.multiple_of(chunk[d] * CHUNK, CHUNK) for d in range(2)}
        for d in range(2):
            phases([(d, h) for h in range(N_HEADS)], chunk, rows)
        return 0

    def phases(units, chunk, rows):
        ks, qts, vts, cexts, raw_s, inters = {}, {}, {}, {}, {}, {}
        for d, h in units:
            j = d * N_HEADS + h
            k = qkv_ref[0, pl.ds(rows[d], CHUNK), GROUP_W + h * HEAD_DIM:GROUP_W + (h + 1) * HEAD_DIM]
            ks[d, h] = k * k_scale
            qts[d, h] = qt_s[chunk[d], h * HEAD_DIM:(h + 1) * HEAD_DIM, :]
            vts[d, h] = vt_s[chunk[d], h * HEAD_DIM:(h + 1) * HEAD_DIM, :]
            cexts[d, h] = c_s[j]
            raw_s[d, h] = _dot(ks[d, h], qts[d, h])
            inters[d, h] = _dot(cexts[d, h].astype(BF16), qts[d, h])
        s_ts, w_inters, floors, sources, keeps = {}, {}, {}, {}, {}
        for d, h in units:
            j = d * N_HEADS + h
            mask = (si <= ti) if d == 0 else (si >= ti)
            end = CHUNK - 1 if d == 0 else 0
            lrow = lrow_s[chunk[d]]
            beta_r = lrow[j:j + 1, :]
            b_r = lrow[8 + j:8 + j + 1, :]
            m_prev = m_s[j][0:1, 0:1]
            beta_m = jnp.where(mask, bcol_s[pl.ds(rows[d], CHUNK), j:j + 1], -jnp.inf)
            mu = jnp.maximum(m_prev, jnp.max(beta_m, axis=0, keepdims=True))
            s_ts[d, h] = raw_s[d, h] * jnp.exp(beta_m - mu)
            w_inters[d, h] = jnp.exp(m_prev - mu)
            floors[d, h] = jnp.exp(-(b_r + mu))
            top = jnp.maximum(m_prev, jnp.max(beta_r, axis=1, keepdims=True))
            v_ext = jnp.concatenate([vts[d, h], one_rows], axis=0).astype(F32)
            sources[d, h] = (v_ext * jnp.exp(beta_r - top)).astype(BF16)
            keeps[d, h] = jnp.exp(m_prev - top)
            m_s[j] = jnp.broadcast_to(b_r[:, end:end + 1] + top, (8, CHUNK))
        hs = []
        for d, h in units:
            s_t = s_ts[d, h]
            inter = inters[d, h]
            num = w_inters[d, h] * inter[0:HEAD_DIM, :] + _dot(vts[d, h], s_t.astype(BF16))
            den = w_inters[d, h] * inter[HEAD_DIM:HEAD_DIM + 1, :] + jnp.sum(s_t, axis=0, keepdims=True)
            hs.append(num / jnp.maximum(jnp.abs(den), floors[d, h]))
        d = units[0][0]
        h_t = jnp.concatenate(hs, axis=0)
        if d == 0:
            htf_s[chunk[d]] = h_t
        else:
            htb_s[chunk[d]] = h_t
        for d, h in units:
            c_s[d * N_HEADS + h] = keeps[d, h] * cexts[d, h] + _dot(sources[d, h], ks[d, h])

    lax.fori_loop(0, N_CHUNKS, chunk_body, 0, unroll=2)

    gain = g_ref[...]

    def out_body(c, _):
        r0 = pl.multiple_of(c * CHUNK, CHUNK)
        h_t = htf_s[c] + htb_s[c]
        normed = []
        for h in range(N_HEADS):
            x = h_t[h * HEAD_DIM:(h + 1) * HEAD_DIM, :]
            normed.append(x * lax.rsqrt(jnp.mean(x * x, axis=0, keepdims=True) + NORM_EPS))
        y = jnp.concatenate(normed, axis=0).T * gain
        o_ref[0, pl.ds(r0, CHUNK), :] = jax.nn.sigmoid(og_ref[0, pl.ds(r0, CHUNK), :]) * y
        return 0

    lax.fori_loop(0, N_CHUNKS, out_body, 0)


def _ml_mixer(qkv, og, sm, i_bias, f_bias, norm_g):
    b = qkv.shape[0]
    bias16 = jnp.concatenate([i_bias.reshape(8), f_bias.reshape(8)])
    bcol = jnp.tile(bias16.reshape(16, 1), (N_CHUNKS, 1))
    full = lambda shape: pl.BlockSpec(shape, lambda i: (0,) * len(shape))
    return pl.pallas_call(
        _ml_kernel,
        grid=(b,),
        in_specs=[pl.BlockSpec((1, SEQ, 3 * GROUP_W), lambda i: (i, 0, QKV_ML_BLOCK)),
                  pl.BlockSpec((1, SEQ, GROUP_W), lambda i: (i, 0, WIDE_MLO_BLOCK)),
                  pl.BlockSpec((1, SEQ, CHUNK), lambda i: (i, 0, WIDE_SM_BLOCK)),
                  full((N_CHUNKS * 16, 1)), full((1, GROUP_W))],
        out_specs=pl.BlockSpec((1, SEQ, GROUP_W), lambda i: (i, 0, 0)),
        out_shape=jax.ShapeDtypeStruct((b, SEQ, GROUP_W), F32),
        scratch_shapes=[pltpu.VMEM((SEQ, CHUNK), F32), pltpu.VMEM((N_CHUNKS, 16, CHUNK), F32),
                        pltpu.VMEM((N_CHUNKS, 16, CHUNK), F32), pltpu.VMEM((N_CHUNKS, GROUP_W, CHUNK), BF16),
                        pltpu.VMEM((N_CHUNKS, GROUP_W, CHUNK), BF16), pltpu.VMEM((N_CHUNKS, GROUP_W, CHUNK), F32),
                        pltpu.VMEM((N_CHUNKS, GROUP_W, CHUNK), F32),
                        pltpu.VMEM((2 * N_HEADS, ML_STATE_ROWS, HEAD_DIM), F32),
                        pltpu.VMEM((2 * N_HEADS, 8, CHUNK), F32)],
        compiler_params=_params("arbitrary"),
        name="mlstm_mixer",
    )(qkv, og, sm, bcol, norm_g.reshape(1, GROUP_W))


def _dft_tables():
    f = np.arange(SEQ, dtype=np.int64)[:, None]
    t = np.arange(SEQ, dtype=np.int64)[None, :]
    ang = (2.0 * np.pi / FFT_N) * ((f * t) % FFT_N).astype(np.float64)
    wc = np.cos(ang)
    ws = np.sin(ang)
    ws[0, :] = np.where(np.arange(SEQ) % 2 == 0, 1.0, -1.0)
    w = np.stack([wc, ws]).reshape(2, N_FREQ_BLOCKS, FREQ_BLOCK, SEQ).transpose(1, 0, 2, 3)
    w = np.ascontiguousarray(w.reshape(FFT_N, SEQ)).astype(np.float32)
    return jnp.asarray(w).astype(BF16), jnp.asarray(np.ascontiguousarray(w.T)).astype(BF16)


def _hy_features():
    t = jnp.linspace(0.0, 1.0, SEQ, dtype=F32)[:, None]
    bands = (HY_EMB - 1) // 2
    f = jnp.linspace(1e-4, bands - 1, bands, dtype=F32)
    ang = (2.0 * math.pi) * (jnp.arange(SEQ, dtype=F32) / SEQ)[:, None] * f[None, :]
    feats = jnp.concatenate([t, jnp.cos(ang), -jnp.sin(ang)], axis=-1)
    deltas = jnp.abs(jnp.linspace(math.log(1e-2) / 1.5, math.log(1e-2) / 0.3, GROUP_W, dtype=F32))
    return jnp.pad(feats, ((0, 0), (0, CHUNK - HY_EMB))), t, deltas.reshape(1, GROUP_W)


def _hy_filter_kernel(feats_ref, w1_ref, b1_ref, w2_ref, b2_ref, w3_ref, freq_ref, t_ref, dl_ref, hp_ref, hm_ref):
    h = jnp.sin(freq_ref[0:1, :] * (_dot(feats_ref[...].astype(BF16), w1_ref[...].astype(BF16)) + b1_ref[...]))
    h = jnp.sin(freq_ref[1:2, :] * (_dot(h.astype(BF16), w2_ref[...].astype(BF16)) + b2_ref[...]))
    h = _dot(h.astype(BF16), w3_ref[...].astype(BF16))
    win = jnp.exp(-t_ref[...] * dl_ref[...])
    row = lax.broadcasted_iota(jnp.int32, (SEQ, 1), 0)
    for o in range(2):
        hf = h[:, (2 * o) * GROUP_W:(2 * o + 1) * GROUP_W] * win
        hb = h[:, (2 * o + 1) * GROUP_W:(2 * o + 2) * GROUP_W] * win
        r = lax.rsqrt(jnp.sum(hf * hf + hb * hb, axis=0, keepdims=True) + NORM_EPS)
        hf = hf * r
        hb = jnp.where(row == 0, 0.0, hb * r)
        hp_ref[:, o * GROUP_W:(o + 1) * GROUP_W] = (hf + hb).astype(BF16)
        hm_ref[:, o * GROUP_W:(o + 1) * GROUP_W] = (hf - hb).astype(BF16)


def _hy_spec_kernel(wc_ref, ws_ref, hp_ref, hm_ref, kc_ref, ks_ref):
    rowg = lax.broadcasted_iota(jnp.int32, (FREQ_BLOCK, 1), 0) + pl.program_id(0) * FREQ_BLOCK
    scale = jnp.where(rowg == 0, 1.0 / FFT_N, 2.0 / FFT_N)
    kc_ref[...] = _dot(wc_ref[...], hp_ref[...]) * scale
    nyq = _dot(ws_ref[0:8, :], hp_ref[...])[0:1, :]
    ks_ref[...] = jnp.where(rowg == 0, nyq, _dot(ws_ref[...], hm_ref[...])) * scale


def _hy_spectrum(w, w1, b1, w2, b2, w3, freq):
    feats, t, deltas = _hy_features()
    w1p = jnp.pad(w1, ((0, CHUNK - HY_EMB), (0, 0)))
    hp, hm = pl.pallas_call(
        _hy_filter_kernel,
        out_shape=[jax.ShapeDtypeStruct((SEQ, 2 * GROUP_W), BF16)] * 2,
        compiler_params=pltpu.CompilerParams(vmem_limit_bytes=VMEM_LIMIT),
        name="hyena_filter",
    )(feats, w1p, b1.reshape(1, HY_FFN), w2, b2.reshape(1, HY_FFN), w3, freq, t, deltas)
    nj = N_FREQ_BLOCKS
    return pl.pallas_call(
        _hy_spec_kernel,
        grid=(nj,),
        in_specs=[pl.BlockSpec((FREQ_BLOCK, SEQ), lambda j: (2 * j, 0)),
                  pl.BlockSpec((FREQ_BLOCK, SEQ), lambda j: (2 * j + 1, 0)),
                  pl.BlockSpec((SEQ, 2 * GROUP_W), lambda j: (0, 0)),
                  pl.BlockSpec((SEQ, 2 * GROUP_W), lambda j: (0, 0))],
        out_specs=[pl.BlockSpec((FREQ_BLOCK, 2 * GROUP_W), lambda j: (j, 0))] * 2,
        out_shape=[jax.ShapeDtypeStruct((SEQ, 2 * GROUP_W), F32)] * 2,
        compiler_params=_params("arbitrary"),
        name="hyena_spectrum",
    )(w, w, hp, hm)


def _hy_conv_kernel(z_ref, gate_ref, cwz_ref, cbz_ref, cwg_ref, cbg_ref, w_ref, wt_ref, kc_ref, ks_ref, skip_ref, g_ref,
                    o_ref, zb_s, acc_s, *, conv_z, final_norm):
    j = pl.program_id(1)
    n_b = z_ref.shape[0]
    cols = slice(0, GROUP_W)

    def z_rows(b, r0):
        if conv_z:
            return _dwconv3_rows(z_ref, b, r0, cols, cwz_ref[...], cbz_ref[...])
        return z_ref[b, pl.ds(r0, CHUNK), :]

    @pl.when(j == 0)
    def _():
        for b in range(n_b):
            def fill(c, _, b=b):
                r0 = pl.multiple_of(c * CHUNK, CHUNK)
                zb_s[pl.ds(r0, CHUNK), b * GROUP_W:(b + 1) * GROUP_W] = z_rows(b, r0).astype(BF16)
                return 0

            lax.fori_loop(0, N_CHUNKS, fill, 0)
        acc_s[...] = jnp.zeros_like(acc_s)

    row0 = (lax.broadcasted_iota(jnp.int32, (FREQ_BLOCK, 1), 0) + j * FREQ_BLOCK) == 0
    kc = jnp.concatenate([kc_ref[...]] * n_b, axis=1)
    ks = jnp.concatenate([ks_ref[...]] * n_b, axis=1)
    x = _dot(w_ref[...], zb_s[...])
    xc = x[0:FREQ_BLOCK, :]
    xs = x[FREQ_BLOCK:, :]
    xs_ks = xs * ks
    yc = xc * kc - jnp.where(row0, 0.0, xs_ks)
    ys = jnp.where(row0, xs_ks, xc * ks + xs * kc)
    acc_s[...] += _dot(wt_ref[...], jnp.concatenate([yc, ys], axis=0).astype(BF16))

    @pl.when(j == pl.num_programs(1) - 1)
    def _():
        for b in range(n_b):
            def finish(c, _, b=b):
                r0 = pl.multiple_of(c * CHUNK, CHUNK)
                gate = _dwconv3_rows(gate_ref, b, r0, cols, cwg_ref[...], cbg_ref[...])
                y = acc_s[pl.ds(r0, CHUNK), b * GROUP_W:(b + 1) * GROUP_W]
                r = gate * (y + skip_ref[...] * z_rows(b, r0))
                if final_norm:
                    r = _rms(r) * g_ref[...]
                o_ref[b, pl.ds(r0, CHUNK), :] = r
                return 0

            lax.fori_loop(0, N_CHUNKS, finish, 0)


def _hy_long_conv(z, z_col, gates, gate_col, conv_w, conv_b, conv_z, w, wt, kc, ks, order, skip, norm_g, final_norm):
    b = z.shape[0]
    bg = HY_BATCH_BLOCK
    nj = N_FREQ_BLOCKS
    taps = lambda col: (conv_w[:, col * GROUP_W:(col + 1) * GROUP_W],
                        conv_b[col * GROUP_W:(col + 1) * GROUP_W].reshape(1, GROUP_W))
    small = lambda shape: pl.BlockSpec(shape, lambda g, j: (0, 0))
    return pl.pallas_call(
        functools.partial(_hy_conv_kernel, conv_z=conv_z, final_norm=final_norm),
        grid=(b // bg, nj),
        in_specs=[pl.BlockSpec((bg, SEQ, GROUP_W), lambda g, j: (g, 0, z_col)),
                  pl.BlockSpec((bg, SEQ, GROUP_W), lambda g, j: (g, 0, gate_col)),
                  small((3, GROUP_W)), small((1, GROUP_W)), small((3, GROUP_W)), small((1, GROUP_W)),
                  pl.BlockSpec((2 * FREQ_BLOCK, SEQ), lambda g, j: (j, 0)),
                  pl.BlockSpec((SEQ, 2 * FREQ_BLOCK), lambda g, j: (0, j)),
                  pl.BlockSpec((FREQ_BLOCK, GROUP_W), lambda g, j: (j, order)),
                  pl.BlockSpec((FREQ_BLOCK, GROUP_W), lambda g, j: (j, order)),
                  pl.BlockSpec((1, GROUP_W), lambda g, j: (0, 0)),
                  pl.BlockSpec((1, GROUP_W), lambda g, j: (0, 0))],
        out_specs=pl.BlockSpec((bg, SEQ, GROUP_W), lambda g, j: (g, 0, 0)),
        out_shape=jax.ShapeDtypeStruct((b, SEQ, GROUP_W), F32),
        scratch_shapes=[pltpu.VMEM((SEQ, bg * GROUP_W), BF16), pltpu.VMEM((SEQ, bg * GROUP_W), F32)],
        compiler_params=_params("arbitrary", "arbitrary"),
        name="hyena_long_conv",
    )(z, gates, *taps(0), *taps(order + 1), w, wt, kc, ks, skip.reshape(1, GROUP_W), norm_g.reshape(1, GROUP_W))


def _hy_mixer(u, w, wt, conv_w, conv_b, w1, b1, w2, b2, w3, freq, skip, norm_g):
    kc, ks = _hy_spectrum(w, w1, b1, w2, b2, w3, freq)
    first = WIDE_HY_BLOCK * 3
    z1 = _hy_long_conv(u, first, u, first + 1, conv_w, conv_b, True, w, wt, kc, ks, 0, skip[0], norm_g, False)
    return _hy_long_conv(z1, 0, u, first + 2, conv_w, conv_b, False, w, wt, kc, ks, 1, skip[1], norm_g, True)


FFN_SPLITS = ((0, 1024), (1024, 2048), (2048, FFN_HIDDEN))


def _ffn_kernel(x_ref, mod_ref, ya_ref, yb_ref, yc_ref, yd_ref, wo_ref, wg_ref, wu_ref, wd_ref, gf_ref, o_ref, *,
                final_norm):
    y_mix = jnp.concatenate([ya_ref[...], yb_ref[...], yc_ref[...], yd_ref[...]], axis=1).astype(BF16)
    x = x_ref[...] + mod_ref[0, 2:3, :] * _dot(y_mix, wo_ref[...])
    h = (_rms(x) * (1.0 + mod_ref[0, 4:5, :]) + mod_ref[0, 3:4, :]).astype(BF16)
    acc = jnp.zeros_like(x)
    for lo, hi in FFN_SPLITS:
        a = _silu(_dot(h, wg_ref[:, lo:hi])) * _dot(h, wu_ref[:, lo:hi])
        acc = acc + _dot(a.astype(BF16), wd_ref[lo:hi, :])
    y = x + mod_ref[0, 5:6, :] * acc
    if final_norm:
        y = _rms(y) * gf_ref[...]
    o_ref[...] = y


def _out_ffn(x2, mod_l, ys, w_out, wg, wu, wd, final_g, final_norm):
    n_tok, d = x2.shape
    tm = TOKEN_TILE
    tiles_per_seq = SEQ // tm
    resident = lambda shape: pl.BlockSpec(shape, lambda i: (0, 0), pipeline_mode=pl.Buffered(1))
    return pl.pallas_call(
        functools.partial(_ffn_kernel, final_norm=final_norm),
        grid=(n_tok // tm,),
        in_specs=[pl.BlockSpec((tm, d), lambda i: (i, 0)),
                  pl.BlockSpec((1, 6, d), lambda i: (i // tiles_per_seq, 0, 0))]
                 + [pl.BlockSpec((tm, GROUP_W), lambda i: (i, 0))] * 4
                 + [resident((d, d)), resident((d, FFN_HIDDEN)), resident((d, FFN_HIDDEN)), resident((FFN_HIDDEN, d)),
                    pl.BlockSpec((1, d), lambda i: (0, 0))],
        out_specs=pl.BlockSpec((tm, d), lambda i: (i, 0)),
        out_shape=jax.ShapeDtypeStruct((n_tok, d), F32),
        compiler_params=_params("arbitrary"),
        name="out_proj_swiglu",
    )(x2, mod_l, *ys, w_out, wg, wu, wd, final_g.reshape(1, d))


def _permute_w_in(w):
    cols = [w[:, 2840:3608], w[:, 2568:2824], w[:, 0:1024], w[:, 1024:1032], w[:, 2824:2840],
            jnp.zeros((w.shape[0], P_PAD - 3608), w.dtype), w[:, 1032:1800], w[:, 1800:2568]]
    return jnp.concatenate(cols, axis=1).astype(BF16)


def kernel(x, c, mod_w, mod_b, w_in, ssd_conv_w, ssd_conv_b, ssd_dt_bias, ssd_a_log, ssd_d, ssd_norm_g, na_rpb, na_norm_g, ml_i_bias, ml_f_bias, ml_norm_g, hy_conv_w, hy_conv_b, hy_w1, hy_b1, hy_w2, hy_b2, hy_w3, hy_freq, hy_skip, hy_norm_g, w_out, ffn_w_gate, ffn_w_up, ffn_w_down, final_norm_g):
    b, seq, d = x.shape
    assert seq == SEQ and d == D_MODEL and b % HY_BATCH_BLOCK == 0
    depth = mod_w.shape[0]
    mod = _modulation(c, mod_w, mod_b).reshape(depth, b, 6, d)
    dft, dft_t = _dft_tables()
    x2 = x.reshape(b * seq, d)
    for l in range(depth):
        wide, qkv = _in_proj(x2, mod[l], _permute_w_in(w_in[l]))
        wide = wide.reshape(b, seq, WIDE_W)
        qkv = qkv.reshape(b, seq, QKV_W)
        y_ssd = _ssd_mixer(wide, wide, ssd_conv_w[l], ssd_conv_b[l], ssd_dt_bias[l], ssd_a_log[l], ssd_d[l],
                           ssd_norm_g[l])
        y_na = _na_mixer(qkv, na_rpb[l], na_norm_g[l])
        y_ml = _ml_mixer(qkv, wide, wide, ml_i_bias[l], ml_f_bias[l], ml_norm_g[l])
        y_hy = _hy_mixer(wide, dft, dft_t, hy_conv_w[l], hy_conv_b[l], hy_w1[l], hy_b1[l], hy_w2[l], hy_b2[l],
                         hy_w3[l], hy_freq[l], hy_skip[l], hy_norm_g[l])
        ys = [t.reshape(b * seq, GROUP_W) for t in (y_ssd, y_na, y_ml, y_hy)]
        x2 = _out_ffn(x2, mod[l], ys, w_out[l].astype(BF16), ffn_w_gate[l].astype(BF16), ffn_w_up[l].astype(BF16),
                      ffn_w_down[l].astype(BF16), final_norm_g, l == depth - 1)
    return x2.reshape(b, seq, d)
```

```python
import functools
import math

import numpy as np
import jax
import jax.numpy as jnp
from jax import lax
from jax.experimental import pallas as pl
from jax.experimental.pallas import tpu as pltpu

F32 = jnp.float32
BF16 = jnp.bfloat16

D_MODEL = 1024
SEQ = 2048
GRID_W = 64
GROUP_W = 256
HEAD_DIM = 64
N_HEADS = 4
SSD_STATE = 128
SSD_XBC = 768
CHUNK = 128
N_CHUNKS = SEQ // CHUNK
NA_KH = 8
NA_KW = 16
NA_ROW_GROUP = 4
HY_EMB = 33
HY_FFN = 64
FFN_HIDDEN = 2816
NORM_EPS = 1e-6
FFT_N = 2 * SEQ
FREQ_BLOCK = 256
N_FREQ_BLOCKS = SEQ // 2 // FREQ_BLOCK
HY_BATCH_BLOCK = 2
TOKEN_TILE = 512
VMEM_LIMIT = 56 * 1024 * 1024

WIDE_W = 1408
WIDE_ZX_BLOCK = 0
WIDE_MLO_BLOCK = 4
WIDE_SM_BLOCK = 10
QKV_W = 1536
QKV_NA_BLOCK = 0
QKV_ML_BLOCK = 1
HY_W = 768
P_PAD = WIDE_W + QKV_W + HY_W
SM_DT = 0
SM_IG = 8
SM_FG = 16

NT_DIMS = (((1,), (1,)), ((), ()))
TN_DIMS = (((0,), (0,)), ((), ()))


def _rms(x):
    return x * lax.rsqrt(jnp.mean(x * x, axis=-1, keepdims=True) + NORM_EPS)


def _silu(x):
    return x * jax.nn.sigmoid(x)


def _dot(a, b):
    return jnp.dot(a, b, preferred_element_type=F32)


def _params(*sem):
    return pltpu.CompilerParams(dimension_semantics=sem, vmem_limit_bytes=VMEM_LIMIT)


def _mod_kernel(c_ref, w_ref, b_ref, o_ref):
    cond = _silu(c_ref[...]).astype(BF16)
    o_ref[0] = _dot(cond, w_ref[0].astype(BF16)) + b_ref[0]


def _modulation(c, mod_w, mod_b):
    depth, d, n = mod_w.shape
    b = c.shape[0]
    nb = n // d
    return pl.pallas_call(
        _mod_kernel,
        grid=(depth, nb),
        in_specs=[pl.BlockSpec((b, d), lambda l, j: (0, 0)),
                  pl.BlockSpec((1, d, d), lambda l, j: (l, 0, j)),
                  pl.BlockSpec((1, 1, d), lambda l, j: (l, 0, j))],
        out_specs=pl.BlockSpec((1, b, d), lambda l, j: (l, 0, j)),
        out_shape=jax.ShapeDtypeStruct((depth, b, n), F32),
        compiler_params=_params("arbitrary", "arbitrary"),
        name="adaln_mod",
    )(c, mod_w, mod_b.reshape(depth, 1, n))


def _in_proj_kernel(x_ref, mod_ref, w_ref, wide_ref, qkv_ref, hy_ref):
    h = (_rms(x_ref[...]) * (1.0 + mod_ref[0, 1:2, :]) + mod_ref[0, 0:1, :]).astype(BF16)
    wide_ref[...] = _dot(h, w_ref[:, 0:WIDE_W])
    qkv_ref[...] = _dot(h, w_ref[:, WIDE_W:WIDE_W + QKV_W]).astype(BF16)
    hy_ref[...] = _dot(h, w_ref[:, WIDE_W + QKV_W:P_PAD])


def _in_proj(x2, mod_l, w_perm):
    n_tok, d = x2.shape
    tm = TOKEN_TILE
    tiles_per_seq = SEQ // tm
    return pl.pallas_call(
        _in_proj_kernel,
        grid=(n_tok // tm,),
        in_specs=[pl.BlockSpec((tm, d), lambda i: (i, 0)),
                  pl.BlockSpec((1, 6, d), lambda i: (i // tiles_per_seq, 0, 0)),
                  pl.BlockSpec((d, P_PAD), lambda i: (0, 0), pipeline_mode=pl.Buffered(1))],
        out_specs=[pl.BlockSpec((tm, w), lambda i: (i, 0)) for w in (WIDE_W, QKV_W, HY_W)],
        out_shape=[jax.ShapeDtypeStruct((n_tok, WIDE_W), F32), jax.ShapeDtypeStruct((n_tok, QKV_W), BF16),
                   jax.ShapeDtypeStruct((n_tok, HY_W), F32)],
        compiler_params=_params("arbitrary"),
        name="in_proj",
    )(x2, mod_l, w_perm)


def _dwconv3_rows(src_ref, b, r0, cols, w, bias):
    x = src_ref[b, pl.ds(r0, CHUNK), cols]
    up = src_ref[b, pl.ds(jnp.maximum(r0 - 1, 0), 1), cols]
    dn = src_ref[b, pl.ds(jnp.minimum(r0 + CHUNK, SEQ - 1), 1), cols]
    up = jnp.where(r0 > 0, up, 0.0)
    dn = jnp.where(r0 + CHUNK < SEQ, dn, 0.0)
    row = lax.broadcasted_iota(jnp.int32, (CHUNK, 1), 0)
    prev = jnp.where(row == 0, up, pltpu.roll(x, 1, 0))
    nxt = jnp.where(row == CHUNK - 1, dn, pltpu.roll(x, CHUNK - 1, 0))
    return w[0:1, :] * prev + w[1:2, :] * x + w[2:3, :] * nxt + bias


def _chunk_scans(a, axis):
    n = a.shape[axis]
    shape = [1, 1]
    shape[axis] = n
    pos = lax.broadcasted_iota(jnp.int32, tuple(shape), axis) % CHUNK
    pre, suf = a, a
    k = 1
    while k < CHUNK:
        pre = pre + jnp.where(pos >= k, pltpu.roll(pre, k, axis), 0.0)
        suf = suf + jnp.where(pos < CHUNK - k, pltpu.roll(suf, n - k, axis), 0.0)
        k *= 2
    return pre, suf


def _gate_rows(sm_ref, rows_s, lo, n):
    for c in range(N_CHUNKS):
        t = sm_ref[0, c * CHUNK:(c + 1) * CHUNK, :].T
        rows_s[c] = t[lo:lo + n, :]


def _ssd_kernel(zx_ref, sm_ref, cw_ref, cb_ref, dtb_col_ref, a_col_ref, dsk_ref, g_ref,
                o_ref, xs_s, b_s, xst_s, ct_s, acol_s, rows_s, dtrow_s, arow_s, ytf_s, ytb_s, st_s):
    cw = cw_ref[...]
    cb = cb_ref[...]

    def conv_body(c, _):
        r0 = pl.multiple_of(c * CHUNK, CHUNK)
        v = _silu(_dwconv3_rows(zx_ref, 0, r0, slice(GROUP_W, GROUP_W + SSD_XBC), cw, cb))
        xs_s[pl.ds(r0, CHUNK), :] = v[:, 0:GROUP_W]
        b_s[pl.ds(r0, CHUNK), :] = v[:, GROUP_W:2 * GROUP_W].astype(BF16)
        xst_s[c] = v[:, 0:GROUP_W].T
        ct_s[c] = v[:, 2 * GROUP_W:3 * GROUP_W].T.astype(BF16)
        return 0

    lax.fori_loop(0, N_CHUNKS, conv_body, 0)

    _gate_rows(sm_ref, rows_s, SM_DT, 8)
    dt_row = jax.nn.softplus(rows_s[...].reshape(N_CHUNKS * 8, CHUNK) + dtb_col_ref[...])
    pre, suf = _chunk_scans(dt_row * (-jnp.exp(a_col_ref[...])), 1)
    rid = lax.broadcasted_iota(jnp.int32, (N_CHUNKS * 8, 1), 0) % 8
    dtrow_s[...] = dt_row.reshape(N_CHUNKS, 8, CHUNK)
    arow_s[...] = jnp.where(rid < N_HEADS, pre, suf).reshape(N_CHUNKS, 8, CHUNK)
    pad_rows = jnp.zeros((CHUNK - 8, CHUNK), F32)
    for c in range(N_CHUNKS):
        acol_s[c * CHUNK:(c + 1) * CHUNK, :] = jnp.concatenate([arow_s[c], pad_rows], axis=0).T

    si = lax.broadcasted_iota(jnp.int32, (CHUNK, CHUNK), 0)
    ti = lax.broadcasted_iota(jnp.int32, (CHUNK, CHUNK), 1)
    first = lax.broadcasted_iota(jnp.int32, (2 * HEAD_DIM, 1), 0) < HEAD_DIM
    st_s[...] = jnp.zeros_like(st_s)

    def chunk_body(i, _):
        units = [(d, g) for d in range(2) for g in range(2)]
        chunk = {0: i, 1: N_CHUNKS - 1 - i}
        rows = {d: pl.multiple_of(chunk[d] * CHUNK, CHUNK) for d in range(2)}
        bgs, cgs, prevs, scores, y_offs = {}, {}, {}, {}, {}
        for d, g in units:
            bgs[d, g] = b_s[pl.ds(rows[d], CHUNK), g * SSD_STATE:(g + 1) * SSD_STATE]
            cgs[d, g] = ct_s[chunk[d], g * SSD_STATE:(g + 1) * SSD_STATE, :]
            prevs[d, g] = st_s[d, g]
            scores[d, g] = _dot(bgs[d, g], cgs[d, g])
            y_offs[d, g] = _dot(prevs[d, g].astype(BF16), cgs[d, g])
        weighted, xdts, x_ends, carries, grow = {}, {}, {}, {}, {}
        for d, g in units:
            mask = (si <= ti) if d == 0 else (si >= ti)
            end = CHUNK - 1 if d == 0 else 0
            acol = acol_s[pl.ds(rows[d], CHUNK), :]
            arow = arow_s[chunk[d]]
            dtr = dtrow_s[chunk[d]]
            for hh in range(2):
                h = 2 * g + hh
                j = d * N_HEADS + h
                ar = arow[j:j + 1, :]
                a_end = ar[:, end:end + 1]
                decay = jnp.exp(jnp.where(mask, ar - acol[:, j:j + 1], -jnp.inf))
                xdt_t = xst_s[chunk[d], h * HEAD_DIM:(h + 1) * HEAD_DIM, :] * dtr[j:j + 1, :]
                weighted[d, h] = (scores[d, g] * decay).astype(BF16)
                xdts[d, h] = xdt_t.astype(BF16)
                x_ends[d, h] = (xdt_t * jnp.exp(a_end - ar)).astype(BF16)
                carries[d, h] = jnp.exp(a_end)
                grow[d, h] = jnp.exp(ar)
        for d in range(2):
            ys = []
            for h in range(N_HEADS):
                g, hh = divmod(h, 2)
                y_off = y_offs[d, g][hh * HEAD_DIM:(hh + 1) * HEAD_DIM, :]
                ys.append(_dot(xdts[d, h], weighted[d, h]) + y_off * grow[d, h])
            y_t = jnp.concatenate(ys, axis=0)
            if d == 0:
                ytf_s[chunk[d]] = y_t
            else:
                ytb_s[chunk[d]] = y_t
        for d, g in units:
            keep = jnp.where(first, carries[d, 2 * g], carries[d, 2 * g + 1])
            x_end = jnp.concatenate([x_ends[d, 2 * g], x_ends[d, 2 * g + 1]], axis=0)
            st_s[d, g] = prevs[d, g] * keep + _dot(x_end, bgs[d, g])
        return 0

    lax.fori_loop(0, N_CHUNKS, chunk_body, 0, unroll=2)

    dsk = dsk_ref[...]
    gain = g_ref[...]

    def out_body(c, _):
        r0 = pl.multiple_of(c * CHUNK, CHUNK)
        y = (ytf_s[c] + ytb_s[c]).T + dsk * xs_s[pl.ds(r0, CHUNK), :]
        y = y * _silu(zx_ref[0, pl.ds(r0, CHUNK), 0:GROUP_W])
        o_ref[0, pl.ds(r0, CHUNK), :] = _rms(y) * gain
        return 0

    lax.fori_loop(0, N_CHUNKS, out_body, 0)


def _ssd_mixer(zx, sm, conv_w, conv_b, dt_bias, a_log, d_skip, norm_g):
    b = zx.shape[0]
    tile_col = lambda v: jnp.tile(v.reshape(8, 1), (N_CHUNKS, 1))
    full = lambda shape: pl.BlockSpec(shape, lambda i: (0,) * len(shape))
    return pl.pallas_call(
        _ssd_kernel,
        grid=(b,),
        in_specs=[pl.BlockSpec((1, SEQ, 1024), lambda i: (i, 0, WIDE_ZX_BLOCK)),
                  pl.BlockSpec((1, SEQ, CHUNK), lambda i: (i, 0, WIDE_SM_BLOCK)),
                  full((3, SSD_XBC)), full((1, SSD_XBC)),
                  full((N_CHUNKS * 8, 1)), full((N_CHUNKS * 8, 1)), full((1, GROUP_W)), full((1, GROUP_W))],
        out_specs=pl.BlockSpec((1, SEQ, GROUP_W), lambda i: (i, 0, 0)),
        out_shape=jax.ShapeDtypeStruct((b, SEQ, GROUP_W), F32),
        scratch_shapes=[pltpu.VMEM((SEQ, GROUP_W), F32), pltpu.VMEM((SEQ, GROUP_W), BF16),
                        pltpu.VMEM((N_CHUNKS, GROUP_W, CHUNK), F32), pltpu.VMEM((N_CHUNKS, GROUP_W, CHUNK), BF16),
                        pltpu.VMEM((SEQ, CHUNK), F32), pltpu.VMEM((N_CHUNKS, 8, CHUNK), F32),
                        pltpu.VMEM((N_CHUNKS, 8, CHUNK), F32), pltpu.VMEM((N_CHUNKS, 8, CHUNK), F32),
                        pltpu.VMEM((N_CHUNKS, GROUP_W, CHUNK), F32), pltpu.VMEM((N_CHUNKS, GROUP_W, CHUNK), F32),
                        pltpu.VMEM((2, 2, 2 * HEAD_DIM, SSD_STATE), F32)],
        compiler_params=_params("arbitrary"),
        name="ssd_mixer",
    )(zx, sm, conv_w, conv_b.reshape(1, SSD_XBC), tile_col(dt_bias), tile_col(a_log),
      jnp.repeat(d_skip, HEAD_DIM).reshape(1, GROUP_W), norm_g.reshape(1, GROUP_W))


def _na_kernel(qkv_ref, bias_ref, g_ref, o_ref):
    n_rows = SEQ // GRID_W
    k_tok = NA_KH * GRID_W
    pair_w = 2 * HEAD_DIM
    first = lax.broadcasted_iota(jnp.int32, (1, pair_w), 1) < HEAD_DIM
    gain = g_ref[...]

    def row_group(g, _):
        units = [(i, pair) for i in range(NA_ROW_GROUP) for pair in range(N_HEADS // 2)]
        q0, k0, dr0, scores, values = {}, {}, {}, {}, {}
        for i in range(NA_ROW_GROUP):
            r = g * NA_ROW_GROUP + i
            start = jnp.clip(r - NA_KH // 2, 0, n_rows - NA_KH)
            dr0[i] = start - r + NA_KH - 1
            q0[i] = pl.multiple_of(r * GRID_W, GRID_W)
            k0[i] = pl.multiple_of(start * GRID_W, GRID_W)
        for i, pair in units:
            lo = pair * pair_w
            q2 = qkv_ref[0, pl.ds(q0[i], GRID_W), lo:lo + pair_w] * (HEAD_DIM ** -0.5)
            k2 = qkv_ref[0, pl.ds(k0[i], k_tok), GROUP_W + lo:GROUP_W + lo + pair_w]
            values[i, pair] = qkv_ref[0, pl.ds(k0[i], k_tok), 2 * GROUP_W + lo:2 * GROUP_W + lo + pair_w]
            qm = jnp.concatenate([jnp.where(first, q2, 0.0), jnp.where(first, 0.0, q2)], axis=0).astype(BF16)
            scores[i, pair] = lax.dot_general(qm, k2, NT_DIMS, preferred_element_type=F32)
        probs, sums = {}, {}
        for i, pair in units:
            bias = jnp.concatenate(
                [jnp.concatenate([bias_ref[2 * pair + hh, dr0[i] + 2 * m] for m in range(NA_KH // 2)], axis=1)
                 for hh in range(2)], axis=0)
            s = scores[i, pair] + bias
            p = jnp.exp(s - jnp.max(s, axis=-1, keepdims=True))
            probs[i, pair] = p.astype(BF16)
            sums[i, pair] = jnp.sum(p, axis=-1, keepdims=True)
        for i in range(NA_ROW_GROUP):
            outs = []
            for pair in range(N_HEADS // 2):
                o2 = _dot(probs[i, pair], values[i, pair]) / sums[i, pair]
                outs.append(jnp.where(first, o2[0:GRID_W, :], o2[GRID_W:, :]))
            o_ref[0, pl.ds(q0[i], GRID_W), :] = _rms(jnp.concatenate(outs, axis=1)) * gain
        return 0

    lax.fori_loop(0, n_rows // NA_ROW_GROUP, row_group, 0, unroll=2)


def _na_bias_tiles(rpb):
    col = np.arange(GRID_W)
    cs = np.clip(col - NA_KW // 2, 0, GRID_W - NA_KW)
    valid_c = (col[None, :] >= cs[:, None]) & (col[None, :] < cs[:, None] + NA_KW)
    dc = col[None, :] - col[:, None] + NA_KW - 1
    sel_c = (valid_c[None] & (dc[None] == np.arange(2 * NA_KW - 1)[:, None, None])).astype(np.float32)
    by_row = jnp.einsum("hab,bqk->haqk", rpb, sel_c, precision=lax.Precision.HIGHEST)
    by_row = jnp.where(valid_c[None, None], by_row, -1e30)
    return jnp.concatenate([by_row[:, :-1], by_row[:, 1:]], axis=-1).astype(F32)


def _na_mixer(qkv, rpb, norm_g):
    b = qkv.shape[0]
    table = _na_bias_tiles(rpb)
    return pl.pallas_call(
        _na_kernel,
        grid=(b,),
        in_specs=[pl.BlockSpec((1, SEQ, 3 * GROUP_W), lambda i: (i, 0, QKV_NA_BLOCK)),
                  pl.BlockSpec(table.shape, lambda i: (0, 0, 0, 0)),
                  pl.BlockSpec((1, GROUP_W), lambda i: (0, 0))],
        out_specs=pl.BlockSpec((1, SEQ, GROUP_W), lambda i: (i, 0, 0)),
        out_shape=jax.ShapeDtypeStruct((b, SEQ, GROUP_W), F32),
        compiler_params=_params("arbitrary"),
        name="na_mixer",
    )(qkv, table, norm_g.reshape(1, GROUP_W))


ML_STATE_ROWS = 80


def _ml_kernel(qkv_ref, og_ref, sm_ref, bcol_ref, g_ref, o_ref, bcol_s, rows_s, lrow_s, qt_s, vt_s, htf_s, htb_s,
               c_s, m_s):
    _gate_rows(sm_ref, rows_s, SM_IG, 16)
    raw = rows_s[...].reshape(N_CHUNKS * 16, CHUNK) + bcol_ref[...]
    pre, suf = _chunk_scans(jax.nn.log_sigmoid(raw), 1)
    rid = lax.broadcasted_iota(jnp.int32, (N_CHUNKS * 16, 1), 0) % 16
    scanned = jnp.where(rid < 8 + N_HEADS, pre, suf)
    beta = raw - pltpu.roll(scanned, N_CHUNKS * 16 - 8, 0)
    lrow_s[...] = jnp.where(rid < 8, beta, scanned).reshape(N_CHUNKS, 16, CHUNK)

    pad_rows = jnp.zeros((CHUNK - 8, CHUNK), F32)
    for c in range(N_CHUNKS):
        rows = slice(c * CHUNK, (c + 1) * CHUNK)
        qt_s[c] = qkv_ref[0, rows, 0:GROUP_W].astype(F32).T.astype(BF16)
        vt_s[c] = qkv_ref[0, rows, 2 * GROUP_W:3 * GROUP_W].astype(F32).T.astype(BF16)
        bcol_s[rows, :] = jnp.concatenate([lrow_s[c, 0:8, :], pad_rows], axis=0).T

    si = lax.broadcasted_iota(jnp.int32, (CHUNK, CHUNK), 0)
    ti = lax.broadcasted_iota(jnp.int32, (CHUNK, CHUNK), 1)
    pad = ML_STATE_ROWS - HEAD_DIM
    one_rows = (lax.broadcasted_iota(jnp.int32, (pad, CHUNK), 0) == 0).astype(BF16)
    k_scale = HEAD_DIM ** -0.5
    c_s[...] = jnp.zeros_like(c_s)
    m_s[...] = jnp.zeros_like(m_s)

    def chunk_body(i, _):
        chunk = {0: i, 1: N_CHUNKS - 1 - i}
        rows = {d: pl.multiple_of(chunk[d] * CHUNK, CHUNK) for d in range(2)}
        for d in range(2):
            phases([(d, h) for h in range(N_HEADS)], chunk, rows)
        return 0

    def phases(units, chunk, rows):
        ks, qts, vts, cexts, raw_s, inters = {}, {}, {}, {}, {}, {}
        for d, h in units:
            j = d * N_HEADS + h
            k = qkv_ref[0, pl.ds(rows[d], CHUNK), GROUP_W + h * HEAD_DIM:GROUP_W + (h + 1) * HEAD_DIM]
            ks[d, h] = k * k_scale
            qts[d, h] = qt_s[chunk[d], h * HEAD_DIM:(h + 1) * HEAD_DIM, :]
            vts[d, h] = vt_s[chunk[d], h * HEAD_DIM:(h + 1) * HEAD_DIM, :]
            cexts[d, h] = c_s[j]
            raw_s[d, h] = _dot(ks[d, h], qts[d, h])
            inters[d, h] = _dot(cexts[d, h].astype(BF16), qts[d, h])
        s_ts, w_inters, floors, sources, keeps = {}, {}, {}, {}, {}
        for d, h in units:
            j = d * N_HEADS + h
            mask = (si <= ti) if d == 0 else (si >= ti)
            end = CHUNK - 1 if d == 0 else 0
            lrow = lrow_s[chunk[d]]
            beta_r = lrow[j:j + 1, :]
            b_r = lrow[8 + j:8 + j + 1, :]
            m_prev = m_s[j][0:1, 0:1]
            beta_m = jnp.where(mask, bcol_s[pl.ds(rows[d], CHUNK), j:j + 1], -jnp.inf)
            mu = jnp.maximum(m_prev, jnp.max(beta_m, axis=0, keepdims=True))
            s_ts[d, h] = raw_s[d, h] * jnp.exp(beta_m - mu)
            w_inters[d, h] = jnp.exp(m_prev - mu)
            floors[d, h] = jnp.exp(-(b_r + mu))
            top = jnp.maximum(m_prev, jnp.max(beta_r, axis=1, keepdims=True))
            v_ext = jnp.concatenate([vts[d, h], one_rows], axis=0).astype(F32)
            sources[d, h] = (v_ext * jnp.exp(beta_r - top)).astype(BF16)
            keeps[d, h] = jnp.exp(m_prev - top)
            m_s[j] = jnp.broadcast_to(b_r[:, end:end + 1] + top, (8, CHUNK))
        hs = []
        for d, h in units:
            s_t = s_ts[d, h]
            inter = inters[d, h]
            num = w_inters[d, h] * inter[0:HEAD_DIM, :] + _dot(vts[d, h], s_t.astype(BF16))
            den = w_inters[d, h] * inter[HEAD_DIM:HEAD_DIM + 1, :] + jnp.sum(s_t, axis=0, keepdims=True)
            hs.append(num / jnp.maximum(jnp.abs(den), floors[d, h]))
        d = units[0][0]
        h_t = jnp.concatenate(hs, axis=0)
        if d == 0:
            htf_s[chunk[d]] = h_t
        else:
            htb_s[chunk[d]] = h_t
        for d, h in units:
            c_s[d * N_HEADS + h] = keeps[d, h] * cexts[d, h] + _dot(sources[d, h], ks[d, h])

    lax.fori_loop(0, N_CHUNKS, chunk_body, 0, unroll=2)

    gain = g_ref[...]

    def out_body(c, _):
        r0 = pl.multiple_of(c * CHUNK, CHUNK)
        h_t = htf_s[c] + htb_s[c]
        normed = []
        for h in range(N_HEADS):
            x = h_t[h * HEAD_DIM:(h + 1) * HEAD_DIM, :]
            normed.append(x * lax.rsqrt(jnp.mean(x * x, axis=0, keepdims=True) + NORM_EPS))
        y = jnp.concatenate(normed, axis=0).T * gain
        o_ref[0, pl.ds(r0, CHUNK), :] = jax.nn.sigmoid(og_ref[0, pl.ds(r0, CHUNK), :]) * y
        return 0

    lax.fori_loop(0, N_CHUNKS, out_body, 0)


def _ml_mixer(qkv, og, sm, i_bias, f_bias, norm_g):
    b = qkv.shape[0]
    bias16 = jnp.concatenate([i_bias.reshape(8), f_bias.reshape(8)])
    bcol = jnp.tile(bias16.reshape(16, 1), (N_CHUNKS, 1))
    full = lambda shape: pl.BlockSpec(shape, lambda i: (0,) * len(shape))
    return pl.pallas_call(
        _ml_kernel,
        grid=(b,),
        in_specs=[pl.BlockSpec((1, SEQ, 3 * GROUP_W), lambda i: (i, 0, QKV_ML_BLOCK)),
                  pl.BlockSpec((1, SEQ, GROUP_W), lambda i: (i, 0, WIDE_MLO_BLOCK)),
                  pl.BlockSpec((1, SEQ, CHUNK), lambda i: (i, 0, WIDE_SM_BLOCK)),
                  full((N_CHUNKS * 16, 1)), full((1, GROUP_W))],
        out_specs=pl.BlockSpec((1, SEQ, GROUP_W), lambda i: (i, 0, 0)),
        out_shape=jax.ShapeDtypeStruct((b, SEQ, GROUP_W), F32),
        scratch_shapes=[pltpu.VMEM((SEQ, CHUNK), F32), pltpu.VMEM((N_CHUNKS, 16, CHUNK), F32),
                        pltpu.VMEM((N_CHUNKS, 16, CHUNK), F32), pltpu.VMEM((N_CHUNKS, GROUP_W, CHUNK), BF16),
                        pltpu.VMEM((N_CHUNKS, GROUP_W, CHUNK), BF16), pltpu.VMEM((N_CHUNKS, GROUP_W, CHUNK), F32),
                        pltpu.VMEM((N_CHUNKS, GROUP_W, CHUNK), F32),
                        pltpu.VMEM((2 * N_HEADS, ML_STATE_ROWS, HEAD_DIM), F32),
                        pltpu.VMEM((2 * N_HEADS, 8, CHUNK), F32)],
        compiler_params=_params("arbitrary"),
        name="mlstm_mixer",
    )(qkv, og, sm, bcol, norm_g.reshape(1, GROUP_W))


HALF = SEQ // 2


def _dft_tables():
    f = np.arange(HALF, dtype=np.int64)[:, None]
    t = np.arange(SEQ, dtype=np.int64)[None, :]
    ang = (2.0 * np.pi / FFT_N) * ((f * t) % FFT_N).astype(np.float64)
    wc = np.cos(ang)
    ws = np.sin(ang)
    ws[0, :] = np.array([1.0, 1.0, -1.0, -1.0])[np.arange(SEQ) % 4]
    tables = []
    for parity in range(2):
        w = np.stack([wc[:, parity::2], ws[:, parity::2]]).reshape(2, N_FREQ_BLOCKS, FREQ_BLOCK, HALF)
        w = np.ascontiguousarray(w.transpose(1, 0, 2, 3).reshape(2 * HALF, HALF)).astype(np.float32)
        tables.append((jnp.asarray(w).astype(BF16), jnp.asarray(np.ascontiguousarray(w.T)).astype(BF16)))
    return tables


def _hy_features():
    t = jnp.linspace(0.0, 1.0, SEQ, dtype=F32)[:, None]
    bands = (HY_EMB - 1) // 2
    f = jnp.linspace(1e-4, bands - 1, bands, dtype=F32)
    ang = (2.0 * math.pi) * (jnp.arange(SEQ, dtype=F32) / SEQ)[:, None] * f[None, :]
    feats = jnp.concatenate([t, jnp.cos(ang), -jnp.sin(ang)], axis=-1)
    deltas = jnp.abs(jnp.linspace(math.log(1e-2) / 1.5, math.log(1e-2) / 0.3, GROUP_W, dtype=F32))
    return jnp.pad(feats, ((0, 0), (0, CHUNK - HY_EMB))), t, deltas.reshape(1, GROUP_W)


def _hy_filter_kernel(feats_ref, w1_ref, b1_ref, w2_ref, b2_ref, w3_ref, freq_ref, t_ref, dl_ref, hp_ref, hm_ref):
    h = jnp.sin(freq_ref[0:1, :] * (_dot(feats_ref[...].astype(BF16), w1_ref[...].astype(BF16)) + b1_ref[...]))
    h = jnp.sin(freq_ref[1:2, :] * (_dot(h.astype(BF16), w2_ref[...].astype(BF16)) + b2_ref[...]))
    h = _dot(h.astype(BF16), w3_ref[...].astype(BF16))
    win = jnp.exp(-t_ref[...] * dl_ref[...])
    row = lax.broadcasted_iota(jnp.int32, (SEQ, 1), 0)
    for o in range(2):
        hf = h[:, (2 * o) * GROUP_W:(2 * o + 1) * GROUP_W] * win
        hb = h[:, (2 * o + 1) * GROUP_W:(2 * o + 2) * GROUP_W] * win
        r = lax.rsqrt(jnp.sum(hf * hf + hb * hb, axis=0, keepdims=True) + NORM_EPS)
        hf = hf * r
        hb = jnp.where(row == 0, 0.0, hb * r)
        hp_ref[:, o * GROUP_W:(o + 1) * GROUP_W] = (hf + hb).astype(BF16)
        hm_ref[:, o * GROUP_W:(o + 1) * GROUP_W] = (hf - hb).astype(BF16)


def _hy_spec_kernel(we_ref, wo_ref, hp_ref, hm_ref, kc_ref, ks_ref, kcp_ref, ksp_ref):
    fb = FREQ_BLOCK
    nc = 2 * GROUP_W
    row0 = (lax.broadcasted_iota(jnp.int32, (fb, 1), 0) + pl.program_id(0) * fb) == 0
    edge = jnp.where(row0, 1.0 / FFT_N, 2.0 / FFT_N)
    pe = _dot(we_ref[0:fb, :], hp_ref[:, 0:nc])
    po = _dot(wo_ref[0:fb, :], hp_ref[:, nc:])
    me = _dot(we_ref[fb:, :], hm_ref[:, 0:nc])
    mo = _dot(wo_ref[fb:, :], hm_ref[:, nc:])
    mid_cos = _dot(we_ref[fb:fb + 8, :], hp_ref[:, 0:nc])[0:1, :]
    kc_ref[...] = (pe + po) * edge
    kcp_ref[...] = (pe - po) * edge
    ks_ref[...] = jnp.where(row0, mid_cos, me + mo) * (2.0 / FFT_N)
    ksp_ref[...] = jnp.where(row0, mo, mo - me) * (2.0 / FFT_N)


def _hy_spectrum(tables, w1, b1, w2, b2, w3, freq):
    feats, t, deltas = _hy_features()
    w1p = jnp.pad(w1, ((0, CHUNK - HY_EMB), (0, 0)))
    hp, hm = pl.pallas_call(
        _hy_filter_kernel,
        out_shape=[jax.ShapeDtypeStruct((SEQ, 2 * GROUP_W), BF16)] * 2,
        compiler_params=pltpu.CompilerParams(vmem_limit_bytes=VMEM_LIMIT),
        name="hyena_filter",
    )(feats, w1p, b1.reshape(1, HY_FFN), w2, b2.reshape(1, HY_FFN), w3, freq, t, deltas)
    split = lambda h: h.reshape(HALF, 4 * GROUP_W)
    table_spec = pl.BlockSpec((2 * FREQ_BLOCK, HALF), lambda j: (j, 0))
    return pl.pallas_call(
        _hy_spec_kernel,
        grid=(N_FREQ_BLOCKS,),
        in_specs=[table_spec, table_spec,
                  pl.BlockSpec((HALF, 4 * GROUP_W), lambda j: (0, 0)),
                  pl.BlockSpec((HALF, 4 * GROUP_W), lambda j: (0, 0))],
        out_specs=[pl.BlockSpec((FREQ_BLOCK, 2 * GROUP_W), lambda j: (j, 0))] * 4,
        out_shape=[jax.ShapeDtypeStruct((HALF, 2 * GROUP_W), F32)] * 4,
        compiler_params=_params("arbitrary"),
        name="hyena_spectrum",
    )(tables[0][0], tables[1][0], split(hp), split(hm))


def _split_conv_rows(e_ref, o_ref, b, r0, w, bias):
    e = e_ref[b, pl.ds(r0, CHUNK), :]
    o = o_ref[b, pl.ds(r0, CHUNK), :]
    o_prev = o_ref[b, pl.ds(jnp.maximum(r0 - 1, 0), 1), :]
    e_next = e_ref[b, pl.ds(jnp.minimum(r0 + CHUNK, HALF - 1), 1), :]
    o_prev = jnp.where(r0 > 0, o_prev, 0.0)
    e_next = jnp.where(r0 + CHUNK < HALF, e_next, 0.0)
    row = lax.broadcasted_iota(jnp.int32, (CHUNK, 1), 0)
    o_dn = jnp.where(row == 0, o_prev, pltpu.roll(o, 1, 0))
    e_up = jnp.where(row == CHUNK - 1, e_next, pltpu.roll(e, CHUNK - 1, 0))
    conv_e = w[0:1, :] * o_dn + w[1:2, :] * e + w[2:3, :] * o + bias
    conv_o = w[0:1, :] * e + w[1:2, :] * o + w[2:3, :] * e_up + bias
    return conv_e, conv_o


def _hy_conv_kernel(ze_ref, zo_ref, ge_ref, go_ref, cwz_ref, cbz_ref, cwg_ref, cbg_ref, we_ref, wo_ref, wet_ref, wot_ref,
                    kc_ref, ks_ref, kcp_ref, ksp_ref, skip_ref, g_ref, o_ref, ze_s, zo_s, acce_s, acco_s, *,
                    conv_z, final_norm):
    j = pl.program_id(1)
    n_b = ze_ref.shape[0]
    fb = FREQ_BLOCK
    half_chunks = HALF // CHUNK

    def z_rows(b, r0):
        if conv_z:
            return _split_conv_rows(ze_ref, zo_ref, b, r0, cwz_ref[...], cbz_ref[...])
        return ze_ref[b, pl.ds(r0, CHUNK), :], zo_ref[b, pl.ds(r0, CHUNK), :]

    @pl.when(j == 0)
    def _():
        for b in range(n_b):
            def fill(c, _, b=b):
                r0 = pl.multiple_of(c * CHUNK, CHUNK)
                z_e, z_o = z_rows(b, r0)
                ze_s[pl.ds(r0, CHUNK), b * GROUP_W:(b + 1) * GROUP_W] = z_e.astype(BF16)
                zo_s[pl.ds(r0, CHUNK), b * GROUP_W:(b + 1) * GROUP_W] = z_o.astype(BF16)
                return 0

            lax.fori_loop(0, half_chunks, fill, 0)
        acce_s[...] = jnp.zeros_like(acce_s)
        acco_s[...] = jnp.zeros_like(acco_s)

    row0 = (lax.broadcasted_iota(jnp.int32, (fb, 1), 0) + j * fb) == 0
    tiled = lambda ref: jnp.concatenate([ref[...]] * n_b, axis=1)
    kc, ks, kcp, ksp = tiled(kc_ref), tiled(ks_ref), tiled(kcp_ref), tiled(ksp_ref)
    xe = _dot(we_ref[...], ze_s[...])
    xo = _dot(wo_ref[...], zo_s[...])
    ae, be, ao, bo = xe[0:fb, :], xe[fb:, :], xo[0:fb, :], xo[fb:, :]
    xc, xcp, xs, xsp = ae + ao, ae - ao, be + bo, bo - be
    yc = xc * kc - xs * ks
    ys = xc * ks + xs * kc
    ycp = xcp * kcp - xsp * ksp
    ysp = xcp * ksp + xsp * kcp
    dc, nyq = xc * kc, xcp * kcp
    pc = jnp.where(row0, dc + nyq, yc + ycp)
    mc = jnp.where(row0, dc - nyq, yc - ycp)
    ms = jnp.where(row0, be * ks - bo * ksp, ys - ysp)
    ps = jnp.where(row0, be * ksp + bo * ks, ys + ysp)
    acce_s[...] += _dot(wet_ref[...], jnp.concatenate([pc, ms], axis=0).astype(BF16))
    acco_s[...] += _dot(wot_ref[...], jnp.concatenate([mc, ps], axis=0).astype(BF16))

    @pl.when(j == pl.num_programs(1) - 1)
    def _():
        for b in range(n_b):
            def finish(c, _, b=b):
                r0 = pl.multiple_of(c * CHUNK, CHUNK)
                gates = _split_conv_rows(ge_ref, go_ref, b, r0, cwg_ref[...], cbg_ref[...])
                for parity, (gate, z, acc) in enumerate(zip(gates, z_rows(b, r0), (acce_s, acco_s))):
                    r = gate * (acc[pl.ds(r0, CHUNK), b * GROUP_W:(b + 1) * GROUP_W] + skip_ref[...] * z)
                    if final_norm:
                        r = _rms(r) * g_ref[...]
                    o_ref[b, pl.ds(r0, CHUNK), parity * GROUP_W:(parity + 1) * GROUP_W] = r
                return 0

            lax.fori_loop(0, half_chunks, finish, 0)


def _hy_long_conv(z, z_cols, gates, gate_cols, conv_w, conv_b, conv_z, tables, spectrum, order, skip, norm_g,
                  final_norm):
    b = z.shape[0]
    bg = HY_BATCH_BLOCK
    taps = lambda col: (conv_w[:, col * GROUP_W:(col + 1) * GROUP_W],
                        conv_b[col * GROUP_W:(col + 1) * GROUP_W].reshape(1, GROUP_W))
    small = lambda shape: pl.BlockSpec(shape, lambda g, j: (0, 0))
    act = lambda col: pl.BlockSpec((bg, HALF, GROUP_W), lambda g, j: (g, 0, col))
    fwd = pl.BlockSpec((2 * FREQ_BLOCK, HALF), lambda g, j: (j, 0))
    inv = pl.BlockSpec((HALF, 2 * FREQ_BLOCK), lambda g, j: (0, j))
    spec = pl.BlockSpec((FREQ_BLOCK, GROUP_W), lambda g, j: (j, order))
    (we, wet), (wo, wot) = tables
    return pl.pallas_call(
        functools.partial(_hy_conv_kernel, conv_z=conv_z, final_norm=final_norm),
        grid=(b // bg, N_FREQ_BLOCKS),
        in_specs=[act(z_cols[0]), act(z_cols[1]), act(gate_cols[0]), act(gate_cols[1]),
                  small((3, GROUP_W)), small((1, GROUP_W)), small((3, GROUP_W)), small((1, GROUP_W)),
                  fwd, fwd, inv, inv, spec, spec, spec, spec, small((1, GROUP_W)), small((1, GROUP_W))],
        out_specs=pl.BlockSpec((bg, HALF, 2 * GROUP_W), lambda g, j: (g, 0, 0)),
        out_shape=jax.ShapeDtypeStruct((b, HALF, 2 * GROUP_W), F32),
        scratch_shapes=[pltpu.VMEM((HALF, bg * GROUP_W), BF16), pltpu.VMEM((HALF, bg * GROUP_W), BF16),
                        pltpu.VMEM((HALF, bg * GROUP_W), F32), pltpu.VMEM((HALF, bg * GROUP_W), F32)],
        compiler_params=_params("arbitrary", "arbitrary"),
        name="hyena_long_conv",
    )(z, z, gates, gates, *taps(0), *taps(order + 1), we, wo, wet, wot, *spectrum,
      skip.reshape(1, GROUP_W), norm_g.reshape(1, GROUP_W))


def _hy_mixer(u, tables, conv_w, conv_b, w1, b1, w2, b2, w3, freq, skip, norm_g):
    b = u.shape[0]
    spectrum = _hy_spectrum(tables, w1, b1, w2, b2, w3, freq)
    u = u.reshape(b, HALF, 6 * GROUP_W)
    z1 = _hy_long_conv(u, (0, 3), u, (1, 4), conv_w, conv_b, True, tables, spectrum, 0, skip[0], norm_g, False)
    y = _hy_long_conv(z1, (0, 1), u, (2, 5), conv_w, conv_b, False, tables, spectrum, 1, skip[1], norm_g, True)
    return y.reshape(b, SEQ, GROUP_W)


FFN_SPLITS = ((0, 1024), (1024, 2048), (2048, FFN_HIDDEN))


def _ffn_kernel(x_ref, mod_ref, ya_ref, yb_ref, yc_ref, yd_ref, wo_ref, wg_ref, wu_ref, wd_ref, gf_ref, o_ref, *,
                final_norm):
    y_mix = jnp.concatenate([ya_ref[...], yb_ref[...], yc_ref[...], yd_ref[...]], axis=1).astype(BF16)
    x = x_ref[...] + mod_ref[0, 2:3, :] * _dot(y_mix, wo_ref[...])
    h = (_rms(x) * (1.0 + mod_ref[0, 4:5, :]) + mod_ref[0, 3:4, :]).astype(BF16)
    acc = jnp.zeros_like(x)
    for lo, hi in FFN_SPLITS:
        a = _silu(_dot(h, wg_ref[:, lo:hi])) * _dot(h, wu_ref[:, lo:hi])
        acc = acc + _dot(a.astype(BF16), wd_ref[lo:hi, :])
    y = x + mod_ref[0, 5:6, :] * acc
    if final_norm:
        y = _rms(y) * gf_ref[...]
    o_ref[...] = y


def _out_ffn(x2, mod_l, ys, w_out, wg, wu, wd, final_g, final_norm):
    n_tok, d = x2.shape
    tm = TOKEN_TILE
    tiles_per_seq = SEQ // tm
    resident = lambda shape: pl.BlockSpec(shape, lambda i: (0, 0), pipeline_mode=pl.Buffered(1))
    return pl.pallas_call(
        functools.partial(_ffn_kernel, final_norm=final_norm),
        grid=(n_tok // tm,),
        in_specs=[pl.BlockSpec((tm, d), lambda i: (i, 0)),
                  pl.BlockSpec((1, 6, d), lambda i: (i // tiles_per_seq, 0, 0))]
                 + [pl.BlockSpec((tm, GROUP_W), lambda i: (i, 0))] * 4
                 + [resident((d, d)), resident((d, FFN_HIDDEN)), resident((d, FFN_HIDDEN)), resident((FFN_HIDDEN, d)),
                    pl.BlockSpec((1, d), lambda i: (0, 0))],
        out_specs=pl.BlockSpec((tm, d), lambda i: (i, 0)),
        out_shape=jax.ShapeDtypeStruct((n_tok, d), F32),
        compiler_params=_params("arbitrary"),
        name="out_proj_swiglu",
    )(x2, mod_l, *ys, w_out, wg, wu, wd, final_g.reshape(1, d))


def _permute_w_in(w):
    cols = [w[:, 0:1024], w[:, 2568:2824], w[:, 1024:1032], w[:, 2824:2840],
            jnp.zeros((w.shape[0], P_PAD - 3608), w.dtype), w[:, 1032:1800], w[:, 1800:2568], w[:, 2840:3608]]
    return jnp.concatenate(cols, axis=1).astype(BF16)


def kernel(x, c, mod_w, mod_b, w_in, ssd_conv_w, ssd_conv_b, ssd_dt_bias, ssd_a_log, ssd_d, ssd_norm_g, na_rpb, na_norm_g, ml_i_bias, ml_f_bias, ml_norm_g, hy_conv_w, hy_conv_b, hy_w1, hy_b1, hy_w2, hy_b2, hy_w3, hy_freq, hy_skip, hy_norm_g, w_out, ffn_w_gate, ffn_w_up, ffn_w_down, final_norm_g):
    b, seq, d = x.shape
    assert seq == SEQ and d == D_MODEL and b % HY_BATCH_BLOCK == 0
    depth = mod_w.shape[0]
    mod = _modulation(c, mod_w, mod_b).reshape(depth, b, 6, d)
    dft = _dft_tables()
    x2 = x.reshape(b * seq, d)
    for l in range(depth):
        wide, qkv, hy = _in_proj(x2, mod[l], _permute_w_in(w_in[l]))
        wide = wide.reshape(b, seq, WIDE_W)
        qkv = qkv.reshape(b, seq, QKV_W)
        y_ssd = _ssd_mixer(wide, wide, ssd_conv_w[l], ssd_conv_b[l], ssd_dt_bias[l], ssd_a_log[l], ssd_d[l],
                           ssd_norm_g[l])
        y_na = _na_mixer(qkv, na_rpb[l], na_norm_g[l])
        y_ml = _ml_mixer(qkv, wide, wide, ml_i_bias[l], ml_f_bias[l], ml_norm_g[l])
        y_hy = _hy_mixer(hy.reshape(b, seq, HY_W), dft, hy_conv_w[l], hy_conv_b[l], hy_w1[l], hy_b1[l], hy_w2[l],
                         hy_b2[l], hy_w3[l], hy_freq[l], hy_skip[l], hy_norm_g[l])
        ys = [t.reshape(b * seq, GROUP_W) for t in (y_ssd, y_na, y_ml, y_hy)]
        x2 = _out_ffn(x2, mod[l], ys, w_out[l].astype(BF16), ffn_w_gate[l].astype(BF16), ffn_w_up[l].astype(BF16),
                      ffn_w_down[l].astype(BF16), final_norm_g, l == depth - 1)
    return x2.reshape(b, seq, d)
```

```python
import functools
import math

import numpy as np
import jax
import jax.numpy as jnp
from jax import lax
from jax.experimental import pallas as pl
from jax.experimental.pallas import tpu as pltpu

F32 = jnp.float32
BF16 = jnp.bfloat16

D_MODEL = 1024
SEQ = 2048
GRID_W = 64
GROUP_W = 256
HEAD_DIM = 64
N_HEADS = 4
SSD_STATE = 128
SSD_XBC = 768
CHUNK = 128
N_CHUNKS = SEQ // CHUNK
NA_KH = 8
NA_KW = 16
NA_ROW_GROUP = 4
HY_EMB = 33
HY_FFN = 64
FFN_HIDDEN = 2816
NORM_EPS = 1e-6
FFT_N = 2 * SEQ
FREQ_BLOCK = 256
N_FREQ_BLOCKS = SEQ // 2 // FREQ_BLOCK
HY_BATCH_BLOCK = 2
TOKEN_TILE = 512
VMEM_LIMIT = 56 * 1024 * 1024

WIDE_W = 1408
WIDE_ZX_BLOCK = 0
WIDE_MLO_BLOCK = 4
WIDE_SM_BLOCK = 10
QKV_W = 1536
QKV_NA_BLOCK = 0
QKV_ML_BLOCK = 1
HY_W = 768
P_PAD = WIDE_W + QKV_W + HY_W
SM_DT = 0
SM_IG = 8
SM_FG = 16

NT_DIMS = (((1,), (1,)), ((), ()))
TN_DIMS = (((0,), (0,)), ((), ()))


def _rms(x):
    return x * lax.rsqrt(jnp.mean(x * x, axis=-1, keepdims=True) + NORM_EPS)


def _silu(x):
    return x * jax.nn.sigmoid(x)


def _dot(a, b):
    return jnp.dot(a, b, preferred_element_type=F32)


def _params(*sem):
    return pltpu.CompilerParams(dimension_semantics=sem, vmem_limit_bytes=VMEM_LIMIT)


def _mod_kernel(c_ref, w_ref, b_ref, o_ref):
    cond = _silu(c_ref[...]).astype(BF16)
    o_ref[0] = _dot(cond, w_ref[0].astype(BF16)) + b_ref[0]


def _modulation(c, mod_w, mod_b):
    depth, d, n = mod_w.shape
    b = c.shape[0]
    nb = n // d
    return pl.pallas_call(
        _mod_kernel,
        grid=(depth, nb),
        in_specs=[pl.BlockSpec((b, d), lambda l, j: (0, 0)),
                  pl.BlockSpec((1, d, d), lambda l, j: (l, 0, j)),
                  pl.BlockSpec((1, 1, d), lambda l, j: (l, 0, j))],
        out_specs=pl.BlockSpec((1, b, d), lambda l, j: (l, 0, j)),
        out_shape=jax.ShapeDtypeStruct((depth, b, n), F32),
        compiler_params=_params("arbitrary", "arbitrary"),
        name="adaln_mod",
    )(c, mod_w, mod_b.reshape(depth, 1, n))


def _in_proj_kernel(x_ref, mod_ref, w_ref, wide_ref, qkv_ref, hy_ref, cols_s):
    h = (_rms(x_ref[...]) * (1.0 + mod_ref[0, 1:2, :]) + mod_ref[0, 0:1, :]).astype(BF16)
    wide_ref[...] = _dot(h, w_ref[:, 0:WIDE_W])
    qkv_ref[...] = _dot(h, w_ref[:, WIDE_W:WIDE_W + QKV_W]).astype(BF16)
    half = x_ref.shape[0] // 2
    n_slabs = cols_s.shape[0]
    for cb in range(n_slabs):
        cols_s[cb] = h[:, cb * CHUNK:(cb + 1) * CHUNK].astype(F32)
    w_hy = w_ref[:, WIDE_W + QKV_W:P_PAD]
    for parity in range(2):
        rows = jnp.concatenate([cols_s[cb, pl.ds(parity, half, stride=2), :] for cb in range(n_slabs)], axis=1)
        hy_ref[:, parity * HY_W:(parity + 1) * HY_W] = _dot(rows.astype(BF16), w_hy)


def _in_proj(x2, mod_l, w_perm):
    n_tok, d = x2.shape
    tm = TOKEN_TILE
    tiles_per_seq = SEQ // tm
    return pl.pallas_call(
        _in_proj_kernel,
        grid=(n_tok // tm,),
        in_specs=[pl.BlockSpec((tm, d), lambda i: (i, 0)),
                  pl.BlockSpec((1, 6, d), lambda i: (i // tiles_per_seq, 0, 0)),
                  pl.BlockSpec((d, P_PAD), lambda i: (0, 0), pipeline_mode=pl.Buffered(1))],
        out_specs=[pl.BlockSpec((tm, WIDE_W), lambda i: (i, 0)), pl.BlockSpec((tm, QKV_W), lambda i: (i, 0)),
                   pl.BlockSpec((tm // 2, 2 * HY_W), lambda i: (i, 0))],
        out_shape=[jax.ShapeDtypeStruct((n_tok, WIDE_W), F32), jax.ShapeDtypeStruct((n_tok, QKV_W), BF16),
                   jax.ShapeDtypeStruct((n_tok // 2, 2 * HY_W), F32)],
        scratch_shapes=[pltpu.VMEM((d // CHUNK, tm, CHUNK), F32)],
        compiler_params=_params("arbitrary"),
        name="in_proj",
    )(x2, mod_l, w_perm)


def _dwconv3_rows(src_ref, b, r0, cols, w, bias):
    x = src_ref[b, pl.ds(r0, CHUNK), cols]
    up = src_ref[b, pl.ds(jnp.maximum(r0 - 1, 0), 1), cols]
    dn = src_ref[b, pl.ds(jnp.minimum(r0 + CHUNK, SEQ - 1), 1), cols]
    up = jnp.where(r0 > 0, up, 0.0)
    dn = jnp.where(r0 + CHUNK < SEQ, dn, 0.0)
    row = lax.broadcasted_iota(jnp.int32, (CHUNK, 1), 0)
    prev = jnp.where(row == 0, up, pltpu.roll(x, 1, 0))
    nxt = jnp.where(row == CHUNK - 1, dn, pltpu.roll(x, CHUNK - 1, 0))
    return w[0:1, :] * prev + w[1:2, :] * x + w[2:3, :] * nxt + bias


def _chunk_scans(a, axis):
    n = a.shape[axis]
    shape = [1, 1]
    shape[axis] = n
    pos = lax.broadcasted_iota(jnp.int32, tuple(shape), axis) % CHUNK
    pre, suf = a, a
    k = 1
    while k < CHUNK:
        pre = pre + jnp.where(pos >= k, pltpu.roll(pre, k, axis), 0.0)
        suf = suf + jnp.where(pos < CHUNK - k, pltpu.roll(suf, n - k, axis), 0.0)
        k *= 2
    return pre, suf


def _gate_rows(sm_ref, rows_s, lo, n):
    for c in range(N_CHUNKS):
        t = sm_ref[0, c * CHUNK:(c + 1) * CHUNK, :].T
        rows_s[c] = t[lo:lo + n, :]


def _ssd_kernel(zx_ref, sm_ref, cw_ref, cb_ref, dtb_col_ref, a_col_ref, dsk_ref, g_ref,
                o_ref, xs_s, b_s, xst_s, ct_s, acol_s, rows_s, dtrow_s, arow_s, ytf_s, ytb_s, st_s):
    cw = cw_ref[...]
    cb = cb_ref[...]

    def conv_body(c, _):
        r0 = pl.multiple_of(c * CHUNK, CHUNK)
        v = _silu(_dwconv3_rows(zx_ref, 0, r0, slice(GROUP_W, GROUP_W + SSD_XBC), cw, cb))
        xs_s[pl.ds(r0, CHUNK), :] = v[:, 0:GROUP_W]
        b_s[pl.ds(r0, CHUNK), :] = v[:, GROUP_W:2 * GROUP_W].astype(BF16)
        xst_s[c] = v[:, 0:GROUP_W].T
        ct_s[c] = v[:, 2 * GROUP_W:3 * GROUP_W].T.astype(BF16)
        return 0

    lax.fori_loop(0, N_CHUNKS, conv_body, 0)

    _gate_rows(sm_ref, rows_s, SM_DT, 8)
    dt_row = jax.nn.softplus(rows_s[...].reshape(N_CHUNKS * 8, CHUNK) + dtb_col_ref[...])
    pre, suf = _chunk_scans(dt_row * (-jnp.exp(a_col_ref[...])), 1)
    rid = lax.broadcasted_iota(jnp.int32, (N_CHUNKS * 8, 1), 0) % 8
    dtrow_s[...] = dt_row.reshape(N_CHUNKS, 8, CHUNK)
    arow_s[...] = jnp.where(rid < N_HEADS, pre, suf).reshape(N_CHUNKS, 8, CHUNK)
    pad_rows = jnp.zeros((CHUNK - 8, CHUNK), F32)
    for c in range(N_CHUNKS):
        acol_s[c * CHUNK:(c + 1) * CHUNK, :] = jnp.concatenate([arow_s[c], pad_rows], axis=0).T

    si = lax.broadcasted_iota(jnp.int32, (CHUNK, CHUNK), 0)
    ti = lax.broadcasted_iota(jnp.int32, (CHUNK, CHUNK), 1)
    first = lax.broadcasted_iota(jnp.int32, (2 * HEAD_DIM, 1), 0) < HEAD_DIM
    st_s[...] = jnp.zeros_like(st_s)

    def chunk_body(i, _):
        units = [(d, g) for d in range(2) for g in range(2)]
        chunk = {0: i, 1: N_CHUNKS - 1 - i}
        rows = {d: pl.multiple_of(chunk[d] * CHUNK, CHUNK) for d in range(2)}
        bgs, cgs, prevs, scores, y_offs = {}, {}, {}, {}, {}
        for d, g in units:
            bgs[d, g] = b_s[pl.ds(rows[d], CHUNK), g * SSD_STATE:(g + 1) * SSD_STATE]
            cgs[d, g] = ct_s[chunk[d], g * SSD_STATE:(g + 1) * SSD_STATE, :]
            prevs[d, g] = st_s[d, g]
            scores[d, g] = _dot(bgs[d, g], cgs[d, g])
            y_offs[d, g] = _dot(prevs[d, g].astype(BF16), cgs[d, g])
        weighted, xdts, x_ends, carries, grow = {}, {}, {}, {}, {}
        for d, g in units:
            mask = (si <= ti) if d == 0 else (si >= ti)
            end = CHUNK - 1 if d == 0 else 0
            acol = acol_s[pl.ds(rows[d], CHUNK), :]
            arow = arow_s[chunk[d]]
            dtr = dtrow_s[chunk[d]]
            for hh in range(2):
                h = 2 * g + hh
                j = d * N_HEADS + h
                ar = arow[j:j + 1, :]
                a_end = ar[:, end:end + 1]
                decay = jnp.exp(jnp.where(mask, ar - acol[:, j:j + 1], -jnp.inf))
                xdt_t = xst_s[chunk[d], h * HEAD_DIM:(h + 1) * HEAD_DIM, :] * dtr[j:j + 1, :]
                weighted[d, h] = (scores[d, g] * decay).astype(BF16)
                xdts[d, h] = xdt_t.astype(BF16)
                x_ends[d, h] = (xdt_t * jnp.exp(a_end - ar)).astype(BF16)
                carries[d, h] = jnp.exp(a_end)
                grow[d, h] = jnp.exp(ar)
        for d in range(2):
            ys = []
            for h in range(N_HEADS):
                g, hh = divmod(h, 2)
                y_off = y_offs[d, g][hh * HEAD_DIM:(hh + 1) * HEAD_DIM, :]
                ys.append(_dot(xdts[d, h], weighted[d, h]) + y_off * grow[d, h])
            y_t = jnp.concatenate(ys, axis=0)
            if d == 0:
                ytf_s[chunk[d]] = y_t
            else:
                ytb_s[chunk[d]] = y_t
        for d, g in units:
            keep = jnp.where(first, carries[d, 2 * g], carries[d, 2 * g + 1])
            x_end = jnp.concatenate([x_ends[d, 2 * g], x_ends[d, 2 * g + 1]], axis=0)
            st_s[d, g] = prevs[d, g] * keep + _dot(x_end, bgs[d, g])
        return 0

    lax.fori_loop(0, N_CHUNKS, chunk_body, 0, unroll=2)

    dsk = dsk_ref[...]
    gain = g_ref[...]

    def out_body(c, _):
        r0 = pl.multiple_of(c * CHUNK, CHUNK)
        y = (ytf_s[c] + ytb_s[c]).T + dsk * xs_s[pl.ds(r0, CHUNK), :]
        y = y * _silu(zx_ref[0, pl.ds(r0, CHUNK), 0:GROUP_W])
        o_ref[0, pl.ds(r0, CHUNK), :] = _rms(y) * gain
        return 0

    lax.fori_loop(0, N_CHUNKS, out_body, 0)


def _ssd_mixer(zx, sm, conv_w, conv_b, dt_bias, a_log, d_skip, norm_g):
    b = zx.shape[0]
    tile_col = lambda v: jnp.tile(v.reshape(8, 1), (N_CHUNKS, 1))
    full = lambda shape: pl.BlockSpec(shape, lambda i: (0,) * len(shape))
    return pl.pallas_call(
        _ssd_kernel,
        grid=(b,),
        in_specs=[pl.BlockSpec((1, SEQ, 1024), lambda i: (i, 0, WIDE_ZX_BLOCK)),
                  pl.BlockSpec((1, SEQ, CHUNK), lambda i: (i, 0, WIDE_SM_BLOCK)),
                  full((3, SSD_XBC)), full((1, SSD_XBC)),
                  full((N_CHUNKS * 8, 1)), full((N_CHUNKS * 8, 1)), full((1, GROUP_W)), full((1, GROUP_W))],
        out_specs=pl.BlockSpec((1, SEQ, GROUP_W), lambda i: (i, 0, 0)),
        out_shape=jax.ShapeDtypeStruct((b, SEQ, GROUP_W), F32),
        scratch_shapes=[pltpu.VMEM((SEQ, GROUP_W), F32), pltpu.VMEM((SEQ, GROUP_W), BF16),
                        pltpu.VMEM((N_CHUNKS, GROUP_W, CHUNK), F32), pltpu.VMEM((N_CHUNKS, GROUP_W, CHUNK), BF16),
                        pltpu.VMEM((SEQ, CHUNK), F32), pltpu.VMEM((N_CHUNKS, 8, CHUNK), F32),
                        pltpu.VMEM((N_CHUNKS, 8, CHUNK), F32), pltpu.VMEM((N_CHUNKS, 8, CHUNK), F32),
                        pltpu.VMEM((N_CHUNKS, GROUP_W, CHUNK), F32), pltpu.VMEM((N_CHUNKS, GROUP_W, CHUNK), F32),
                        pltpu.VMEM((2, 2, 2 * HEAD_DIM, SSD_STATE), F32)],
        compiler_params=_params("arbitrary"),
        name="ssd_mixer",
    )(zx, sm, conv_w, conv_b.reshape(1, SSD_XBC), tile_col(dt_bias), tile_col(a_log),
      jnp.repeat(d_skip, HEAD_DIM).reshape(1, GROUP_W), norm_g.reshape(1, GROUP_W))


def _na_kernel(qkv_ref, bias_ref, g_ref, o_ref):
    n_rows = SEQ // GRID_W
    k_tok = NA_KH * GRID_W
    pair_w = 2 * HEAD_DIM
    first = lax.broadcasted_iota(jnp.int32, (1, pair_w), 1) < HEAD_DIM
    gain = g_ref[...]

    def row_group(g, _):
        units = [(i, pair) for i in range(NA_ROW_GROUP) for pair in range(N_HEADS // 2)]
        q0, k0, dr0, scores, values = {}, {}, {}, {}, {}
        for i in range(NA_ROW_GROUP):
            r = g * NA_ROW_GROUP + i
            start = jnp.clip(r - NA_KH // 2, 0, n_rows - NA_KH)
            dr0[i] = start - r + NA_KH - 1
            q0[i] = pl.multiple_of(r * GRID_W, GRID_W)
            k0[i] = pl.multiple_of(start * GRID_W, GRID_W)
        for i, pair in units:
            lo = pair * pair_w
            q2 = qkv_ref[0, pl.ds(q0[i], GRID_W), lo:lo + pair_w] * (HEAD_DIM ** -0.5)
            k2 = qkv_ref[0, pl.ds(k0[i], k_tok), GROUP_W + lo:GROUP_W + lo + pair_w]
            values[i, pair] = qkv_ref[0, pl.ds(k0[i], k_tok), 2 * GROUP_W + lo:2 * GROUP_W + lo + pair_w]
            qm = jnp.concatenate([jnp.where(first, q2, 0.0), jnp.where(first, 0.0, q2)], axis=0).astype(BF16)
            scores[i, pair] = lax.dot_general(qm, k2, NT_DIMS, preferred_element_type=F32)
        probs, sums = {}, {}
        for i, pair in units:
            bias = jnp.concatenate(
                [jnp.concatenate([bias_ref[2 * pair + hh, dr0[i] + 2 * m] for m in range(NA_KH // 2)], axis=1)
                 for hh in range(2)], axis=0)
            s = scores[i, pair] + bias
            p = jnp.exp(s - jnp.max(s, axis=-1, keepdims=True))
            probs[i, pair] = p.astype(BF16)
            sums[i, pair] = jnp.sum(p, axis=-1, keepdims=True)
        for i in range(NA_ROW_GROUP):
            outs = []
            for pair in range(N_HEADS // 2):
                o2 = _dot(probs[i, pair], values[i, pair]) / sums[i, pair]
                outs.append(jnp.where(first, o2[0:GRID_W, :], o2[GRID_W:, :]))
            o_ref[0, pl.ds(q0[i], GRID_W), :] = _rms(jnp.concatenate(outs, axis=1)) * gain
        return 0

    lax.fori_loop(0, n_rows // NA_ROW_GROUP, row_group, 0, unroll=2)


def _na_bias_tiles(rpb):
    col = np.arange(GRID_W)
    cs = np.clip(col - NA_KW // 2, 0, GRID_W - NA_KW)
    valid_c = (col[None, :] >= cs[:, None]) & (col[None, :] < cs[:, None] + NA_KW)
    dc = col[None, :] - col[:, None] + NA_KW - 1
    sel_c = (valid_c[None] & (dc[None] == np.arange(2 * NA_KW - 1)[:, None, None])).astype(np.float32)
    by_row = jnp.einsum("hab,bqk->haqk", rpb, sel_c, precision=lax.Precision.HIGHEST)
    by_row = jnp.where(valid_c[None, None], by_row, -1e30)
    return jnp.concatenate([by_row[:, :-1], by_row[:, 1:]], axis=-1).astype(F32)


def _na_mixer(qkv, rpb, norm_g):
    b = qkv.shape[0]
    table = _na_bias_tiles(rpb)
    return pl.pallas_call(
        _na_kernel,
        grid=(b,),
        in_specs=[pl.BlockSpec((1, SEQ, 3 * GROUP_W), lambda i: (i, 0, QKV_NA_BLOCK)),
                  pl.BlockSpec(table.shape, lambda i: (0, 0, 0, 0)),
                  pl.BlockSpec((1, GROUP_W), lambda i: (0, 0))],
        out_specs=pl.BlockSpec((1, SEQ, GROUP_W), lambda i: (i, 0, 0)),
        out_shape=jax.ShapeDtypeStruct((b, SEQ, GROUP_W), F32),
        compiler_params=_params("arbitrary"),
        name="na_mixer",
    )(qkv, table, norm_g.reshape(1, GROUP_W))


ML_STATE_ROWS = 80


def _ml_kernel(qkv_ref, og_ref, sm_ref, bcol_ref, g_ref, o_ref, bcol_s, rows_s, lrow_s, qt_s, vt_s, htf_s, htb_s,
               c_s, m_s):
    _gate_rows(sm_ref, rows_s, SM_IG, 16)
    raw = rows_s[...].reshape(N_CHUNKS * 16, CHUNK) + bcol_ref[...]
    pre, suf = _chunk_scans(jax.nn.log_sigmoid(raw), 1)
    rid = lax.broadcasted_iota(jnp.int32, (N_CHUNKS * 16, 1), 0) % 16
    scanned = jnp.where(rid < 8 + N_HEADS, pre, suf)
    beta = raw - pltpu.roll(scanned, N_CHUNKS * 16 - 8, 0)
    lrow_s[...] = jnp.where(rid < 8, beta, scanned).reshape(N_CHUNKS, 16, CHUNK)

    pad_rows = jnp.zeros((CHUNK - 8, CHUNK), F32)
    for c in range(N_CHUNKS):
        rows = slice(c * CHUNK, (c + 1) * CHUNK)
        qt_s[c] = qkv_ref[0, rows, 0:GROUP_W].astype(F32).T.astype(BF16)
        vt_s[c] = qkv_ref[0, rows, 2 * GROUP_W:3 * GROUP_W].astype(F32).T.astype(BF16)
        bcol_s[rows, :] = jnp.concatenate([lrow_s[c, 0:8, :], pad_rows], axis=0).T

    si = lax.broadcasted_iota(jnp.int32, (CHUNK, CHUNK), 0)
    ti = lax.broadcasted_iota(jnp.int32, (CHUNK, CHUNK), 1)
    pad = ML_STATE_ROWS - HEAD_DIM
    one_rows = (lax.broadcasted_iota(jnp.int32, (pad, CHUNK), 0) == 0).astype(BF16)
    k_scale = HEAD_DIM ** -0.5
    c_s[...] = jnp.zeros_like(c_s)
    m_s[...] = jnp.zeros_like(m_s)

    def chunk_body(i, _):
        chunk = {0: i, 1: N_CHUNKS - 1 - i}
        rows = {d: pl.multiple_of(chunk[d] * CHUNK, CHUNK) for d in range(2)}
        for d in range(2):
            phases([(d, h) for h in range(N_HEADS)], chunk, rows)
        return 0

    def phases(units, chunk, rows):
        ks, qts, vts, cexts, raw_s, inters = {}, {}, {}, {}, {}, {}
        for d, h in units:
            j = d * N_HEADS + h
            k = qkv_ref[0, pl.ds(rows[d], CHUNK), GROUP_W + h * HEAD_DIM:GROUP_W + (h + 1) * HEAD_DIM]
            ks[d, h] = k * k_scale
            qts[d, h] = qt_s[chunk[d], h * HEAD_DIM:(h + 1) * HEAD_DIM, :]
            vts[d, h] = vt_s[chunk[d], h * HEAD_DIM:(h + 1) * HEAD_DIM, :]
            cexts[d, h] = c_s[j]
            raw_s[d, h] = _dot(ks[d, h], qts[d, h])
            inters[d, h] = _dot(cexts[d, h].astype(BF16), qts[d, h])
        s_ts, w_inters, floors, sources, keeps = {}, {}, {}, {}, {}
        for d, h in units:
            j = d * N_HEADS + h
            mask = (si <= ti) if d == 0 else (si >= ti)
            end = CHUNK - 1 if d == 0 else 0
            lrow = lrow_s[chunk[d]]
            beta_r = lrow[j:j + 1, :]
            b_r = lrow[8 + j:8 + j + 1, :]
            m_prev = m_s[j][0:1, 0:1]
            beta_m = jnp.where(mask, bcol_s[pl.ds(rows[d], CHUNK), j:j + 1], -jnp.inf)
            mu = jnp.maximum(m_prev, jnp.max(beta_m, axis=0, keepdims=True))
            s_ts[d, h] = raw_s[d, h] * jnp.exp(beta_m - mu)
            w_inters[d, h] = jnp.exp(m_prev - mu)
            floors[d, h] = jnp.exp(-(b_r + mu))
            top = jnp.maximum(m_prev, jnp.max(beta_r, axis=1, keepdims=True))
            v_ext = jnp.concatenate([vts[d, h], one_rows], axis=0).astype(F32)
            sources[d, h] = (v_ext * jnp.exp(beta_r - top)).astype(BF16)
            keeps[d, h] = jnp.exp(m_prev - top)
            m_s[j] = jnp.broadcast_to(b_r[:, end:end + 1] + top, (8, CHUNK))
        hs = []
        for d, h in units:
            s_t = s_ts[d, h]
            inter = inters[d, h]
            num = w_inters[d, h] * inter[0:HEAD_DIM, :] + _dot(vts[d, h], s_t.astype(BF16))
            den = w_inters[d, h] * inter[HEAD_DIM:HEAD_DIM + 1, :] + jnp.sum(s_t, axis=0, keepdims=True)
            hs.append(num / jnp.maximum(jnp.abs(den), floors[d, h]))
        d = units[0][0]
        h_t = jnp.concatenate(hs, axis=0)
        if d == 0:
            htf_s[chunk[d]] = h_t
        else:
            htb_s[chunk[d]] = h_t
        for d, h in units:
            c_s[d * N_HEADS + h] = keeps[d, h] * cexts[d, h] + _dot(sources[d, h], ks[d, h])

    lax.fori_loop(0, N_CHUNKS, chunk_body, 0, unroll=2)

    gain = g_ref[...]

    def out_body(c, _):
        r0 = pl.multiple_of(c * CHUNK, CHUNK)
        h_t = htf_s[c] + htb_s[c]
        normed = []
        for h in range(N_HEADS):
            x = h_t[h * HEAD_DIM:(h + 1) * HEAD_DIM, :]
            normed.append(x * lax.rsqrt(jnp.mean(x * x, axis=0, keepdims=True) + NORM_EPS))
        y = jnp.concatenate(normed, axis=0).T * gain
        o_ref[0, pl.ds(r0, CHUNK), :] = jax.nn.sigmoid(og_ref[0, pl.ds(r0, CHUNK), :]) * y
        return 0

    lax.fori_loop(0, N_CHUNKS, out_body, 0)


def _ml_mixer(qkv, og, sm, i_bias, f_bias, norm_g):
    b = qkv.shape[0]
    bias16 = jnp.concatenate([i_bias.reshape(8), f_bias.reshape(8)])
    bcol = jnp.tile(bias16.reshape(16, 1), (N_CHUNKS, 1))
    full = lambda shape: pl.BlockSpec(shape, lambda i: (0,) * len(shape))
    return pl.pallas_call(
        _ml_kernel,
        grid=(b,),
        in_specs=[pl.BlockSpec((1, SEQ, 3 * GROUP_W), lambda i: (i, 0, QKV_ML_BLOCK)),
                  pl.BlockSpec((1, SEQ, GROUP_W), lambda i: (i, 0, WIDE_MLO_BLOCK)),
                  pl.BlockSpec((1, SEQ, CHUNK), lambda i: (i, 0, WIDE_SM_BLOCK)),
                  full((N_CHUNKS * 16, 1)), full((1, GROUP_W))],
        out_specs=pl.BlockSpec((1, SEQ, GROUP_W), lambda i: (i, 0, 0)),
        out_shape=jax.ShapeDtypeStruct((b, SEQ, GROUP_W), F32),
        scratch_shapes=[pltpu.VMEM((SEQ, CHUNK), F32), pltpu.VMEM((N_CHUNKS, 16, CHUNK), F32),
                        pltpu.VMEM((N_CHUNKS, 16, CHUNK), F32), pltpu.VMEM((N_CHUNKS, GROUP_W, CHUNK), BF16),
                        pltpu.VMEM((N_CHUNKS, GROUP_W, CHUNK), BF16), pltpu.VMEM((N_CHUNKS, GROUP_W, CHUNK), F32),
                        pltpu.VMEM((N_CHUNKS, GROUP_W, CHUNK), F32),
                        pltpu.VMEM((2 * N_HEADS, ML_STATE_ROWS, HEAD_DIM), F32),
                        pltpu.VMEM((2 * N_HEADS, 8, CHUNK), F32)],
        compiler_params=_params("arbitrary"),
        name="mlstm_mixer",
    )(qkv, og, sm, bcol, norm_g.reshape(1, GROUP_W))


HALF = SEQ // 2


def _dft_tables():
    f = np.arange(HALF, dtype=np.int64)[:, None]
    t = np.arange(SEQ, dtype=np.int64)[None, :]
    ang = (2.0 * np.pi / FFT_N) * ((f * t) % FFT_N).astype(np.float64)
    wc = np.cos(ang)
    ws = np.sin(ang)
    ws[0, :] = np.array([1.0, 1.0, -1.0, -1.0])[np.arange(SEQ) % 4]
    tables = []
    for parity in range(2):
        w = np.stack([wc[:, parity::2], ws[:, parity::2]]).reshape(2, N_FREQ_BLOCKS, FREQ_BLOCK, HALF)
        w = np.ascontiguousarray(w.transpose(1, 0, 2, 3).reshape(2 * HALF, HALF)).astype(np.float32)
        tables.append((jnp.asarray(w).astype(BF16), jnp.asarray(np.ascontiguousarray(w.T)).astype(BF16)))
    return tables


def _hy_features():
    t = jnp.linspace(0.0, 1.0, SEQ, dtype=F32)[:, None]
    bands = (HY_EMB - 1) // 2
    f = jnp.linspace(1e-4, bands - 1, bands, dtype=F32)
    ang = (2.0 * math.pi) * (jnp.arange(SEQ, dtype=F32) / SEQ)[:, None] * f[None, :]
    feats = jnp.concatenate([t, jnp.cos(ang), -jnp.sin(ang)], axis=-1)
    deltas = jnp.abs(jnp.linspace(math.log(1e-2) / 1.5, math.log(1e-2) / 0.3, GROUP_W, dtype=F32))
    return jnp.pad(feats, ((0, 0), (0, CHUNK - HY_EMB))), t, deltas.reshape(1, GROUP_W)


def _hy_filter_kernel(feats_ref, w1_ref, b1_ref, w2_ref, b2_ref, w3_ref, freq_ref, t_ref, dl_ref, hp_ref, hm_ref):
    h = jnp.sin(freq_ref[0:1, :] * (_dot(feats_ref[...].astype(BF16), w1_ref[...].astype(BF16)) + b1_ref[...]))
    h = jnp.sin(freq_ref[1:2, :] * (_dot(h.astype(BF16), w2_ref[...].astype(BF16)) + b2_ref[...]))
    h = _dot(h.astype(BF16), w3_ref[...].astype(BF16))
    win = jnp.exp(-t_ref[...] * dl_ref[...])
    row = lax.broadcasted_iota(jnp.int32, (SEQ, 1), 0)
    for o in range(2):
        hf = h[:, (2 * o) * GROUP_W:(2 * o + 1) * GROUP_W] * win
        hb = h[:, (2 * o + 1) * GROUP_W:(2 * o + 2) * GROUP_W] * win
        r = lax.rsqrt(jnp.sum(hf * hf + hb * hb, axis=0, keepdims=True) + NORM_EPS)
        hf = hf * r
        hb = jnp.where(row == 0, 0.0, hb * r)
        hp_ref[:, o * GROUP_W:(o + 1) * GROUP_W] = (hf + hb).astype(BF16)
        hm_ref[:, o * GROUP_W:(o + 1) * GROUP_W] = (hf - hb).astype(BF16)


def _hy_spec_kernel(we_ref, wo_ref, hp_ref, hm_ref, kc_ref, ks_ref, kcp_ref, ksp_ref):
    fb = FREQ_BLOCK
    nc = 2 * GROUP_W
    row0 = (lax.broadcasted_iota(jnp.int32, (fb, 1), 0) + pl.program_id(0) * fb) == 0
    edge = jnp.where(row0, 1.0 / FFT_N, 2.0 / FFT_N)
    pe = _dot(we_ref[0:fb, :], hp_ref[:, 0:nc])
    po = _dot(wo_ref[0:fb, :], hp_ref[:, nc:])
    me = _dot(we_ref[fb:, :], hm_ref[:, 0:nc])
    mo = _dot(wo_ref[fb:, :], hm_ref[:, nc:])
    mid_cos = _dot(we_ref[fb:fb + 8, :], hp_ref[:, 0:nc])[0:1, :]
    kc_ref[...] = (pe + po) * edge
    kcp_ref[...] = (pe - po) * edge
    ks_ref[...] = jnp.where(row0, mid_cos, me + mo) * (2.0 / FFT_N)
    ksp_ref[...] = jnp.where(row0, mo, mo - me) * (2.0 / FFT_N)


def _hy_spectrum(tables, w1, b1, w2, b2, w3, freq):
    feats, t, deltas = _hy_features()
    w1p = jnp.pad(w1, ((0, CHUNK - HY_EMB), (0, 0)))
    hp, hm = pl.pallas_call(
        _hy_filter_kernel,
        out_shape=[jax.ShapeDtypeStruct((SEQ, 2 * GROUP_W), BF16)] * 2,
        compiler_params=pltpu.CompilerParams(vmem_limit_bytes=VMEM_LIMIT),
        name="hyena_filter",
    )(feats, w1p, b1.reshape(1, HY_FFN), w2, b2.reshape(1, HY_FFN), w3, freq, t, deltas)
    split = lambda h: h.reshape(HALF, 4 * GROUP_W)
    table_spec = pl.BlockSpec((2 * FREQ_BLOCK, HALF), lambda j: (j, 0))
    return pl.pallas_call(
        _hy_spec_kernel,
        grid=(N_FREQ_BLOCKS,),
        in_specs=[table_spec, table_spec,
                  pl.BlockSpec((HALF, 4 * GROUP_W), lambda j: (0, 0)),
                  pl.BlockSpec((HALF, 4 * GROUP_W), lambda j: (0, 0))],
        out_specs=[pl.BlockSpec((FREQ_BLOCK, 2 * GROUP_W), lambda j: (j, 0))] * 4,
        out_shape=[jax.ShapeDtypeStruct((HALF, 2 * GROUP_W), F32)] * 4,
        compiler_params=_params("arbitrary"),
        name="hyena_spectrum",
    )(tables[0][0], tables[1][0], split(hp), split(hm))


def _split_conv_rows(e_ref, o_ref, b, r0, w, bias):
    e = e_ref[b, pl.ds(r0, CHUNK), :]
    o = o_ref[b, pl.ds(r0, CHUNK), :]
    o_prev = o_ref[b, pl.ds(jnp.maximum(r0 - 1, 0), 1), :]
    e_next = e_ref[b, pl.ds(jnp.minimum(r0 + CHUNK, HALF - 1), 1), :]
    o_prev = jnp.where(r0 > 0, o_prev, 0.0)
    e_next = jnp.where(r0 + CHUNK < HALF, e_next, 0.0)
    row = lax.broadcasted_iota(jnp.int32, (CHUNK, 1), 0)
    o_dn = jnp.where(row == 0, o_prev, pltpu.roll(o, 1, 0))
    e_up = jnp.where(row == CHUNK - 1, e_next, pltpu.roll(e, CHUNK - 1, 0))
    conv_e = w[0:1, :] * o_dn + w[1:2, :] * e + w[2:3, :] * o + bias
    conv_o = w[0:1, :] * e + w[1:2, :] * o + w[2:3, :] * e_up + bias
    return conv_e, conv_o


def _hy_conv_kernel(ze_ref, zo_ref, ge_ref, go_ref, cwz_ref, cbz_ref, cwg_ref, cbg_ref, we_ref, wo_ref, wet_ref, wot_ref,
                    kc_ref, ks_ref, kcp_ref, ksp_ref, skip_ref, g_ref, o_ref, ze_s, zo_s, acce_s, acco_s, *,
                    conv_z, final_norm):
    j = pl.program_id(1)
    n_b = ze_ref.shape[0]
    fb = FREQ_BLOCK
    half_chunks = HALF // CHUNK

    def z_rows(b, r0):
        if conv_z:
            return _split_conv_rows(ze_ref, zo_ref, b, r0, cwz_ref[...], cbz_ref[...])
        return ze_ref[b, pl.ds(r0, CHUNK), :], zo_ref[b, pl.ds(r0, CHUNK), :]

    @pl.when(j == 0)
    def _():
        for b in range(n_b):
            def fill(c, _, b=b):
                r0 = pl.multiple_of(c * CHUNK, CHUNK)
                z_e, z_o = z_rows(b, r0)
                ze_s[pl.ds(r0, CHUNK), b * GROUP_W:(b + 1) * GROUP_W] = z_e.astype(BF16)
                zo_s[pl.ds(r0, CHUNK), b * GROUP_W:(b + 1) * GROUP_W] = z_o.astype(BF16)
                return 0

            lax.fori_loop(0, half_chunks, fill, 0)
        acce_s[...] = jnp.zeros_like(acce_s)
        acco_s[...] = jnp.zeros_like(acco_s)

    row0 = (lax.broadcasted_iota(jnp.int32, (fb, 1), 0) + j * fb) == 0
    tiled = lambda ref: jnp.concatenate([ref[...]] * n_b, axis=1)
    kc, ks, kcp, ksp = tiled(kc_ref), tiled(ks_ref), tiled(kcp_ref), tiled(ksp_ref)
    xe = _dot(we_ref[...], ze_s[...])
    xo = _dot(wo_ref[...], zo_s[...])
    ae, be, ao, bo = xe[0:fb, :], xe[fb:, :], xo[0:fb, :], xo[fb:, :]
    xc, xcp, xs, xsp = ae + ao, ae - ao, be + bo, bo - be
    yc = xc * kc - xs * ks
    ys = xc * ks + xs * kc
    ycp = xcp * kcp - xsp * ksp
    ysp = xcp * ksp + xsp * kcp
    dc, nyq = xc * kc, xcp * kcp
    pc = jnp.where(row0, dc + nyq, yc + ycp)
    mc = jnp.where(row0, dc - nyq, yc - ycp)
    ms = jnp.where(row0, be * ks - bo * ksp, ys - ysp)
    ps = jnp.where(row0, be * ksp + bo * ks, ys + ysp)
    acce_s[...] += _dot(wet_ref[...], jnp.concatenate([pc, ms], axis=0).astype(BF16))
    acco_s[...] += _dot(wot_ref[...], jnp.concatenate([mc, ps], axis=0).astype(BF16))

    @pl.when(j == pl.num_programs(1) - 1)
    def _():
        for b in range(n_b):
            def finish(c, _, b=b):
                r0 = pl.multiple_of(c * CHUNK, CHUNK)
                gates = _split_conv_rows(ge_ref, go_ref, b, r0, cwg_ref[...], cbg_ref[...])
                for parity, (gate, z, acc) in enumerate(zip(gates, z_rows(b, r0), (acce_s, acco_s))):
                    r = gate * (acc[pl.ds(r0, CHUNK), b * GROUP_W:(b + 1) * GROUP_W] + skip_ref[...] * z)
                    if final_norm:
                        r = _rms(r) * g_ref[...]
                    o_ref[b, pl.ds(r0, CHUNK), parity * GROUP_W:(parity + 1) * GROUP_W] = r
                return 0

            lax.fori_loop(0, half_chunks, finish, 0)


def _hy_long_conv(z, z_cols, gates, gate_cols, conv_w, conv_b, conv_z, tables, spectrum, order, skip, norm_g,
                  final_norm):
    b = z.shape[0]
    bg = HY_BATCH_BLOCK
    taps = lambda col: (conv_w[:, col * GROUP_W:(col + 1) * GROUP_W],
                        conv_b[col * GROUP_W:(col + 1) * GROUP_W].reshape(1, GROUP_W))
    small = lambda shape: pl.BlockSpec(shape, lambda g, j: (0, 0))
    act = lambda col: pl.BlockSpec((bg, HALF, GROUP_W), lambda g, j: (g, 0, col))
    fwd = pl.BlockSpec((2 * FREQ_BLOCK, HALF), lambda g, j: (j, 0))
    inv = pl.BlockSpec((HALF, 2 * FREQ_BLOCK), lambda g, j: (0, j))
    spec = pl.BlockSpec((FREQ_BLOCK, GROUP_W), lambda g, j: (j, order))
    (we, wet), (wo, wot) = tables
    return pl.pallas_call(
        functools.partial(_hy_conv_kernel, conv_z=conv_z, final_norm=final_norm),
        grid=(b // bg, N_FREQ_BLOCKS),
        in_specs=[act(z_cols[0]), act(z_cols[1]), act(gate_cols[0]), act(gate_cols[1]),
                  small((3, GROUP_W)), small((1, GROUP_W)), small((3, GROUP_W)), small((1, GROUP_W)),
                  fwd, fwd, inv, inv, spec, spec, spec, spec, small((1, GROUP_W)), small((1, GROUP_W))],
        out_specs=pl.BlockSpec((bg, HALF, 2 * GROUP_W), lambda g, j: (g, 0, 0)),
        out_shape=jax.ShapeDtypeStruct((b, HALF, 2 * GROUP_W), F32),
        scratch_shapes=[pltpu.VMEM((HALF, bg * GROUP_W), BF16), pltpu.VMEM((HALF, bg * GROUP_W), BF16),
                        pltpu.VMEM((HALF, bg * GROUP_W), F32), pltpu.VMEM((HALF, bg * GROUP_W), F32)],
        compiler_params=_params("arbitrary", "arbitrary"),
        name="hyena_long_conv",
    )(z, z, gates, gates, *taps(0), *taps(order + 1), we, wo, wet, wot, *spectrum,
      skip.reshape(1, GROUP_W), norm_g.reshape(1, GROUP_W))


def _hy_mixer(u, tables, conv_w, conv_b, w1, b1, w2, b2, w3, freq, skip, norm_g):
    spectrum = _hy_spectrum(tables, w1, b1, w2, b2, w3, freq)
    z1 = _hy_long_conv(u, (0, 3), u, (1, 4), conv_w, conv_b, True, tables, spectrum, 0, skip[0], norm_g, False)
    return _hy_long_conv(z1, (0, 1), u, (2, 5), conv_w, conv_b, False, tables, spectrum, 1, skip[1], norm_g, True)


FFN_SPLITS = ((0, 1024), (1024, 2048), (2048, FFN_HIDDEN))


def _ffn_kernel(x_ref, mod_ref, ya_ref, yb_ref, yc_ref, yd_ref, perm_ref, wo_ref, wg_ref, wu_ref, wd_ref, gf_ref, o_ref,
                *, final_norm):
    yd = yd_ref[...]
    yd = jnp.concatenate([yd[:, 0:GROUP_W], yd[:, GROUP_W:]], axis=0).astype(BF16)
    yd = _dot(perm_ref[...], yd)
    y_mix = jnp.concatenate([ya_ref[...], yb_ref[...], yc_ref[...], yd], axis=1).astype(BF16)
    x = x_ref[...] + mod_ref[0, 2:3, :] * _dot(y_mix, wo_ref[...])
    h = (_rms(x) * (1.0 + mod_ref[0, 4:5, :]) + mod_ref[0, 3:4, :]).astype(BF16)
    acc = jnp.zeros_like(x)
    for lo, hi in FFN_SPLITS:
        a = _silu(_dot(h, wg_ref[:, lo:hi])) * _dot(h, wu_ref[:, lo:hi])
        acc = acc + _dot(a.astype(BF16), wd_ref[lo:hi, :])
    y = x + mod_ref[0, 5:6, :] * acc
    if final_norm:
        y = _rms(y) * gf_ref[...]
    o_ref[...] = y


def _out_ffn(x2, mod_l, ys, w_out, wg, wu, wd, final_g, final_norm):
    n_tok, d = x2.shape
    tm = TOKEN_TILE
    tiles_per_seq = SEQ // tm
    tok = np.arange(tm)
    perm = (np.arange(tm)[None, :] == (tok // 2 + (tok % 2) * (tm // 2))[:, None]).astype(np.float32)
    resident = lambda shape: pl.BlockSpec(shape, lambda i: (0, 0), pipeline_mode=pl.Buffered(1))
    return pl.pallas_call(
        functools.partial(_ffn_kernel, final_norm=final_norm),
        grid=(n_tok // tm,),
        in_specs=[pl.BlockSpec((tm, d), lambda i: (i, 0)),
                  pl.BlockSpec((1, 6, d), lambda i: (i // tiles_per_seq, 0, 0))]
                 + [pl.BlockSpec((tm, GROUP_W), lambda i: (i, 0))] * 3
                 + [pl.BlockSpec((tm // 2, 2 * GROUP_W), lambda i: (i, 0)), resident((tm, tm)),
                    resident((d, d)), resident((d, FFN_HIDDEN)), resident((d, FFN_HIDDEN)), resident((FFN_HIDDEN, d)),
                    pl.BlockSpec((1, d), lambda i: (0, 0))],
        out_specs=pl.BlockSpec((tm, d), lambda i: (i, 0)),
        out_shape=jax.ShapeDtypeStruct((n_tok, d), F32),
        compiler_params=_params("arbitrary"),
        name="out_proj_swiglu",
    )(x2, mod_l, *ys, jnp.asarray(perm).astype(BF16), w_out, wg, wu, wd, final_g.reshape(1, d))


def _permute_w_in(w):
    cols = [w[:, 0:1024], w[:, 2568:2824], w[:, 1024:1032], w[:, 2824:2840],
            jnp.zeros((w.shape[0], P_PAD - 3608), w.dtype), w[:, 1032:1800], w[:, 1800:2568], w[:, 2840:3608]]
    return jnp.concatenate(cols, axis=1).astype(BF16)


def kernel(x, c, mod_w, mod_b, w_in, ssd_conv_w, ssd_conv_b, ssd_dt_bias, ssd_a_log, ssd_d, ssd_norm_g, na_rpb, na_norm_g, ml_i_bias, ml_f_bias, ml_norm_g, hy_conv_w, hy_conv_b, hy_w1, hy_b1, hy_w2, hy_b2, hy_w3, hy_freq, hy_skip, hy_norm_g, w_out, ffn_w_gate, ffn_w_up, ffn_w_down, final_norm_g):
    b, seq, d = x.shape
    assert seq == SEQ and d == D_MODEL and b % HY_BATCH_BLOCK == 0
    depth = mod_w.shape[0]
    mod = _modulation(c, mod_w, mod_b).reshape(depth, b, 6, d)
    dft = _dft_tables()
    x2 = x.reshape(b * seq, d)
    for l in range(depth):
        wide, qkv, hy = _in_proj(x2, mod[l], _permute_w_in(w_in[l]))
        wide = wide.reshape(b, seq, WIDE_W)
        qkv = qkv.reshape(b, seq, QKV_W)
        y_ssd = _ssd_mixer(wide, wide, ssd_conv_w[l], ssd_conv_b[l], ssd_dt_bias[l], ssd_a_log[l], ssd_d[l],
                           ssd_norm_g[l])
        y_na = _na_mixer(qkv, na_rpb[l], na_norm_g[l])
        y_ml = _ml_mixer(qkv, wide, wide, ml_i_bias[l], ml_f_bias[l], ml_norm_g[l])
        y_hy = _hy_mixer(hy.reshape(b, HALF, 2 * HY_W), dft, hy_conv_w[l], hy_conv_b[l], hy_w1[l], hy_b1[l], hy_w2[l],
                         hy_b2[l], hy_w3[l], hy_freq[l], hy_skip[l], hy_norm_g[l])
        ys = [t.reshape(b * seq, GROUP_W) for t in (y_ssd, y_na, y_ml)] + [y_hy.reshape(b * HALF, 2 * GROUP_W)]
        x2 = _out_ffn(x2, mod[l], ys, w_out[l].astype(BF16), ffn_w_gate[l].astype(BF16), ffn_w_up[l].astype(BF16),
                      ffn_w_down[l].astype(BF16), final_norm_g, l == depth - 1)
    return x2.reshape(b, seq, d)
```

```python
import functools
import math

import numpy as np
import jax
import jax.numpy as jnp
from jax import lax
from jax.experimental import pallas as pl
from jax.experimental.pallas import tpu as pltpu

F32 = jnp.float32
BF16 = jnp.bfloat16

D_MODEL = 1024
SEQ = 2048
GRID_W = 64
GROUP_W = 256
HEAD_DIM = 64
N_HEADS = 4
SSD_STATE = 128
SSD_XBC = 768
CHUNK = 128
N_CHUNKS = SEQ // CHUNK
NA_KH = 8
NA_KW = 16
NA_ROW_GROUP = 4
HY_EMB = 33
HY_FFN = 64
FFN_HIDDEN = 2816
NORM_EPS = 1e-6
FFT_N = 2 * SEQ
FREQ_BLOCK = 256
N_FREQ_BLOCKS = SEQ // 2 // FREQ_BLOCK
HY_BATCH_BLOCK = 2
TOKEN_TILE = 512
IN_PROJ_TILE = 1024
VMEM_LIMIT = 56 * 1024 * 1024

WIDE_W = 1408
WIDE_ZX_BLOCK = 0
WIDE_MLO_BLOCK = 4
WIDE_SM_BLOCK = 10
QKV_W = 1536
QKV_NA_BLOCK = 0
QKV_ML_BLOCK = 1
HY_W = 768
P_PAD = WIDE_W + QKV_W + HY_W
SM_DT = 0
SM_IG = 8
SM_FG = 16

NT_DIMS = (((1,), (1,)), ((), ()))
TN_DIMS = (((0,), (0,)), ((), ()))


def _rms(x):
    return x * lax.rsqrt(jnp.mean(x * x, axis=-1, keepdims=True) + NORM_EPS)


def _silu(x):
    return x * jax.nn.sigmoid(x)


def _dot(a, b):
    return jnp.dot(a, b, preferred_element_type=F32)


def _params(*sem):
    return pltpu.CompilerParams(dimension_semantics=sem, vmem_limit_bytes=VMEM_LIMIT)


def _mod_kernel(c_ref, w_ref, b_ref, o_ref):
    cond = _silu(c_ref[...]).astype(BF16)
    o_ref[0] = _dot(cond, w_ref[0].astype(BF16)) + b_ref[0]


def _modulation(c, mod_w, mod_b):
    depth, d, n = mod_w.shape
    b = c.shape[0]
    wn = n // 2
    return pl.pallas_call(
        _mod_kernel,
        grid=(depth, n // wn),
        in_specs=[pl.BlockSpec((b, d), lambda l, j: (0, 0)),
                  pl.BlockSpec((1, d, wn), lambda l, j: (l, 0, j)),
                  pl.BlockSpec((1, 1, wn), lambda l, j: (l, 0, j))],
        out_specs=pl.BlockSpec((1, b, wn), lambda l, j: (l, 0, j)),
        out_shape=jax.ShapeDtypeStruct((depth, b, n), F32),
        compiler_params=_params("arbitrary", "arbitrary"),
        name="adaln_mod",
    )(c, mod_w, mod_b.reshape(depth, 1, n))


def _in_proj_kernel(x_ref, mod_ref, w_ref, wide_ref, qkv_ref, hy_ref, cols_s):
    h = (_rms(x_ref[...]) * (1.0 + mod_ref[0, 1:2, :]) + mod_ref[0, 0:1, :]).astype(BF16)
    wide_ref[...] = _dot(h, w_ref[:, 0:WIDE_W])
    qkv_ref[...] = _dot(h, w_ref[:, WIDE_W:WIDE_W + QKV_W]).astype(BF16)
    half = x_ref.shape[0] // 2
    n_slabs = cols_s.shape[0]
    for cb in range(n_slabs):
        cols_s[cb] = h[:, cb * CHUNK:(cb + 1) * CHUNK].astype(F32)
    w_hy = w_ref[:, WIDE_W + QKV_W:P_PAD]
    for parity in range(2):
        rows = jnp.concatenate([cols_s[cb, pl.ds(parity, half, stride=2), :] for cb in range(n_slabs)], axis=1)
        hy_ref[:, parity * HY_W:(parity + 1) * HY_W] = _dot(rows.astype(BF16), w_hy)


def _in_proj(x2, mod_l, w_perm, layer):
    n_tok, d = x2.shape
    tm = IN_PROJ_TILE
    tiles_per_seq = SEQ // tm
    return pl.pallas_call(
        _in_proj_kernel,
        grid=(n_tok // tm,),
        in_specs=[pl.BlockSpec((tm, d), lambda i: (i, 0)),
                  pl.BlockSpec((1, 6, d), lambda i: (i // tiles_per_seq, 0, 0)),
                  pl.BlockSpec((None, d, P_PAD), lambda i: (layer, 0, 0), pipeline_mode=pl.Buffered(1))],
        out_specs=[pl.BlockSpec((tm, WIDE_W), lambda i: (i, 0)), pl.BlockSpec((tm, QKV_W), lambda i: (i, 0)),
                   pl.BlockSpec((tm // 2, 2 * HY_W), lambda i: (i, 0))],
        out_shape=[jax.ShapeDtypeStruct((n_tok, WIDE_W), F32), jax.ShapeDtypeStruct((n_tok, QKV_W), BF16),
                   jax.ShapeDtypeStruct((n_tok // 2, 2 * HY_W), F32)],
        scratch_shapes=[pltpu.VMEM((d // CHUNK, tm, CHUNK), F32)],
        compiler_params=_params("arbitrary"),
        name="in_proj",
    )(x2, mod_l, w_perm)


def _dwconv3_rows(src_ref, b, r0, cols, w, bias):
    x = src_ref[b, pl.ds(r0, CHUNK), cols]
    up = src_ref[b, pl.ds(jnp.maximum(r0 - 1, 0), 1), cols]
    dn = src_ref[b, pl.ds(jnp.minimum(r0 + CHUNK, SEQ - 1), 1), cols]
    up = jnp.where(r0 > 0, up, 0.0)
    dn = jnp.where(r0 + CHUNK < SEQ, dn, 0.0)
    row = lax.broadcasted_iota(jnp.int32, (CHUNK, 1), 0)
    prev = jnp.where(row == 0, up, pltpu.roll(x, 1, 0))
    nxt = jnp.where(row == CHUNK - 1, dn, pltpu.roll(x, CHUNK - 1, 0))
    return w[0:1, :] * prev + w[1:2, :] * x + w[2:3, :] * nxt + bias


def _chunk_scans(a, axis):
    n = a.shape[axis]
    shape = [1, 1]
    shape[axis] = n
    pos = lax.broadcasted_iota(jnp.int32, tuple(shape), axis) % CHUNK
    pre, suf = a, a
    k = 1
    while k < CHUNK:
        pre = pre + jnp.where(pos >= k, pltpu.roll(pre, k, axis), 0.0)
        suf = suf + jnp.where(pos < CHUNK - k, pltpu.roll(suf, n - k, axis), 0.0)
        k *= 2
    return pre, suf


def _gate_rows(sm_ref, rows_s, lo, n):
    for c in range(N_CHUNKS):
        t = sm_ref[0, c * CHUNK:(c + 1) * CHUNK, :].T
        rows_s[c] = t[lo:lo + n, :]


def _ssd_kernel(zx_ref, sm_ref, cw_ref, cb_ref, dtb_col_ref, a_col_ref, dsk_ref, g_ref,
                o_ref, xs_s, b_s, xst_s, ct_s, acol_s, rows_s, dtrow_s, arow_s, ytf_s, ytb_s, st_s):
    cw = cw_ref[...]
    cb = cb_ref[...]

    def conv_body(c, _):
        r0 = pl.multiple_of(c * CHUNK, CHUNK)
        v = _silu(_dwconv3_rows(zx_ref, 0, r0, slice(GROUP_W, GROUP_W + SSD_XBC), cw, cb))
        xs_s[pl.ds(r0, CHUNK), :] = v[:, 0:GROUP_W]
        b_s[pl.ds(r0, CHUNK), :] = v[:, GROUP_W:2 * GROUP_W].astype(BF16)
        xst_s[c] = v[:, 0:GROUP_W].T
        ct_s[c] = v[:, 2 * GROUP_W:3 * GROUP_W].T.astype(BF16)
        return 0

    lax.fori_loop(0, N_CHUNKS, conv_body, 0)

    _gate_rows(sm_ref, rows_s, SM_DT, 8)
    dt_row = jax.nn.softplus(rows_s[...].reshape(N_CHUNKS * 8, CHUNK) + dtb_col_ref[...])
    pre, suf = _chunk_scans(dt_row * (-jnp.exp(a_col_ref[...])), 1)
    rid = lax.broadcasted_iota(jnp.int32, (N_CHUNKS * 8, 1), 0) % 8
    dtrow_s[...] = dt_row.reshape(N_CHUNKS, 8, CHUNK)
    arow_s[...] = jnp.where(rid < N_HEADS, pre, suf).reshape(N_CHUNKS, 8, CHUNK)
    pad_rows = jnp.zeros((CHUNK - 8, CHUNK), F32)
    for c in range(N_CHUNKS):
        acol_s[c * CHUNK:(c + 1) * CHUNK, :] = jnp.concatenate([arow_s[c], pad_rows], axis=0).T

    si = lax.broadcasted_iota(jnp.int32, (CHUNK, CHUNK), 0)
    ti = lax.broadcasted_iota(jnp.int32, (CHUNK, CHUNK), 1)
    first = lax.broadcasted_iota(jnp.int32, (2 * HEAD_DIM, 1), 0) < HEAD_DIM
    st_s[...] = jnp.zeros_like(st_s)

    def chunk_body(i, _):
        units = [(d, g) for d in range(2) for g in range(2)]
        chunk = {0: i, 1: N_CHUNKS - 1 - i}
        rows = {d: pl.multiple_of(chunk[d] * CHUNK, CHUNK) for d in range(2)}
        bgs, cgs, prevs, scores, y_offs = {}, {}, {}, {}, {}
        for d, g in units:
            bgs[d, g] = b_s[pl.ds(rows[d], CHUNK), g * SSD_STATE:(g + 1) * SSD_STATE]
            cgs[d, g] = ct_s[chunk[d], g * SSD_STATE:(g + 1) * SSD_STATE, :]
            prevs[d, g] = st_s[d, g]
            scores[d, g] = _dot(bgs[d, g], cgs[d, g])
            y_offs[d, g] = _dot(prevs[d, g].astype(BF16), cgs[d, g])
        weighted, xdts, x_ends, carries, grow = {}, {}, {}, {}, {}
        for d, g in units:
            mask = (si <= ti) if d == 0 else (si >= ti)
            end = CHUNK - 1 if d == 0 else 0
            acol = acol_s[pl.ds(rows[d], CHUNK), :]
            arow = arow_s[chunk[d]]
            dtr = dtrow_s[chunk[d]]
            for hh in range(2):
                h = 2 * g + hh
                j = d * N_HEADS + h
                ar = arow[j:j + 1, :]
                a_end = ar[:, end:end + 1]
                decay = jnp.exp(jnp.where(mask, ar - acol[:, j:j + 1], -jnp.inf))
                xdt_t = xst_s[chunk[d], h * HEAD_DIM:(h + 1) * HEAD_DIM, :] * dtr[j:j + 1, :]
                weighted[d, h] = (scores[d, g] * decay).astype(BF16)
                xdts[d, h] = xdt_t.astype(BF16)
                x_ends[d, h] = (xdt_t * jnp.exp(a_end - ar)).astype(BF16)
                carries[d, h] = jnp.exp(a_end)
                grow[d, h] = jnp.exp(ar)
        for d in range(2):
            ys = []
            for h in range(N_HEADS):
                g, hh = divmod(h, 2)
                y_off = y_offs[d, g][hh * HEAD_DIM:(hh + 1) * HEAD_DIM, :]
                ys.append(_dot(xdts[d, h], weighted[d, h]) + y_off * grow[d, h])
            y_t = jnp.concatenate(ys, axis=0)
            if d == 0:
                ytf_s[chunk[d]] = y_t
            else:
                ytb_s[chunk[d]] = y_t
        for d, g in units:
            keep = jnp.where(first, carries[d, 2 * g], carries[d, 2 * g + 1])
            x_end = jnp.concatenate([x_ends[d, 2 * g], x_ends[d, 2 * g + 1]], axis=0)
            st_s[d, g] = prevs[d, g] * keep + _dot(x_end, bgs[d, g])
        return 0

    lax.fori_loop(0, N_CHUNKS, chunk_body, 0, unroll=2)

    dsk = dsk_ref[...]
    gain = g_ref[...]

    def out_body(c, _):
        r0 = pl.multiple_of(c * CHUNK, CHUNK)
        y = (ytf_s[c] + ytb_s[c]).T + dsk * xs_s[pl.ds(r0, CHUNK), :]
        y = y * _silu(zx_ref[0, pl.ds(r0, CHUNK), 0:GROUP_W])
        o_ref[0, pl.ds(r0, CHUNK), :] = _rms(y) * gain
        return 0

    lax.fori_loop(0, N_CHUNKS, out_body, 0)


def _ssd_mixer(zx, sm, conv_w, conv_b, dt_bias, a_log, d_skip, norm_g):
    b = zx.shape[0]
    tile_col = lambda v: jnp.tile(v.reshape(8, 1), (N_CHUNKS, 1))
    full = lambda shape: pl.BlockSpec(shape, lambda i: (0,) * len(shape))
    return pl.pallas_call(
        _ssd_kernel,
        grid=(b,),
        in_specs=[pl.BlockSpec((1, SEQ, 1024), lambda i: (i, 0, WIDE_ZX_BLOCK)),
                  pl.BlockSpec((1, SEQ, CHUNK), lambda i: (i, 0, WIDE_SM_BLOCK)),
                  full((3, SSD_XBC)), full((1, SSD_XBC)),
                  full((N_CHUNKS * 8, 1)), full((N_CHUNKS * 8, 1)), full((1, GROUP_W)), full((1, GROUP_W))],
        out_specs=pl.BlockSpec((1, SEQ, GROUP_W), lambda i: (i, 0, 0)),
        out_shape=jax.ShapeDtypeStruct((b, SEQ, GROUP_W), F32),
        scratch_shapes=[pltpu.VMEM((SEQ, GROUP_W), F32), pltpu.VMEM((SEQ, GROUP_W), BF16),
                        pltpu.VMEM((N_CHUNKS, GROUP_W, CHUNK), F32), pltpu.VMEM((N_CHUNKS, GROUP_W, CHUNK), BF16),
                        pltpu.VMEM((SEQ, CHUNK), F32), pltpu.VMEM((N_CHUNKS, 8, CHUNK), F32),
                        pltpu.VMEM((N_CHUNKS, 8, CHUNK), F32), pltpu.VMEM((N_CHUNKS, 8, CHUNK), F32),
                        pltpu.VMEM((N_CHUNKS, GROUP_W, CHUNK), F32), pltpu.VMEM((N_CHUNKS, GROUP_W, CHUNK), F32),
                        pltpu.VMEM((2, 2, 2 * HEAD_DIM, SSD_STATE), F32)],
        compiler_params=_params("arbitrary"),
        name="ssd_mixer",
    )(zx, sm, conv_w, conv_b.reshape(1, SSD_XBC), tile_col(dt_bias), tile_col(a_log),
      jnp.repeat(d_skip, HEAD_DIM).reshape(1, GROUP_W), norm_g.reshape(1, GROUP_W))


def _na_kernel(qkv_ref, bias_ref, g_ref, o_ref):
    n_rows = SEQ // GRID_W
    k_tok = NA_KH * GRID_W
    pair_w = 2 * HEAD_DIM
    first = lax.broadcasted_iota(jnp.int32, (1, pair_w), 1) < HEAD_DIM
    gain = g_ref[...]

    def row_group(g, _):
        units = [(i, pair) for i in range(NA_ROW_GROUP) for pair in range(N_HEADS // 2)]
        q0, k0, dr0, scores, values = {}, {}, {}, {}, {}
        for i in range(NA_ROW_GROUP):
            r = g * NA_ROW_GROUP + i
            start = jnp.clip(r - NA_KH // 2, 0, n_rows - NA_KH)
            dr0[i] = start - r + NA_KH - 1
            q0[i] = pl.multiple_of(r * GRID_W, GRID_W)
            k0[i] = pl.multiple_of(start * GRID_W, GRID_W)
        for i, pair in units:
            lo = pair * pair_w
            q2 = qkv_ref[0, pl.ds(q0[i], GRID_W), lo:lo + pair_w] * (HEAD_DIM ** -0.5)
            k2 = qkv_ref[0, pl.ds(k0[i], k_tok), GROUP_W + lo:GROUP_W + lo + pair_w]
            values[i, pair] = qkv_ref[0, pl.ds(k0[i], k_tok), 2 * GROUP_W + lo:2 * GROUP_W + lo + pair_w]
            qm = jnp.concatenate([jnp.where(first, q2, 0.0), jnp.where(first, 0.0, q2)], axis=0).astype(BF16)
            scores[i, pair] = lax.dot_general(qm, k2, NT_DIMS, preferred_element_type=F32)
        probs, sums = {}, {}
        for i, pair in units:
            bias = jnp.concatenate(
                [jnp.concatenate([bias_ref[2 * pair + hh, dr0[i] + 2 * m] for m in range(NA_KH // 2)], axis=1)
                 for hh in range(2)], axis=0)
            s = scores[i, pair] + bias
            p = jnp.exp(s - jnp.max(s, axis=-1, keepdims=True))
            probs[i, pair] = p.astype(BF16)
            sums[i, pair] = jnp.sum(p, axis=-1, keepdims=True)
        for i in range(NA_ROW_GROUP):
            outs = []
            for pair in range(N_HEADS // 2):
                o2 = _dot(probs[i, pair], values[i, pair]) / sums[i, pair]
                outs.append(jnp.where(first, o2[0:GRID_W, :], o2[GRID_W:, :]))
            o_ref[0, pl.ds(q0[i], GRID_W), :] = _rms(jnp.concatenate(outs, axis=1)) * gain
        return 0

    lax.fori_loop(0, n_rows // NA_ROW_GROUP, row_group, 0, unroll=2)


def _na_bias_tiles(rpb):
    col = np.arange(GRID_W)
    cs = np.clip(col - NA_KW // 2, 0, GRID_W - NA_KW)
    valid_c = (col[None, :] >= cs[:, None]) & (col[None, :] < cs[:, None] + NA_KW)
    dc = col[None, :] - col[:, None] + NA_KW - 1
    sel_c = (valid_c[None] & (dc[None] == np.arange(2 * NA_KW - 1)[:, None, None])).astype(np.float32)
    by_row = jnp.einsum("hab,bqk->haqk", rpb, sel_c, precision=lax.Precision.HIGHEST)
    by_row = jnp.where(valid_c[None, None], by_row, -1e30)
    return jnp.concatenate([by_row[:, :-1], by_row[:, 1:]], axis=-1).astype(F32)


def _na_mixer(qkv, rpb, norm_g):
    b = qkv.shape[0]
    table = _na_bias_tiles(rpb)
    return pl.pallas_call(
        _na_kernel,
        grid=(b,),
        in_specs=[pl.BlockSpec((1, SEQ, 3 * GROUP_W), lambda i: (i, 0, QKV_NA_BLOCK)),
                  pl.BlockSpec(table.shape, lambda i: (0, 0, 0, 0)),
                  pl.BlockSpec((1, GROUP_W), lambda i: (0, 0))],
        out_specs=pl.BlockSpec((1, SEQ, GROUP_W), lambda i: (i, 0, 0)),
        out_shape=jax.ShapeDtypeStruct((b, SEQ, GROUP_W), F32),
        compiler_params=_params("arbitrary"),
        name="na_mixer",
    )(qkv, table, norm_g.reshape(1, GROUP_W))


ML_STATE_ROWS = 80
ML_PHASE_HEADS = 4


def _ml_kernel(qkv_ref, og_ref, sm_ref, bcol_ref, g_ref, o_ref, bcol_s, rows_s, lrow_s, qt_s, vt_s, htf_s, htb_s,
               c_s, m_s):
    _gate_rows(sm_ref, rows_s, SM_IG, 16)
    raw = rows_s[...].reshape(N_CHUNKS * 16, CHUNK) + bcol_ref[...]
    pre, suf = _chunk_scans(jax.nn.log_sigmoid(raw), 1)
    rid = lax.broadcasted_iota(jnp.int32, (N_CHUNKS * 16, 1), 0) % 16
    scanned = jnp.where(rid < 8 + N_HEADS, pre, suf)
    beta = raw - pltpu.roll(scanned, N_CHUNKS * 16 - 8, 0)
    lrow_s[...] = jnp.where(rid < 8, beta, scanned).reshape(N_CHUNKS, 16, CHUNK)

    pad_rows = jnp.zeros((CHUNK - 8, CHUNK), F32)
    for c in range(N_CHUNKS):
        rows = slice(c * CHUNK, (c + 1) * CHUNK)
        qt_s[c] = qkv_ref[0, rows, 0:GROUP_W].astype(F32).T.astype(BF16)
        vt_s[c] = qkv_ref[0, rows, 2 * GROUP_W:3 * GROUP_W].astype(F32).T.astype(BF16)
        bcol_s[rows, :] = jnp.concatenate([lrow_s[c, 0:8, :], pad_rows], axis=0).T

    si = lax.broadcasted_iota(jnp.int32, (CHUNK, CHUNK), 0)
    ti = lax.broadcasted_iota(jnp.int32, (CHUNK, CHUNK), 1)
    pad = ML_STATE_ROWS - HEAD_DIM
    one_rows = (lax.broadcasted_iota(jnp.int32, (pad, CHUNK), 0) == 0).astype(BF16)
    k_scale = HEAD_DIM ** -0.5
    c_s[...] = jnp.zeros_like(c_s)
    m_s[...] = jnp.zeros_like(m_s)

    def chunk_body(i, _):
        chunk = {0: i, 1: N_CHUNKS - 1 - i}
        rows = {d: pl.multiple_of(chunk[d] * CHUNK, CHUNK) for d in range(2)}
        for d in range(2):
            for h0 in range(0, N_HEADS, ML_PHASE_HEADS):
                phases([(d, h) for h in range(h0, h0 + ML_PHASE_HEADS)], chunk, rows)
        return 0

    def phases(units, chunk, rows):
        ks, qts, vts, cexts, raw_s, inters = {}, {}, {}, {}, {}, {}
        for d, h in units:
            j = d * N_HEADS + h
            k = qkv_ref[0, pl.ds(rows[d], CHUNK), GROUP_W + h * HEAD_DIM:GROUP_W + (h + 1) * HEAD_DIM]
            ks[d, h] = k * k_scale
            qts[d, h] = qt_s[chunk[d], h * HEAD_DIM:(h + 1) * HEAD_DIM, :]
            vts[d, h] = vt_s[chunk[d], h * HEAD_DIM:(h + 1) * HEAD_DIM, :]
            cexts[d, h] = c_s[j]
            raw_s[d, h] = _dot(ks[d, h], qts[d, h])
            inters[d, h] = _dot(cexts[d, h].astype(BF16), qts[d, h])
        s_ts, w_inters, floors, sources, keeps = {}, {}, {}, {}, {}
        for d, h in units:
            j = d * N_HEADS + h
            mask = (si <= ti) if d == 0 else (si >= ti)
            end = CHUNK - 1 if d == 0 else 0
            lrow = lrow_s[chunk[d]]
            beta_r = lrow[j:j + 1, :]
            b_r = lrow[8 + j:8 + j + 1, :]
            m_prev = m_s[j][0:1, 0:1]
            beta_m = jnp.where(mask, bcol_s[pl.ds(rows[d], CHUNK), j:j + 1], -jnp.inf)
            mu = jnp.maximum(m_prev, jnp.max(beta_m, axis=0, keepdims=True))
            s_ts[d, h] = raw_s[d, h] * jnp.exp(beta_m - mu)
            w_inters[d, h] = jnp.exp(m_prev - mu)
            floors[d, h] = jnp.exp(-(b_r + mu))
            top = jnp.maximum(m_prev, jnp.max(beta_r, axis=1, keepdims=True))
            v_ext = jnp.concatenate([vts[d, h], one_rows], axis=0).astype(F32)
            sources[d, h] = (v_ext * jnp.exp(beta_r - top)).astype(BF16)
            keeps[d, h] = jnp.exp(m_prev - top)
            m_s[j] = jnp.broadcast_to(b_r[:, end:end + 1] + top, (8, CHUNK))
        for d, h in units:
            s_t = s_ts[d, h]
            inter = inters[d, h]
            num = w_inters[d, h] * inter[0:HEAD_DIM, :] + _dot(vts[d, h], s_t.astype(BF16))
            den = w_inters[d, h] * inter[HEAD_DIM:HEAD_DIM + 1, :] + jnp.sum(s_t, axis=0, keepdims=True)
            out_s = htf_s if d == 0 else htb_s
            out_s[chunk[d], h * HEAD_DIM:(h + 1) * HEAD_DIM, :] = num / jnp.maximum(jnp.abs(den), floors[d, h])
        for d, h in units:
            c_s[d * N_HEADS + h] = keeps[d, h] * cexts[d, h] + _dot(sources[d, h], ks[d, h])

    lax.fori_loop(0, N_CHUNKS, chunk_body, 0, unroll=2)

    gain = g_ref[...]

    def out_body(c, _):
        r0 = pl.multiple_of(c * CHUNK, CHUNK)
        h_t = htf_s[c] + htb_s[c]
        normed = []
        for h in range(N_HEADS):
            x = h_t[h * HEAD_DIM:(h + 1) * HEAD_DIM, :]
            normed.append(x * lax.rsqrt(jnp.mean(x * x, axis=0, keepdims=True) + NORM_EPS))
        y = jnp.concatenate(normed, axis=0).T * gain
        o_ref[0, pl.ds(r0, CHUNK), :] = jax.nn.sigmoid(og_ref[0, pl.ds(r0, CHUNK), :]) * y
        return 0

    lax.fori_loop(0, N_CHUNKS, out_body, 0)


def _ml_mixer(qkv, og, sm, i_bias, f_bias, norm_g):
    b = qkv.shape[0]
    bias16 = jnp.concatenate([i_bias.reshape(8), f_bias.reshape(8)])
    bcol = jnp.tile(bias16.reshape(16, 1), (N_CHUNKS, 1))
    full = lambda shape: pl.BlockSpec(shape, lambda i: (0,) * len(shape))
    return pl.pallas_call(
        _ml_kernel,
        grid=(b,),
        in_specs=[pl.BlockSpec((1, SEQ, 3 * GROUP_W), lambda i: (i, 0, QKV_ML_BLOCK)),
                  pl.BlockSpec((1, SEQ, GROUP_W), lambda i: (i, 0, WIDE_MLO_BLOCK)),
                  pl.BlockSpec((1, SEQ, CHUNK), lambda i: (i, 0, WIDE_SM_BLOCK)),
                  full((N_CHUNKS * 16, 1)), full((1, GROUP_W))],
        out_specs=pl.BlockSpec((1, SEQ, GROUP_W), lambda i: (i, 0, 0)),
        out_shape=jax.ShapeDtypeStruct((b, SEQ, GROUP_W), F32),
        scratch_shapes=[pltpu.VMEM((SEQ, CHUNK), F32), pltpu.VMEM((N_CHUNKS, 16, CHUNK), F32),
                        pltpu.VMEM((N_CHUNKS, 16, CHUNK), F32), pltpu.VMEM((N_CHUNKS, GROUP_W, CHUNK), BF16),
                        pltpu.VMEM((N_CHUNKS, GROUP_W, CHUNK), BF16), pltpu.VMEM((N_CHUNKS, GROUP_W, CHUNK), F32),
                        pltpu.VMEM((N_CHUNKS, GROUP_W, CHUNK), F32),
                        pltpu.VMEM((2 * N_HEADS, ML_STATE_ROWS, HEAD_DIM), F32),
                        pltpu.VMEM((2 * N_HEADS, 8, CHUNK), F32)],
        compiler_params=_params("arbitrary"),
        name="mlstm_mixer",
    )(qkv, og, sm, bcol, norm_g.reshape(1, GROUP_W))


HALF = SEQ // 2


def _dft_tables():
    f = np.arange(HALF, dtype=np.int64)[:, None]
    t = np.arange(SEQ, dtype=np.int64)[None, :]
    ang = (2.0 * np.pi / FFT_N) * ((f * t) % FFT_N).astype(np.float64)
    wc = np.cos(ang)
    ws = np.sin(ang)
    ws[0, :] = np.array([1.0, 1.0, -1.0, -1.0])[np.arange(SEQ) % 4]
    tables = []
    for parity in range(2):
        w = np.stack([wc[:, parity::2], ws[:, parity::2]]).reshape(2, N_FREQ_BLOCKS, FREQ_BLOCK, HALF)
        w = np.ascontiguousarray(w.transpose(1, 0, 2, 3).reshape(2 * HALF, HALF)).astype(np.float32)
        tables.append((jnp.asarray(w).astype(BF16), jnp.asarray(np.ascontiguousarray(w.T)).astype(BF16)))
    return tables


def _hy_features():
    t = jnp.linspace(0.0, 1.0, SEQ, dtype=F32)[:, None]
    bands = (HY_EMB - 1) // 2
    f = jnp.linspace(1e-4, bands - 1, bands, dtype=F32)
    ang = (2.0 * math.pi) * (jnp.arange(SEQ, dtype=F32) / SEQ)[:, None] * f[None, :]
    feats = jnp.concatenate([t, jnp.cos(ang), -jnp.sin(ang)], axis=-1)
    deltas = jnp.abs(jnp.linspace(math.log(1e-2) / 1.5, math.log(1e-2) / 0.3, GROUP_W, dtype=F32))
    return jnp.pad(feats, ((0, 0), (0, CHUNK - HY_EMB))), t, deltas.reshape(1, GROUP_W)


def _hy_filter_kernel(feats_ref, w1_ref, b1_ref, w2_ref, b2_ref, w3_ref, freq_ref, t_ref, dl_ref, hp_ref, hm_ref):
    h = jnp.sin(freq_ref[0:1, :] * (_dot(feats_ref[...].astype(BF16), w1_ref[...].astype(BF16)) + b1_ref[...]))
    h = jnp.sin(freq_ref[1:2, :] * (_dot(h.astype(BF16), w2_ref[...].astype(BF16)) + b2_ref[...]))
    h = _dot(h.astype(BF16), w3_ref[...].astype(BF16))
    win = jnp.exp(-t_ref[...] * dl_ref[...])
    row = lax.broadcasted_iota(jnp.int32, (SEQ, 1), 0)
    for o in range(2):
        hf = h[:, (2 * o) * GROUP_W:(2 * o + 1) * GROUP_W] * win
        hb = h[:, (2 * o + 1) * GROUP_W:(2 * o + 2) * GROUP_W] * win
        r = lax.rsqrt(jnp.sum(hf * hf + hb * hb, axis=0, keepdims=True) + NORM_EPS)
        hf = hf * r
        hb = jnp.where(row == 0, 0.0, hb * r)
        hp_ref[:, o * GROUP_W:(o + 1) * GROUP_W] = (hf + hb).astype(BF16)
        hm_ref[:, o * GROUP_W:(o + 1) * GROUP_W] = (hf - hb).astype(BF16)


def _hy_spec_kernel(we_ref, wo_ref, hp_ref, hm_ref, kc_ref, ks_ref, kcp_ref, ksp_ref):
    fb = FREQ_BLOCK
    nc = 2 * GROUP_W
    row0 = (lax.broadcasted_iota(jnp.int32, (fb, 1), 0) + pl.program_id(0) * fb) == 0
    edge = jnp.where(row0, 1.0 / FFT_N, 2.0 / FFT_N)
    pe = _dot(we_ref[0:fb, :], hp_ref[:, 0:nc])
    po = _dot(wo_ref[0:fb, :], hp_ref[:, nc:])
    me = _dot(we_ref[fb:, :], hm_ref[:, 0:nc])
    mo = _dot(wo_ref[fb:, :], hm_ref[:, nc:])
    mid_cos = _dot(we_ref[fb:fb + 8, :], hp_ref[:, 0:nc])[0:1, :]
    kc_ref[...] = (pe + po) * edge
    kcp_ref[...] = (pe - po) * edge
    ks_ref[...] = jnp.where(row0, mid_cos, me + mo) * (2.0 / FFT_N)
    ksp_ref[...] = jnp.where(row0, mo, mo - me) * (2.0 / FFT_N)


def _hy_spectrum(tables, w1, b1, w2, b2, w3, freq):
    feats, t, deltas = _hy_features()
    w1p = jnp.pad(w1, ((0, CHUNK - HY_EMB), (0, 0)))
    hp, hm = pl.pallas_call(
        _hy_filter_kernel,
        out_shape=[jax.ShapeDtypeStruct((SEQ, 2 * GROUP_W), BF16)] * 2,
        compiler_params=pltpu.CompilerParams(vmem_limit_bytes=VMEM_LIMIT),
        name="hyena_filter",
    )(feats, w1p, b1.reshape(1, HY_FFN), w2, b2.reshape(1, HY_FFN), w3, freq, t, deltas)
    split = lambda h: h.reshape(HALF, 4 * GROUP_W)
    table_spec = pl.BlockSpec((2 * FREQ_BLOCK, HALF), lambda j: (j, 0))
    return pl.pallas_call(
        _hy_spec_kernel,
        grid=(N_FREQ_BLOCKS,),
        in_specs=[table_spec, table_spec,
                  pl.BlockSpec((HALF, 4 * GROUP_W), lambda j: (0, 0)),
                  pl.BlockSpec((HALF, 4 * GROUP_W), lambda j: (0, 0))],
        out_specs=[pl.BlockSpec((FREQ_BLOCK, 2 * GROUP_W), lambda j: (j, 0))] * 4,
        out_shape=[jax.ShapeDtypeStruct((HALF, 2 * GROUP_W), F32)] * 4,
        compiler_params=_params("arbitrary"),
        name="hyena_spectrum",
    )(tables[0][0], tables[1][0], split(hp), split(hm))


def _split_conv_rows(e_ref, o_ref, b, r0, w, bias):
    e = e_ref[b, pl.ds(r0, CHUNK), :]
    o = o_ref[b, pl.ds(r0, CHUNK), :]
    o_prev = o_ref[b, pl.ds(jnp.maximum(r0 - 1, 0), 1), :]
    e_next = e_ref[b, pl.ds(jnp.minimum(r0 + CHUNK, HALF - 1), 1), :]
    o_prev = jnp.where(r0 > 0, o_prev, 0.0)
    e_next = jnp.where(r0 + CHUNK < HALF, e_next, 0.0)
    row = lax.broadcasted_iota(jnp.int32, (CHUNK, 1), 0)
    o_dn = jnp.where(row == 0, o_prev, pltpu.roll(o, 1, 0))
    e_up = jnp.where(row == CHUNK - 1, e_next, pltpu.roll(e, CHUNK - 1, 0))
    conv_e = w[0:1, :] * o_dn + w[1:2, :] * e + w[2:3, :] * o + bias
    conv_o = w[0:1, :] * e + w[1:2, :] * o + w[2:3, :] * e_up + bias
    return conv_e, conv_o


def _hy_conv_kernel(ze_ref, zo_ref, ge_ref, go_ref, cwz_ref, cbz_ref, cwg_ref, cbg_ref, we_ref, wo_ref, wet_ref, wot_ref,
                    kc_ref, ks_ref, kcp_ref, ksp_ref, skip_ref, g_ref, o_ref, ze_s, zo_s, acce_s, acco_s, *,
                    conv_z, final_norm):
    j = pl.program_id(1)
    n_b = ze_ref.shape[0]
    fb = FREQ_BLOCK
    half_chunks = HALF // CHUNK

    def z_rows(b, r0):
        if conv_z:
            return _split_conv_rows(ze_ref, zo_ref, b, r0, cwz_ref[...], cbz_ref[...])
        return ze_ref[b, pl.ds(r0, CHUNK), :], zo_ref[b, pl.ds(r0, CHUNK), :]

    @pl.when(j == 0)
    def _():
        for b in range(n_b):
            def fill(c, _, b=b):
                r0 = pl.multiple_of(c * CHUNK, CHUNK)
                z_e, z_o = z_rows(b, r0)
                ze_s[pl.ds(r0, CHUNK), b * GROUP_W:(b + 1) * GROUP_W] = z_e.astype(BF16)
                zo_s[pl.ds(r0, CHUNK), b * GROUP_W:(b + 1) * GROUP_W] = z_o.astype(BF16)
                return 0

            lax.fori_loop(0, half_chunks, fill, 0)
        acce_s[...] = jnp.zeros_like(acce_s)
        acco_s[...] = jnp.zeros_like(acco_s)

    row0 = (lax.broadcasted_iota(jnp.int32, (fb, 1), 0) + j * fb) == 0
    tiled = lambda ref: jnp.concatenate([ref[...]] * n_b, axis=1)
    kc, ks, kcp, ksp = tiled(kc_ref), tiled(ks_ref), tiled(kcp_ref), tiled(ksp_ref)
    xe = _dot(we_ref[...], ze_s[...])
    xo = _dot(wo_ref[...], zo_s[...])
    ae, be, ao, bo = xe[0:fb, :], xe[fb:, :], xo[0:fb, :], xo[fb:, :]
    xc, xcp, xs, xsp = ae + ao, ae - ao, be + bo, bo - be
    yc = xc * kc - xs * ks
    ys = xc * ks + xs * kc
    ycp = xcp * kcp - xsp * ksp
    ysp = xcp * ksp + xsp * kcp
    dc, nyq = xc * kc, xcp * kcp
    pc = jnp.where(row0, dc + nyq, yc + ycp)
    mc = jnp.where(row0, dc - nyq, yc - ycp)
    ms = jnp.where(row0, be * ks - bo * ksp, ys - ysp)
    ps = jnp.where(row0, be * ksp + bo * ks, ys + ysp)
    acce_s[...] += _dot(wet_ref[...], jnp.concatenate([pc, ms], axis=0).astype(BF16))
    acco_s[...] += _dot(wot_ref[...], jnp.concatenate([mc, ps], axis=0).astype(BF16))

    @pl.when(j == pl.num_programs(1) - 1)
    def _():
        for b in range(n_b):
            def finish(c, _, b=b):
                r0 = pl.multiple_of(c * CHUNK, CHUNK)
                gates = _split_conv_rows(ge_ref, go_ref, b, r0, cwg_ref[...], cbg_ref[...])
                for parity, (gate, z, acc) in enumerate(zip(gates, z_rows(b, r0), (acce_s, acco_s))):
                    r = gate * (acc[pl.ds(r0, CHUNK), b * GROUP_W:(b + 1) * GROUP_W] + skip_ref[...] * z)
                    if final_norm:
                        r = _rms(r) * g_ref[...]
                    o_ref[b, pl.ds(r0, CHUNK), parity * GROUP_W:(parity + 1) * GROUP_W] = r
                return 0

            lax.fori_loop(0, half_chunks, finish, 0)


def _hy_long_conv(z, z_cols, gates, gate_cols, conv_w, conv_b, conv_z, tables, spectrum, order, skip, norm_g,
                  final_norm):
    b = z.shape[0]
    bg = HY_BATCH_BLOCK
    taps = lambda col: (conv_w[:, col * GROUP_W:(col + 1) * GROUP_W],
                        conv_b[col * GROUP_W:(col + 1) * GROUP_W].reshape(1, GROUP_W))
    small = lambda shape: pl.BlockSpec(shape, lambda g, j: (0, 0))
    act = lambda col: pl.BlockSpec((bg, HALF, GROUP_W), lambda g, j: (g, 0, col))
    fwd = pl.BlockSpec((2 * FREQ_BLOCK, HALF), lambda g, j: (j, 0))
    inv = pl.BlockSpec((HALF, 2 * FREQ_BLOCK), lambda g, j: (0, j))
    spec = pl.BlockSpec((FREQ_BLOCK, GROUP_W), lambda g, j: (j, order))
    (we, wet), (wo, wot) = tables
    return pl.pallas_call(
        functools.partial(_hy_conv_kernel, conv_z=conv_z, final_norm=final_norm),
        grid=(b // bg, N_FREQ_BLOCKS),
        in_specs=[act(z_cols[0]), act(z_cols[1]), act(gate_cols[0]), act(gate_cols[1]),
                  small((3, GROUP_W)), small((1, GROUP_W)), small((3, GROUP_W)), small((1, GROUP_W)),
                  fwd, fwd, inv, inv, spec, spec, spec, spec, small((1, GROUP_W)), small((1, GROUP_W))],
        out_specs=pl.BlockSpec((bg, HALF, 2 * GROUP_W), lambda g, j: (g, 0, 0)),
        out_shape=jax.ShapeDtypeStruct((b, HALF, 2 * GROUP_W), F32),
        scratch_shapes=[pltpu.VMEM((HALF, bg * GROUP_W), BF16), pltpu.VMEM((HALF, bg * GROUP_W), BF16),
                        pltpu.VMEM((HALF, bg * GROUP_W), F32), pltpu.VMEM((HALF, bg * GROUP_W), F32)],
        compiler_params=_params("arbitrary", "arbitrary"),
        name="hyena_long_conv",
    )(z, z, gates, gates, *taps(0), *taps(order + 1), we, wo, wet, wot, *spectrum,
      skip.reshape(1, GROUP_W), norm_g.reshape(1, GROUP_W))


def _hy_mixer(u, tables, conv_w, conv_b, w1, b1, w2, b2, w3, freq, skip, norm_g):
    spectrum = _hy_spectrum(tables, w1, b1, w2, b2, w3, freq)
    z1 = _hy_long_conv(u, (0, 3), u, (1, 4), conv_w, conv_b, True, tables, spectrum, 0, skip[0], norm_g, False)
    return _hy_long_conv(z1, (0, 1), u, (2, 5), conv_w, conv_b, False, tables, spectrum, 1, skip[1], norm_g, True)


FFN_SPLITS = ((0, 1024), (1024, 2048), (2048, FFN_HIDDEN))


def _ffn_kernel(x_ref, mod_ref, ya_ref, yb_ref, yc_ref, yd_ref, perm_ref, wo_ref, wg_ref, wu_ref, wd_ref, gf_ref, o_ref,
                *, final_norm):
    yd = yd_ref[...]
    yd = jnp.concatenate([yd[:, 0:GROUP_W], yd[:, GROUP_W:]], axis=0).astype(BF16)
    yd = _dot(perm_ref[...], yd)
    y_mix = jnp.concatenate([ya_ref[...], yb_ref[...], yc_ref[...], yd], axis=1).astype(BF16)
    x = x_ref[...] + mod_ref[0, 2:3, :] * _dot(y_mix, wo_ref[...])
    h = (_rms(x) * (1.0 + mod_ref[0, 4:5, :]) + mod_ref[0, 3:4, :]).astype(BF16)
    acc = jnp.zeros_like(x)
    for lo, hi in FFN_SPLITS:
        a = _silu(_dot(h, wg_ref[:, lo:hi])) * _dot(h, wu_ref[:, lo:hi])
        acc = acc + _dot(a.astype(BF16), wd_ref[lo:hi, :])
    y = x + mod_ref[0, 5:6, :] * acc
    if final_norm:
        y = _rms(y) * gf_ref[...]
    o_ref[...] = y


def _out_ffn(x2, mod_l, ys, w_out, wg, wu, wd, layer, final_g, final_norm):
    n_tok, d = x2.shape
    tm = TOKEN_TILE
    tiles_per_seq = SEQ // tm
    tok = np.arange(tm)
    perm = (np.arange(tm)[None, :] == (tok // 2 + (tok % 2) * (tm // 2))[:, None]).astype(np.float32)
    resident = lambda shape: pl.BlockSpec((None,) + shape, lambda i: (layer, 0, 0), pipeline_mode=pl.Buffered(1))
    return pl.pallas_call(
        functools.partial(_ffn_kernel, final_norm=final_norm),
        grid=(n_tok // tm,),
        in_specs=[pl.BlockSpec((tm, d), lambda i: (i, 0)),
                  pl.BlockSpec((1, 6, d), lambda i: (i // tiles_per_seq, 0, 0))]
                 + [pl.BlockSpec((tm, GROUP_W), lambda i: (i, 0))] * 3
                 + [pl.BlockSpec((tm // 2, 2 * GROUP_W), lambda i: (i, 0)),
                    pl.BlockSpec((tm, tm), lambda i: (0, 0), pipeline_mode=pl.Buffered(1)),
                    resident((d, d)), resident((d, FFN_HIDDEN)), resident((d, FFN_HIDDEN)), resident((FFN_HIDDEN, d)),
                    pl.BlockSpec((1, d), lambda i: (0, 0))],
        out_specs=pl.BlockSpec((tm, d), lambda i: (i, 0)),
        out_shape=jax.ShapeDtypeStruct((n_tok, d), F32),
        compiler_params=_params("arbitrary"),
        name="out_proj_swiglu",
    )(x2, mod_l, *ys, jnp.asarray(perm).astype(BF16), w_out, wg, wu, wd, final_g.reshape(1, d))


def _permute_w_in(w):
    w = w.astype(BF16)
    cols = [w[..., 0:1024], w[..., 2568:2824], w[..., 1024:1032], w[..., 2824:2840],
            jnp.zeros(w.shape[:-1] + (P_PAD - 3608,), BF16), w[..., 1032:1800], w[..., 1800:2568], w[..., 2840:3608]]
    return jnp.concatenate(cols, axis=-1)


def kernel(x, c, mod_w, mod_b, w_in, ssd_conv_w, ssd_conv_b, ssd_dt_bias, ssd_a_log, ssd_d, ssd_norm_g, na_rpb, na_norm_g, ml_i_bias, ml_f_bias, ml_norm_g, hy_conv_w, hy_conv_b, hy_w1, hy_b1, hy_w2, hy_b2, hy_w3, hy_freq, hy_skip, hy_norm_g, w_out, ffn_w_gate, ffn_w_up, ffn_w_down, final_norm_g):
    b, seq, d = x.shape
    assert seq == SEQ and d == D_MODEL and b % HY_BATCH_BLOCK == 0
    depth = mod_w.shape[0]
    mod = _modulation(c, mod_w, mod_b).reshape(depth, b, 6, d)
    dft = _dft_tables()
    w_in_p = _permute_w_in(w_in)
    w_out_b, w_gate_b, w_up_b, w_down_b = (w.astype(BF16) for w in (w_out, ffn_w_gate, ffn_w_up, ffn_w_down))
    x2 = x.reshape(b * seq, d)
    for l in range(depth):
        wide, qkv, hy = _in_proj(x2, mod[l], w_in_p, l)
        wide = wide.reshape(b, seq, WIDE_W)
        qkv = qkv.reshape(b, seq, QKV_W)
        y_ssd = _ssd_mixer(wide, wide, ssd_conv_w[l], ssd_conv_b[l], ssd_dt_bias[l], ssd_a_log[l], ssd_d[l],
                           ssd_norm_g[l])
        y_na = _na_mixer(qkv, na_rpb[l], na_norm_g[l])
        y_ml = _ml_mixer(qkv, wide, wide, ml_i_bias[l], ml_f_bias[l], ml_norm_g[l])
        y_hy = _hy_mixer(hy.reshape(b, HALF, 2 * HY_W), dft, hy_conv_w[l], hy_conv_b[l], hy_w1[l], hy_b1[l], hy_w2[l],
                         hy_b2[l], hy_w3[l], hy_freq[l], hy_skip[l], hy_norm_g[l])
        ys = [t.reshape(b * seq, GROUP_W) for t in (y_ssd, y_na, y_ml)] + [y_hy.reshape(b * HALF, 2 * GROUP_W)]
        x2 = _out_ffn(x2, mod[l], ys, w_out_b, w_gate_b, w_up_b, w_down_b, l, final_norm_g, l == depth - 1)
    return x2.reshape(b, seq, d)
```

```python
import functools
import math

import numpy as np
import jax
import jax.numpy as jnp
from jax import lax
from jax.experimental import pallas as pl
from jax.experimental.pallas import tpu as pltpu

F32 = jnp.float32
BF16 = jnp.bfloat16

D_MODEL = 1024
SEQ = 2048
GRID_W = 64
GROUP_W = 256
HEAD_DIM = 64
N_HEADS = 4
SSD_STATE = 128
SSD_XBC = 768
CHUNK = 128
N_CHUNKS = SEQ // CHUNK
NA_KH = 8
NA_KW = 16
NA_ROW_GROUP = 2
HY_EMB = 33
HY_FFN = 64
FFN_HIDDEN = 2816
NORM_EPS = 1e-6
FFT_N = 2 * SEQ
FREQ_SUB = 128
FREQ_BLOCK = 256
N_FREQ_BLOCKS = SEQ // 2 // FREQ_BLOCK
HY_BATCH_BLOCK = 2
TOKEN_TILE = 512
IN_PROJ_TILE = 1024
VMEM_LIMIT = 56 * 1024 * 1024

WIDE_W = 1408
WIDE_ZX_BLOCK = 0
WIDE_MLO_BLOCK = 4
WIDE_SM_BLOCK = 10
QKV_W = 1536
QKV_NA_BLOCK = 0
QKV_ML_BLOCK = 1
HY_W = 768
P_PAD = WIDE_W + QKV_W + HY_W
SM_DT = 0
SM_IG = 8
SM_FG = 16

NT_DIMS = (((1,), (1,)), ((), ()))
TN_DIMS = (((0,), (0,)), ((), ()))


def _rms(x):
    return x * lax.rsqrt(jnp.mean(x * x, axis=-1, keepdims=True) + NORM_EPS)


def _silu(x):
    half = 0.5 * x
    return half + half * jnp.tanh(half)


def _dot(a, b):
    return jnp.dot(a, b, preferred_element_type=F32)


def _params(*sem):
    return pltpu.CompilerParams(dimension_semantics=sem, vmem_limit_bytes=VMEM_LIMIT)


def _mod_kernel(c_ref, w_ref, b_ref, o_ref):
    cond = _silu(c_ref[...]).astype(BF16)
    o_ref[0] = _dot(cond, w_ref[0].astype(BF16)) + b_ref[0]


def _modulation(c, mod_w, mod_b):
    depth, d, n = mod_w.shape
    b = c.shape[0]
    wn = n // 2
    return pl.pallas_call(
        _mod_kernel,
        grid=(depth, n // wn),
        in_specs=[pl.BlockSpec((b, d), lambda l, j: (0, 0)),
                  pl.BlockSpec((1, d, wn), lambda l, j: (l, 0, j)),
                  pl.BlockSpec((1, 1, wn), lambda l, j: (l, 0, j))],
        out_specs=pl.BlockSpec((1, b, wn), lambda l, j: (l, 0, j)),
        out_shape=jax.ShapeDtypeStruct((depth, b, n), F32),
        compiler_params=_params("arbitrary", "arbitrary"),
        name="adaln_mod",
    )(c, mod_w, mod_b.reshape(depth, 1, n))


def _in_proj_kernel(x_ref, mod_ref, w_ref, wide_ref, qkv_ref, hy_ref, cols_s):
    h = (_rms(x_ref[...]) * (1.0 + mod_ref[0, 1:2, :]) + mod_ref[0, 0:1, :]).astype(BF16)
    wide_ref[...] = _dot(h, w_ref[:, 0:WIDE_W])
    qkv_ref[...] = _dot(h, w_ref[:, WIDE_W:WIDE_W + QKV_W]).astype(BF16)
    half = x_ref.shape[0] // 2
    n_slabs = cols_s.shape[0]
    for cb in range(n_slabs):
        cols_s[cb] = h[:, cb * CHUNK:(cb + 1) * CHUNK].astype(F32)
    w_hy = w_ref[:, WIDE_W + QKV_W:P_PAD]
    for parity in range(2):
        rows = jnp.concatenate([cols_s[cb, pl.ds(parity, half, stride=2), :] for cb in range(n_slabs)], axis=1)
        hy_ref[:, parity * HY_W:(parity + 1) * HY_W] = _dot(rows.astype(BF16), w_hy)


def _in_proj(x2, mod_l, w_perm, layer):
    n_tok, d = x2.shape
    tm = IN_PROJ_TILE
    tiles_per_seq = SEQ // tm
    return pl.pallas_call(
        _in_proj_kernel,
        grid=(n_tok // tm,),
        in_specs=[pl.BlockSpec((tm, d), lambda i: (i, 0)),
                  pl.BlockSpec((1, 6, d), lambda i: (i // tiles_per_seq, 0, 0)),
                  pl.BlockSpec((None, d, P_PAD), lambda i: (layer, 0, 0), pipeline_mode=pl.Buffered(1))],
        out_specs=[pl.BlockSpec((tm, WIDE_W), lambda i: (i, 0)), pl.BlockSpec((tm, QKV_W), lambda i: (i, 0)),
                   pl.BlockSpec((tm // 2, 2 * HY_W), lambda i: (i, 0))],
        out_shape=[jax.ShapeDtypeStruct((n_tok, WIDE_W), F32), jax.ShapeDtypeStruct((n_tok, QKV_W), BF16),
                   jax.ShapeDtypeStruct((n_tok // 2, 2 * HY_W), F32)],
        scratch_shapes=[pltpu.VMEM((d // CHUNK, tm, CHUNK), F32)],
        compiler_params=_params("arbitrary"),
        name="in_proj",
    )(x2, mod_l, w_perm)


def _dwconv3_rows(src_ref, b, r0, cols, w, bias):
    x = src_ref[b, pl.ds(r0, CHUNK), cols]
    up = src_ref[b, pl.ds(jnp.maximum(r0 - 1, 0), 1), cols]
    dn = src_ref[b, pl.ds(jnp.minimum(r0 + CHUNK, SEQ - 1), 1), cols]
    up = jnp.where(r0 > 0, up, 0.0)
    dn = jnp.where(r0 + CHUNK < SEQ, dn, 0.0)
    row = lax.broadcasted_iota(jnp.int32, (CHUNK, 1), 0)
    prev = jnp.where(row == 0, up, pltpu.roll(x, 1, 0))
    nxt = jnp.where(row == CHUNK - 1, dn, pltpu.roll(x, CHUNK - 1, 0))
    return w[0:1, :] * prev + w[1:2, :] * x + w[2:3, :] * nxt + bias


def _chunk_scans(a, axis):
    n = a.shape[axis]
    shape = [1, 1]
    shape[axis] = n
    pos = lax.broadcasted_iota(jnp.int32, tuple(shape), axis) % CHUNK
    pre, suf = a, a
    k = 1
    while k < CHUNK:
        pre = pre + jnp.where(pos >= k, pltpu.roll(pre, k, axis), 0.0)
        suf = suf + jnp.where(pos < CHUNK - k, pltpu.roll(suf, n - k, axis), 0.0)
        k *= 2
    return pre, suf


def _gate_rows(sm_ref, rows_s, lo, n):
    for c in range(N_CHUNKS):
        t = sm_ref[0, c * CHUNK:(c + 1) * CHUNK, :].T
        rows_s[c] = t[lo:lo + n, :]


def _ssd_kernel(zx_ref, sm_ref, cw_ref, cb_ref, dtb_col_ref, a_col_ref, dsk_ref, g_ref,
                o_ref, xs_s, b_s, xst_s, ct_s, acol_s, rows_s, dtrow_s, arow_s, ytf_s, ytb_s, st_s):
    cw = cw_ref[...]
    cb = cb_ref[...]

    def conv_body(c, _):
        r0 = pl.multiple_of(c * CHUNK, CHUNK)
        v = _silu(_dwconv3_rows(zx_ref, 0, r0, slice(GROUP_W, GROUP_W + SSD_XBC), cw, cb))
        xs_s[pl.ds(r0, CHUNK), :] = v[:, 0:GROUP_W]
        b_s[pl.ds(r0, CHUNK), :] = v[:, GROUP_W:2 * GROUP_W].astype(BF16)
        xst_s[c] = v[:, 0:GROUP_W].T
        ct_s[c] = v[:, 2 * GROUP_W:3 * GROUP_W].T.astype(BF16)
        return 0

    lax.fori_loop(0, N_CHUNKS, conv_body, 0)

    _gate_rows(sm_ref, rows_s, SM_DT, 8)
    dt_row = jax.nn.softplus(rows_s[...].reshape(N_CHUNKS * 8, CHUNK) + dtb_col_ref[...])
    pre, suf = _chunk_scans(dt_row * (-jnp.exp(a_col_ref[...])), 1)
    rid = lax.broadcasted_iota(jnp.int32, (N_CHUNKS * 8, 1), 0) % 8
    dtrow_s[...] = dt_row.reshape(N_CHUNKS, 8, CHUNK)
    arow_s[...] = jnp.where(rid < N_HEADS, pre, suf).reshape(N_CHUNKS, 8, CHUNK)
    pad_rows = jnp.zeros((CHUNK - 8, CHUNK), F32)
    for c in range(N_CHUNKS):
        acol_s[c * CHUNK:(c + 1) * CHUNK, :] = jnp.concatenate([arow_s[c], pad_rows], axis=0).T

    si = lax.broadcasted_iota(jnp.int32, (CHUNK, CHUNK), 0)
    ti = lax.broadcasted_iota(jnp.int32, (CHUNK, CHUNK), 1)
    first = lax.broadcasted_iota(jnp.int32, (2 * HEAD_DIM, 1), 0) < HEAD_DIM
    st_s[...] = jnp.zeros_like(st_s)

    def chunk_body(i, _):
        units = [(d, g) for d in range(2) for g in range(2)]
        chunk = {0: i, 1: N_CHUNKS - 1 - i}
        rows = {d: pl.multiple_of(chunk[d] * CHUNK, CHUNK) for d in range(2)}
        bgs, cgs, prevs, scores, y_offs = {}, {}, {}, {}, {}
        for d, g in units:
            bgs[d, g] = b_s[pl.ds(rows[d], CHUNK), g * SSD_STATE:(g + 1) * SSD_STATE]
            cgs[d, g] = ct_s[chunk[d], g * SSD_STATE:(g + 1) * SSD_STATE, :]
            prevs[d, g] = st_s[d, g]
            scores[d, g] = _dot(bgs[d, g], cgs[d, g])
            y_offs[d, g] = _dot(prevs[d, g].astype(BF16), cgs[d, g])
        weighted, xdts, x_ends, carries, grow = {}, {}, {}, {}, {}
        for d, g in units:
            mask = (si <= ti) if d == 0 else (si >= ti)
            end = CHUNK - 1 if d == 0 else 0
            acol = acol_s[pl.ds(rows[d], CHUNK), :]
            arow = arow_s[chunk[d]]
            dtr = dtrow_s[chunk[d]]
            for hh in range(2):
                h = 2 * g + hh
                j = d * N_HEADS + h
                ar = arow[j:j + 1, :]
                a_end = ar[:, end:end + 1]
                decay = jnp.exp(jnp.where(mask, ar - acol[:, j:j + 1], -jnp.inf))
                xdt_t = xst_s[chunk[d], h * HEAD_DIM:(h + 1) * HEAD_DIM, :] * dtr[j:j + 1, :]
                weighted[d, h] = (scores[d, g] * decay).astype(BF16)
                xdts[d, h] = xdt_t.astype(BF16)
                x_ends[d, h] = (xdt_t * jnp.exp(a_end - ar)).astype(BF16)
                carries[d, h] = jnp.exp(a_end)
                grow[d, h] = jnp.exp(ar)
        for d in range(2):
            ys = []
            for h in range(N_HEADS):
                g, hh = divmod(h, 2)
                y_off = y_offs[d, g][hh * HEAD_DIM:(hh + 1) * HEAD_DIM, :]
                ys.append(_dot(xdts[d, h], weighted[d, h]) + y_off * grow[d, h])
            y_t = jnp.concatenate(ys, axis=0)
            if d == 0:
                ytf_s[chunk[d]] = y_t
            else:
                ytb_s[chunk[d]] = y_t
        for d, g in units:
            keep = jnp.where(first, carries[d, 2 * g], carries[d, 2 * g + 1])
            x_end = jnp.concatenate([x_ends[d, 2 * g], x_ends[d, 2 * g + 1]], axis=0)
            st_s[d, g] = prevs[d, g] * keep + _dot(x_end, bgs[d, g])
        return 0

    lax.fori_loop(0, N_CHUNKS, chunk_body, 0, unroll=4)

    dsk = dsk_ref[...]
    gain = g_ref[...]

    def out_body(c, _):
        r0 = pl.multiple_of(c * CHUNK, CHUNK)
        y = (ytf_s[c] + ytb_s[c]).T + dsk * xs_s[pl.ds(r0, CHUNK), :]
        y = y * _silu(zx_ref[0, pl.ds(r0, CHUNK), 0:GROUP_W])
        o_ref[0, pl.ds(r0, CHUNK), :] = _rms(y) * gain
        return 0

    lax.fori_loop(0, N_CHUNKS, out_body, 0)


def _ssd_mixer(zx, sm, conv_w, conv_b, dt_bias, a_log, d_skip, norm_g):
    b = zx.shape[0]
    tile_col = lambda v: jnp.tile(v.reshape(8, 1), (N_CHUNKS, 1))
    full = lambda shape: pl.BlockSpec(shape, lambda i: (0,) * len(shape))
    return pl.pallas_call(
        _ssd_kernel,
        grid=(b,),
        in_specs=[pl.BlockSpec((1, SEQ, 1024), lambda i: (i, 0, WIDE_ZX_BLOCK)),
                  pl.BlockSpec((1, SEQ, CHUNK), lambda i: (i, 0, WIDE_SM_BLOCK)),
                  full((3, SSD_XBC)), full((1, SSD_XBC)),
                  full((N_CHUNKS * 8, 1)), full((N_CHUNKS * 8, 1)), full((1, GROUP_W)), full((1, GROUP_W))],
        out_specs=pl.BlockSpec((1, SEQ, GROUP_W), lambda i: (i, 0, 0)),
        out_shape=jax.ShapeDtypeStruct((b, SEQ, GROUP_W), F32),
        scratch_shapes=[pltpu.VMEM((SEQ, GROUP_W), F32), pltpu.VMEM((SEQ, GROUP_W), BF16),
                        pltpu.VMEM((N_CHUNKS, GROUP_W, CHUNK), F32), pltpu.VMEM((N_CHUNKS, GROUP_W, CHUNK), BF16),
                        pltpu.VMEM((SEQ, CHUNK), F32), pltpu.VMEM((N_CHUNKS, 8, CHUNK), F32),
                        pltpu.VMEM((N_CHUNKS, 8, CHUNK), F32), pltpu.VMEM((N_CHUNKS, 8, CHUNK), F32),
                        pltpu.VMEM((N_CHUNKS, GROUP_W, CHUNK), F32), pltpu.VMEM((N_CHUNKS, GROUP_W, CHUNK), F32),
                        pltpu.VMEM((2, 2, 2 * HEAD_DIM, SSD_STATE), F32)],
        compiler_params=_params("arbitrary"),
        name="ssd_mixer",
    )(zx, sm, conv_w, conv_b.reshape(1, SSD_XBC), tile_col(dt_bias), tile_col(a_log),
      jnp.repeat(d_skip, HEAD_DIM).reshape(1, GROUP_W), norm_g.reshape(1, GROUP_W))


def _na_kernel(qkv_ref, bias_ref, g_ref, o_ref):
    n_rows = SEQ // GRID_W
    k_tok = NA_KH * GRID_W
    pair_w = 2 * HEAD_DIM
    first = lax.broadcasted_iota(jnp.int32, (1, pair_w), 1) < HEAD_DIM
    gain = g_ref[...]

    def row_group(g, _):
        units = [(i, pair) for i in range(NA_ROW_GROUP) for pair in range(N_HEADS // 2)]
        q0, k0, dr0, scores, values = {}, {}, {}, {}, {}
        for i in range(NA_ROW_GROUP):
            r = g * NA_ROW_GROUP + i
            start = jnp.clip(r - NA_KH // 2, 0, n_rows - NA_KH)
            dr0[i] = start - r + NA_KH - 1
            q0[i] = pl.multiple_of(r * GRID_W, GRID_W)
            k0[i] = pl.multiple_of(start * GRID_W, GRID_W)
        for i, pair in units:
            lo = pair * pair_w
            q2 = qkv_ref[0, pl.ds(q0[i], GRID_W), lo:lo + pair_w] * (HEAD_DIM ** -0.5)
            k2 = qkv_ref[0, pl.ds(k0[i], k_tok), GROUP_W + lo:GROUP_W + lo + pair_w]
            values[i, pair] = qkv_ref[0, pl.ds(k0[i], k_tok), 2 * GROUP_W + lo:2 * GROUP_W + lo + pair_w]
            qm = jnp.concatenate([jnp.where(first, q2, 0.0), jnp.where(first, 0.0, q2)], axis=0).astype(BF16)
            scores[i, pair] = lax.dot_general(qm, k2, NT_DIMS, preferred_element_type=F32)
        probs, sums = {}, {}
        for i, pair in units:
            bias = jnp.concatenate(
                [jnp.concatenate([bias_ref[2 * pair + hh, dr0[i] + 2 * m] for m in range(NA_KH // 2)], axis=1)
                 for hh in range(2)], axis=0)
            s = scores[i, pair] + bias
            p = jnp.exp(s - jnp.max(s, axis=-1, keepdims=True))
            probs[i, pair] = p.astype(BF16)
            sums[i, pair] = jnp.sum(p, axis=-1, keepdims=True)
        for i in range(NA_ROW_GROUP):
            outs = []
            for pair in range(N_HEADS // 2):
                o2 = _dot(probs[i, pair], values[i, pair]) / sums[i, pair]
                outs.append(jnp.where(first, o2[0:GRID_W, :], o2[GRID_W:, :]))
            o_ref[0, pl.ds(q0[i], GRID_W), :] = _rms(jnp.concatenate(outs, axis=1)) * gain
        return 0

    lax.fori_loop(0, n_rows // NA_ROW_GROUP, row_group, 0, unroll=4)


def _na_bias_tiles(rpb):
    col = np.arange(GRID_W)
    cs = np.clip(col - NA_KW // 2, 0, GRID_W - NA_KW)
    valid_c = (col[None, :] >= cs[:, None]) & (col[None, :] < cs[:, None] + NA_KW)
    dc = col[None, :] - col[:, None] + NA_KW - 1
    sel_c = (valid_c[None] & (dc[None] == np.arange(2 * NA_KW - 1)[:, None, None])).astype(np.float32)
    by_row = jnp.einsum("hab,bqk->haqk", rpb, sel_c, precision=lax.Precision.HIGHEST)
    by_row = jnp.where(valid_c[None, None], by_row, -1e30)
    return jnp.concatenate([by_row[:, :-1], by_row[:, 1:]], axis=-1).astype(F32)


def _na_mixer(qkv, rpb, norm_g):
    b = qkv.shape[0]
    table = _na_bias_tiles(rpb)
    return pl.pallas_call(
        _na_kernel,
        grid=(b,),
        in_specs=[pl.BlockSpec((1, SEQ, 3 * GROUP_W), lambda i: (i, 0, QKV_NA_BLOCK)),
                  pl.BlockSpec(table.shape, lambda i: (0, 0, 0, 0)),
                  pl.BlockSpec((1, GROUP_W), lambda i: (0, 0))],
        out_specs=pl.BlockSpec((1, SEQ, GROUP_W), lambda i: (i, 0, 0)),
        out_shape=jax.ShapeDtypeStruct((b, SEQ, GROUP_W), F32),
        compiler_params=_params("arbitrary"),
        name="na_mixer",
    )(qkv, table, norm_g.reshape(1, GROUP_W))


ML_STATE_ROWS = 80
ML_PHASE_HEADS = 4


def _ml_kernel(qkv_ref, og_ref, sm_ref, bcol_ref, g_ref, o_ref, bcol_s, rows_s, lrow_s, qt_s, vt_s, htf_s, htb_s,
               c_s, m_s):
    _gate_rows(sm_ref, rows_s, SM_IG, 16)
    raw = rows_s[...].reshape(N_CHUNKS * 16, CHUNK) + bcol_ref[...]
    pre, suf = _chunk_scans(jax.nn.log_sigmoid(raw), 1)
    rid = lax.broadcasted_iota(jnp.int32, (N_CHUNKS * 16, 1), 0) % 16
    scanned = jnp.where(rid < 8 + N_HEADS, pre, suf)
    beta = raw - pltpu.roll(scanned, N_CHUNKS * 16 - 8, 0)
    lrow_s[...] = jnp.where(rid < 8, beta, scanned).reshape(N_CHUNKS, 16, CHUNK)

    pad_rows = jnp.zeros((CHUNK - 8, CHUNK), F32)
    for c in range(N_CHUNKS):
        rows = slice(c * CHUNK, (c + 1) * CHUNK)
        qt_s[c] = qkv_ref[0, rows, 0:GROUP_W].astype(F32).T.astype(BF16)
        vt_s[c] = qkv_ref[0, rows, 2 * GROUP_W:3 * GROUP_W].astype(F32).T.astype(BF16)
        bcol_s[rows, :] = jnp.concatenate([lrow_s[c, 0:8, :], pad_rows], axis=0).T

    si = lax.broadcasted_iota(jnp.int32, (CHUNK, CHUNK), 0)
    ti = lax.broadcasted_iota(jnp.int32, (CHUNK, CHUNK), 1)
    pad = ML_STATE_ROWS - HEAD_DIM
    one_rows = (lax.broadcasted_iota(jnp.int32, (pad, CHUNK), 0) == 0).astype(BF16)
    k_scale = HEAD_DIM ** -0.5
    c_s[...] = jnp.zeros_like(c_s)
    m_s[...] = jnp.zeros_like(m_s)

    def chunk_body(i, _):
        chunk = {0: i, 1: N_CHUNKS - 1 - i}
        rows = {d: pl.multiple_of(chunk[d] * CHUNK, CHUNK) for d in range(2)}
        for d in range(2):
            for h0 in range(0, N_HEADS, ML_PHASE_HEADS):
                phases([(d, h) for h in range(h0, h0 + ML_PHASE_HEADS)], chunk, rows)
        return 0

    def phases(units, chunk, rows):
        ks, qts, vts, cexts, raw_s, inters = {}, {}, {}, {}, {}, {}
        for d, h in units:
            j = d * N_HEADS + h
            k = qkv_ref[0, pl.ds(rows[d], CHUNK), GROUP_W + h * HEAD_DIM:GROUP_W + (h + 1) * HEAD_DIM]
            ks[d, h] = k * k_scale
            qts[d, h] = qt_s[chunk[d], h * HEAD_DIM:(h + 1) * HEAD_DIM, :]
            vts[d, h] = vt_s[chunk[d], h * HEAD_DIM:(h + 1) * HEAD_DIM, :]
            cexts[d, h] = c_s[j]
            raw_s[d, h] = _dot(ks[d, h], qts[d, h])
            inters[d, h] = _dot(cexts[d, h].astype(BF16), qts[d, h])
        s_ts, w_inters, floors, sources, keeps = {}, {}, {}, {}, {}
        for d, h in units:
            j = d * N_HEADS + h
            mask = (si <= ti) if d == 0 else (si >= ti)
            end = CHUNK - 1 if d == 0 else 0
            lrow = lrow_s[chunk[d]]
            beta_r = lrow[j:j + 1, :]
            b_r = lrow[8 + j:8 + j + 1, :]
            m_prev = m_s[j][0:1, 0:1]
            beta_m = jnp.where(mask, bcol_s[pl.ds(rows[d], CHUNK), j:j + 1], -jnp.inf)
            mu = jnp.maximum(m_prev, jnp.max(beta_m, axis=0, keepdims=True))
            s_ts[d, h] = raw_s[d, h] * jnp.exp(beta_m - mu)
            w_inters[d, h] = jnp.exp(m_prev - mu)
            floors[d, h] = jnp.exp(-(b_r + mu))
            top = jnp.maximum(m_prev, jnp.max(beta_r, axis=1, keepdims=True))
            v_ext = jnp.concatenate([vts[d, h], one_rows], axis=0).astype(F32)
            sources[d, h] = (v_ext * jnp.exp(beta_r - top)).astype(BF16)
            keeps[d, h] = jnp.exp(m_prev - top)
            m_s[j] = jnp.broadcast_to(b_r[:, end:end + 1] + top, (8, CHUNK))
        for d, h in units:
            s_t = s_ts[d, h]
            inter = inters[d, h]
            num = w_inters[d, h] * inter[0:HEAD_DIM, :] + _dot(vts[d, h], s_t.astype(BF16))
            den = w_inters[d, h] * inter[HEAD_DIM:HEAD_DIM + 1, :] + jnp.sum(s_t, axis=0, keepdims=True)
            out_s = htf_s if d == 0 else htb_s
            out_s[chunk[d], h * HEAD_DIM:(h + 1) * HEAD_DIM, :] = num / jnp.maximum(jnp.abs(den), floors[d, h])
        for d, h in units:
            c_s[d * N_HEADS + h] = keeps[d, h] * cexts[d, h] + _dot(sources[d, h], ks[d, h])

    lax.fori_loop(0, N_CHUNKS, chunk_body, 0, unroll=2)

    gain = g_ref[...]

    def out_body(c, _):
        r0 = pl.multiple_of(c * CHUNK, CHUNK)
        h_t = htf_s[c] + htb_s[c]
        normed = []
        for h in range(N_HEADS):
            x = h_t[h * HEAD_DIM:(h + 1) * HEAD_DIM, :]
            normed.append(x * lax.rsqrt(jnp.mean(x * x, axis=0, keepdims=True) + NORM_EPS))
        y = jnp.concatenate(normed, axis=0).T * gain
        o_ref[0, pl.ds(r0, CHUNK), :] = jax.nn.sigmoid(og_ref[0, pl.ds(r0, CHUNK), :]) * y
        return 0

    lax.fori_loop(0, N_CHUNKS, out_body, 0)


def _ml_mixer(qkv, og, sm, i_bias, f_bias, norm_g):
    b = qkv.shape[0]
    bias16 = jnp.concatenate([i_bias.reshape(8), f_bias.reshape(8)])
    bcol = jnp.tile(bias16.reshape(16, 1), (N_CHUNKS, 1))
    full = lambda shape: pl.BlockSpec(shape, lambda i: (0,) * len(shape))
    return pl.pallas_call(
        _ml_kernel,
        grid=(b,),
        in_specs=[pl.BlockSpec((1, SEQ, 3 * GROUP_W), lambda i: (i, 0, QKV_ML_BLOCK)),
                  pl.BlockSpec((1, SEQ, GROUP_W), lambda i: (i, 0, WIDE_MLO_BLOCK)),
                  pl.BlockSpec((1, SEQ, CHUNK), lambda i: (i, 0, WIDE_SM_BLOCK)),
                  full((N_CHUNKS * 16, 1)), full((1, GROUP_W))],
        out_specs=pl.BlockSpec((1, SEQ, GROUP_W), lambda i: (i, 0, 0)),
        out_shape=jax.ShapeDtypeStruct((b, SEQ, GROUP_W), F32),
        scratch_shapes=[pltpu.VMEM((SEQ, CHUNK), F32), pltpu.VMEM((N_CHUNKS, 16, CHUNK), F32),
                        pltpu.VMEM((N_CHUNKS, 16, CHUNK), F32), pltpu.VMEM((N_CHUNKS, GROUP_W, CHUNK), BF16),
                        pltpu.VMEM((N_CHUNKS, GROUP_W, CHUNK), BF16), pltpu.VMEM((N_CHUNKS, GROUP_W, CHUNK), F32),
                        pltpu.VMEM((N_CHUNKS, GROUP_W, CHUNK), F32),
                        pltpu.VMEM((2 * N_HEADS, ML_STATE_ROWS, HEAD_DIM), F32),
                        pltpu.VMEM((2 * N_HEADS, 8, CHUNK), F32)],
        compiler_params=_params("arbitrary"),
        name="mlstm_mixer",
    )(qkv, og, sm, bcol, norm_g.reshape(1, GROUP_W))


HALF = SEQ // 2


def _dft_tables():
    f = np.arange(HALF, dtype=np.int64)[:, None]
    t = np.arange(SEQ, dtype=np.int64)[None, :]
    ang = (2.0 * np.pi / FFT_N) * ((f * t) % FFT_N).astype(np.float64)
    wc = np.cos(ang)
    ws = np.sin(ang)
    ws[0, :] = np.array([1.0, 1.0, -1.0, -1.0])[np.arange(SEQ) % 4]
    tables = []
    for parity in range(2):
        w = np.stack([wc[:, parity::2], ws[:, parity::2]]).reshape(2, HALF // FREQ_SUB, FREQ_SUB, HALF)
        w = np.ascontiguousarray(w.transpose(1, 0, 2, 3).reshape(2 * HALF, HALF)).astype(np.float32)
        tables.append((jnp.asarray(w).astype(BF16), jnp.asarray(np.ascontiguousarray(w.T)).astype(BF16)))
    return tables


def _hy_features():
    t = jnp.linspace(0.0, 1.0, SEQ, dtype=F32)[:, None]
    bands = (HY_EMB - 1) // 2
    f = jnp.linspace(1e-4, bands - 1, bands, dtype=F32)
    ang = (2.0 * math.pi) * (jnp.arange(SEQ, dtype=F32) / SEQ)[:, None] * f[None, :]
    feats = jnp.concatenate([t, jnp.cos(ang), -jnp.sin(ang)], axis=-1)
    deltas = jnp.abs(jnp.linspace(math.log(1e-2) / 1.5, math.log(1e-2) / 0.3, GROUP_W, dtype=F32))
    return jnp.pad(feats, ((0, 0), (0, CHUNK - HY_EMB))), t, deltas.reshape(1, GROUP_W)


def _hy_filter_kernel(feats_ref, w1_ref, b1_ref, w2_ref, b2_ref, w3_ref, freq_ref, t_ref, dl_ref, hp_ref, hm_ref):
    h = jnp.sin(freq_ref[0:1, :] * (_dot(feats_ref[...].astype(BF16), w1_ref[...].astype(BF16)) + b1_ref[...]))
    h = jnp.sin(freq_ref[1:2, :] * (_dot(h.astype(BF16), w2_ref[...].astype(BF16)) + b2_ref[...]))
    h = _dot(h.astype(BF16), w3_ref[...].astype(BF16))
    win = jnp.exp(-t_ref[...] * dl_ref[...])
    row = lax.broadcasted_iota(jnp.int32, (SEQ, 1), 0)
    for o in range(2):
        hf = h[:, (2 * o) * GROUP_W:(2 * o + 1) * GROUP_W] * win
        hb = h[:, (2 * o + 1) * GROUP_W:(2 * o + 2) * GROUP_W] * win
        r = lax.rsqrt(jnp.sum(hf * hf + hb * hb, axis=0, keepdims=True) + NORM_EPS)
        hf = hf * r
        hb = jnp.where(row == 0, 0.0, hb * r)
        hp_ref[:, o * GROUP_W:(o + 1) * GROUP_W] = (hf + hb).astype(BF16)
        hm_ref[:, o * GROUP_W:(o + 1) * GROUP_W] = (hf - hb).astype(BF16)


def _hy_spec_kernel(we_ref, wo_ref, hp_ref, hm_ref, kc_ref, ks_ref, kcp_ref, ksp_ref):
    fb = FREQ_SUB
    nc = 2 * GROUP_W
    row0 = (lax.broadcasted_iota(jnp.int32, (fb, 1), 0) + pl.program_id(0) * fb) == 0
    edge = jnp.where(row0, 1.0 / FFT_N, 2.0 / FFT_N)
    pe = _dot(we_ref[0:fb, :], hp_ref[:, 0:nc])
    po = _dot(wo_ref[0:fb, :], hp_ref[:, nc:])
    me = _dot(we_ref[fb:, :], hm_ref[:, 0:nc])
    mo = _dot(wo_ref[fb:, :], hm_ref[:, nc:])
    mid_cos = _dot(we_ref[fb:fb + 8, :], hp_ref[:, 0:nc])[0:1, :]
    kc_ref[...] = (pe + po) * edge
    kcp_ref[...] = (pe - po) * edge
    ks_ref[...] = jnp.where(row0, mid_cos, me + mo) * (2.0 / FFT_N)
    ksp_ref[...] = jnp.where(row0, mo, mo - me) * (2.0 / FFT_N)


def _hy_spectrum(tables, w1, b1, w2, b2, w3, freq):
    feats, t, deltas = _hy_features()
    w1p = jnp.pad(w1, ((0, CHUNK - HY_EMB), (0, 0)))
    hp, hm = pl.pallas_call(
        _hy_filter_kernel,
        out_shape=[jax.ShapeDtypeStruct((SEQ, 2 * GROUP_W), BF16)] * 2,
        compiler_params=pltpu.CompilerParams(vmem_limit_bytes=VMEM_LIMIT),
        name="hyena_filter",
    )(feats, w1p, b1.reshape(1, HY_FFN), w2, b2.reshape(1, HY_FFN), w3, freq, t, deltas)
    split = lambda h: h.reshape(HALF, 4 * GROUP_W)
    table_spec = pl.BlockSpec((2 * FREQ_SUB, HALF), lambda j: (j, 0))
    return pl.pallas_call(
        _hy_spec_kernel,
        grid=(HALF // FREQ_SUB,),
        in_specs=[table_spec, table_spec,
                  pl.BlockSpec((HALF, 4 * GROUP_W), lambda j: (0, 0)),
                  pl.BlockSpec((HALF, 4 * GROUP_W), lambda j: (0, 0))],
        out_specs=[pl.BlockSpec((FREQ_SUB, 2 * GROUP_W), lambda j: (j, 0))] * 4,
        out_shape=[jax.ShapeDtypeStruct((HALF, 2 * GROUP_W), F32)] * 4,
        compiler_params=_params("arbitrary"),
        name="hyena_spectrum",
    )(tables[0][0], tables[1][0], split(hp), split(hm))


def _split_conv_rows(e_ref, o_ref, b, r0, w, bias):
    e = e_ref[b, pl.ds(r0, CHUNK), :]
    o = o_ref[b, pl.ds(r0, CHUNK), :]
    o_prev = o_ref[b, pl.ds(jnp.maximum(r0 - 1, 0), 1), :]
    e_next = e_ref[b, pl.ds(jnp.minimum(r0 + CHUNK, HALF - 1), 1), :]
    o_prev = jnp.where(r0 > 0, o_prev, 0.0)
    e_next = jnp.where(r0 + CHUNK < HALF, e_next, 0.0)
    row = lax.broadcasted_iota(jnp.int32, (CHUNK, 1), 0)
    o_dn = jnp.where(row == 0, o_prev, pltpu.roll(o, 1, 0))
    e_up = jnp.where(row == CHUNK - 1, e_next, pltpu.roll(e, CHUNK - 1, 0))
    conv_e = w[0:1, :] * o_dn + w[1:2, :] * e + w[2:3, :] * o + bias
    conv_o = w[0:1, :] * e + w[1:2, :] * o + w[2:3, :] * e_up + bias
    return conv_e, conv_o


def _hy_conv_kernel(ze_ref, zo_ref, ge_ref, go_ref, cwz_ref, cbz_ref, cwg_ref, cbg_ref, we_ref, wo_ref, wet_ref, wot_ref,
                    kc_ref, ks_ref, kcp_ref, ksp_ref, skip_ref, g_ref, o_ref, ze_s, zo_s, acce_s, acco_s, *,
                    conv_z, final_norm):
    j = pl.program_id(1)
    n_b = ze_ref.shape[0]
    fb = FREQ_BLOCK
    half_chunks = HALF // CHUNK

    def z_rows(b, r0):
        if conv_z:
            return _split_conv_rows(ze_ref, zo_ref, b, r0, cwz_ref[...], cbz_ref[...])
        return ze_ref[b, pl.ds(r0, CHUNK), :], zo_ref[b, pl.ds(r0, CHUNK), :]

    @pl.when(j == 0)
    def _():
        for b in range(n_b):
            def fill(c, _, b=b):
                r0 = pl.multiple_of(c * CHUNK, CHUNK)
                z_e, z_o = z_rows(b, r0)
                ze_s[pl.ds(r0, CHUNK), b * GROUP_W:(b + 1) * GROUP_W] = z_e.astype(BF16)
                zo_s[pl.ds(r0, CHUNK), b * GROUP_W:(b + 1) * GROUP_W] = z_o.astype(BF16)
                return 0

            lax.fori_loop(0, half_chunks, fill, 0)
        acce_s[...] = jnp.zeros_like(acce_s)
        acco_s[...] = jnp.zeros_like(acco_s)

    sub = FREQ_SUB
    groups = range(fb // sub)
    fwd = []
    for s in groups:
        rows = slice(2 * sub * s, 2 * sub * (s + 1))
        fwd.append((_dot(we_ref[rows, :], ze_s[...]), _dot(wo_ref[rows, :], zo_s[...])))
    inv_e, inv_o = [], []
    for s in groups:
        row0 = (lax.broadcasted_iota(jnp.int32, (sub, 1), 0) + j * fb + s * sub) == 0
        tiled = lambda ref: jnp.concatenate([ref[sub * s:sub * (s + 1), :]] * n_b, axis=1)
        kc, ks, kcp, ksp = tiled(kc_ref), tiled(ks_ref), tiled(kcp_ref), tiled(ksp_ref)
        xe, xo = fwd[s]
        ae, be, ao, bo = xe[0:sub, :], xe[sub:, :], xo[0:sub, :], xo[sub:, :]
        xc, xcp, xs, xsp = ae + ao, ae - ao, be + bo, bo - be
        yc = xc * kc - xs * ks
        ys = xc * ks + xs * kc
        ycp = xcp * kcp - xsp * ksp
        ysp = xcp * ksp + xsp * kcp
        dc, nyq = xc * kc, xcp * kcp
        pc = jnp.where(row0, dc + nyq, yc + ycp)
        mc = jnp.where(row0, dc - nyq, yc - ycp)
        ms = jnp.where(row0, be * ks - bo * ksp, ys - ysp)
        ps = jnp.where(row0, be * ksp + bo * ks, ys + ysp)
        cols = slice(2 * sub * s, 2 * sub * (s + 1))
        inv_e.append(_dot(wet_ref[:, cols], jnp.concatenate([pc, ms], axis=0).astype(BF16)))
        inv_o.append(_dot(wot_ref[:, cols], jnp.concatenate([mc, ps], axis=0).astype(BF16)))
    acce_s[...] += sum(inv_e)
    acco_s[...] += sum(inv_o)

    @pl.when(j == pl.num_programs(1) - 1)
    def _():
        for b in range(n_b):
            def finish(c, _, b=b):
                r0 = pl.multiple_of(c * CHUNK, CHUNK)
                gates = _split_conv_rows(ge_ref, go_ref, b, r0, cwg_ref[...], cbg_ref[...])
                for parity, (gate, z, acc) in enumerate(zip(gates, z_rows(b, r0), (acce_s, acco_s))):
                    r = gate * (acc[pl.ds(r0, CHUNK), b * GROUP_W:(b + 1) * GROUP_W] + skip_ref[...] * z)
                    if final_norm:
                        r = _rms(r) * g_ref[...]
                    o_ref[b, pl.ds(r0, CHUNK), parity * GROUP_W:(parity + 1) * GROUP_W] = r
                return 0

            lax.fori_loop(0, half_chunks, finish, 0)


def _hy_long_conv(z, z_cols, gates, gate_cols, conv_w, conv_b, conv_z, tables, spectrum, order, skip, norm_g,
                  final_norm):
    b = z.shape[0]
    bg = HY_BATCH_BLOCK
    taps = lambda col: (conv_w[:, col * GROUP_W:(col + 1) * GROUP_W],
                        conv_b[col * GROUP_W:(col + 1) * GROUP_W].reshape(1, GROUP_W))
    small = lambda shape: pl.BlockSpec(shape, lambda g, j: (0, 0))
    act = lambda col: pl.BlockSpec((bg, HALF, GROUP_W), lambda g, j: (g, 0, col))
    fwd = pl.BlockSpec((2 * FREQ_BLOCK, HALF), lambda g, j: (j, 0))
    inv = pl.BlockSpec((HALF, 2 * FREQ_BLOCK), lambda g, j: (0, j))
    spec = pl.BlockSpec((FREQ_BLOCK, GROUP_W), lambda g, j: (j, order))
    (we, wet), (wo, wot) = tables
    return pl.pallas_call(
        functools.partial(_hy_conv_kernel, conv_z=conv_z, final_norm=final_norm),
        grid=(b // bg, N_FREQ_BLOCKS),
        in_specs=[act(z_cols[0]), act(z_cols[1]), act(gate_cols[0]), act(gate_cols[1]),
                  small((3, GROUP_W)), small((1, GROUP_W)), small((3, GROUP_W)), small((1, GROUP_W)),
                  fwd, fwd, inv, inv, spec, spec, spec, spec, small((1, GROUP_W)), small((1, GROUP_W))],
        out_specs=pl.BlockSpec((bg, HALF, 2 * GROUP_W), lambda g, j: (g, 0, 0)),
        out_shape=jax.ShapeDtypeStruct((b, HALF, 2 * GROUP_W), F32),
        scratch_shapes=[pltpu.VMEM((HALF, bg * GROUP_W), BF16), pltpu.VMEM((HALF, bg * GROUP_W), BF16),
                        pltpu.VMEM((HALF, bg * GROUP_W), F32), pltpu.VMEM((HALF, bg * GROUP_W), F32)],
        compiler_params=_params("arbitrary", "arbitrary"),
        name="hyena_long_conv",
    )(z, z, gates, gates, *taps(0), *taps(order + 1), we, wo, wet, wot, *spectrum,
      skip.reshape(1, GROUP_W), norm_g.reshape(1, GROUP_W))


def _hy_mixer(u, tables, conv_w, conv_b, w1, b1, w2, b2, w3, freq, skip, norm_g):
    spectrum = _hy_spectrum(tables, w1, b1, w2, b2, w3, freq)
    z1 = _hy_long_conv(u, (0, 3), u, (1, 4), conv_w, conv_b, True, tables, spectrum, 0, skip[0], norm_g, False)
    return _hy_long_conv(z1, (0, 1), u, (2, 5), conv_w, conv_b, False, tables, spectrum, 1, skip[1], norm_g, True)


FFN_SPLITS = ((0, 1024), (1024, 2048), (2048, FFN_HIDDEN))


def _ffn_kernel(x_ref, mod_ref, ya_ref, yb_ref, yc_ref, yd_ref, perm_ref, wo_ref, wg_ref, wu_ref, wd_ref, gf_ref, o_ref,
                *, final_norm):
    yd = yd_ref[...]
    yd = jnp.concatenate([yd[:, 0:GROUP_W], yd[:, GROUP_W:]], axis=0).astype(BF16)
    yd = _dot(perm_ref[...], yd)
    y_mix = jnp.concatenate([ya_ref[...], yb_ref[...], yc_ref[...], yd], axis=1).astype(BF16)
    x = x_ref[...] + mod_ref[0, 2:3, :] * _dot(y_mix, wo_ref[...])
    h = (_rms(x) * (1.0 + mod_ref[0, 4:5, :]) + mod_ref[0, 3:4, :]).astype(BF16)
    acc = jnp.zeros_like(x)
    for lo, hi in FFN_SPLITS:
        a = _silu(_dot(h, wg_ref[:, lo:hi])) * _dot(h, wu_ref[:, lo:hi])
        acc = acc + _dot(a.astype(BF16), wd_ref[lo:hi, :])
    y = x + mod_ref[0, 5:6, :] * acc
    if final_norm:
        y = _rms(y) * gf_ref[...]
    o_ref[...] = y


def _out_ffn(x2, mod_l, ys, w_out, wg, wu, wd, layer, final_g, final_norm):
    n_tok, d = x2.shape
    tm = TOKEN_TILE
    tiles_per_seq = SEQ // tm
    tok = np.arange(tm)
    perm = (np.arange(tm)[None, :] == (tok // 2 + (tok % 2) * (tm // 2))[:, None]).astype(np.float32)
    resident = lambda shape: pl.BlockSpec((None,) + shape, lambda i: (layer, 0, 0), pipeline_mode=pl.Buffered(1))
    return pl.pallas_call(
        functools.partial(_ffn_kernel, final_norm=final_norm),
        grid=(n_tok // tm,),
        in_specs=[pl.BlockSpec((tm, d), lambda i: (i, 0)),
                  pl.BlockSpec((1, 6, d), lambda i: (i // tiles_per_seq, 0, 0))]
                 + [pl.BlockSpec((tm, GROUP_W), lambda i: (i, 0))] * 3
                 + [pl.BlockSpec((tm // 2, 2 * GROUP_W), lambda i: (i, 0)),
                    pl.BlockSpec((tm, tm), lambda i: (0, 0), pipeline_mode=pl.Buffered(1)),
                    resident((d, d)), resident((d, FFN_HIDDEN)), resident((d, FFN_HIDDEN)), resident((FFN_HIDDEN, d)),
                    pl.BlockSpec((1, d), lambda i: (0, 0))],
        out_specs=pl.BlockSpec((tm, d), lambda i: (i, 0)),
        out_shape=jax.ShapeDtypeStruct((n_tok, d), F32),
        compiler_params=_params("arbitrary"),
        name="out_proj_swiglu",
    )(x2, mod_l, *ys, jnp.asarray(perm).astype(BF16), w_out, wg, wu, wd, final_g.reshape(1, d))


def _permute_w_in(w):
    w = w.astype(BF16)
    cols = [w[..., 0:1024], w[..., 2568:2824], w[..., 1024:1032], w[..., 2824:2840],
            jnp.zeros(w.shape[:-1] + (P_PAD - 3608,), BF16), w[..., 1032:1800], w[..., 1800:2568], w[..., 2840:3608]]
    return jnp.concatenate(cols, axis=-1)


def kernel(x, c, mod_w, mod_b, w_in, ssd_conv_w, ssd_conv_b, ssd_dt_bias, ssd_a_log, ssd_d, ssd_norm_g, na_rpb, na_norm_g, ml_i_bias, ml_f_bias, ml_norm_g, hy_conv_w, hy_conv_b, hy_w1, hy_b1, hy_w2, hy_b2, hy_w3, hy_freq, hy_skip, hy_norm_g, w_out, ffn_w_gate, ffn_w_up, ffn_w_down, final_norm_g):
    b, seq, d = x.shape
    assert seq == SEQ and d == D_MODEL and b % HY_BATCH_BLOCK == 0
    depth = mod_w.shape[0]
    mod = _modulation(c, mod_w, mod_b).reshape(depth, b, 6, d)
    dft = _dft_tables()
    w_in_p = _permute_w_in(w_in)
    w_out_b, w_gate_b, w_up_b, w_down_b = (w.astype(BF16) for w in (w_out, ffn_w_gate, ffn_w_up, ffn_w_down))
    x2 = x.reshape(b * seq, d)
    for l in range(depth):
        wide, qkv, hy = _in_proj(x2, mod[l], w_in_p, l)
        wide = wide.reshape(b, seq, WIDE_W)
        qkv = qkv.reshape(b, seq, QKV_W)
        y_ssd = _ssd_mixer(wide, wide, ssd_conv_w[l], ssd_conv_b[l], ssd_dt_bias[l], ssd_a_log[l], ssd_d[l],
                           ssd_norm_g[l])
        y_na = _na_mixer(qkv, na_rpb[l], na_norm_g[l])
        y_ml = _ml_mixer(qkv, wide, wide, ml_i_bias[l], ml_f_bias[l], ml_norm_g[l])
        y_hy = _hy_mixer(hy.reshape(b, HALF, 2 * HY_W), dft, hy_conv_w[l], hy_conv_b[l], hy_w1[l], hy_b1[l], hy_w2[l],
                         hy_b2[l], hy_w3[l], hy_freq[l], hy_skip[l], hy_norm_g[l])
        ys = [t.reshape(b * seq, GROUP_W) for t in (y_ssd, y_na, y_ml)] + [y_hy.reshape(b * HALF, 2 * GROUP_W)]
        x2 = _out_ffn(x2, mod[l], ys, w_out_b, w_gate_b, w_up_b, w_down_b, l, final_norm_g, l == depth - 1)
    return x2.reshape(b, seq, d)
```

```python
import functools
import math

import numpy as np
import jax
import jax.numpy as jnp
from jax import lax
from jax.experimental import pallas as pl
from jax.experimental.pallas import tpu as pltpu

F32 = jnp.float32
BF16 = jnp.bfloat16

D_MODEL = 1024
SEQ = 2048
GRID_W = 64
GROUP_W = 256
HEAD_DIM = 64
N_HEADS = 4
SSD_STATE = 128
SSD_XBC = 768
CHUNK = 128
N_CHUNKS = SEQ // CHUNK
NA_KH = 8
NA_KW = 16
NA_ROW_GROUP = 4
HY_EMB = 33
HY_FFN = 64
FFN_HIDDEN = 2816
NORM_EPS = 1e-6
FFT_N = 2 * SEQ
FREQ_SUB = 128
FREQ_BLOCK = 256
N_FREQ_BLOCKS = SEQ // 2 // FREQ_BLOCK
HY_BATCH_BLOCK = 2
TOKEN_TILE = 512
IN_PROJ_TILE = 1024
VMEM_LIMIT = 56 * 1024 * 1024

WIDE_W = 1408
WIDE_ZX_BLOCK = 0
WIDE_MLO_BLOCK = 4
WIDE_SM_BLOCK = 10
QKV_W = 1536
QKV_NA_BLOCK = 0
QKV_ML_BLOCK = 1
HY_W = 768
P_PAD = WIDE_W + QKV_W + HY_W
SM_DT = 0
SM_IG = 8
SM_FG = 16

NT_DIMS = (((1,), (1,)), ((), ()))
TN_DIMS = (((0,), (0,)), ((), ()))


def _rms(x):
    return x * lax.rsqrt(jnp.mean(x * x, axis=-1, keepdims=True) + NORM_EPS)


def _silu(x):
    half = 0.5 * x
    return half + half * jnp.tanh(half)


def _dot(a, b):
    return jnp.dot(a, b, preferred_element_type=F32)


def _params(*sem):
    return pltpu.CompilerParams(dimension_semantics=sem, vmem_limit_bytes=VMEM_LIMIT)


def _mod_kernel(c_ref, w_ref, b_ref, o_ref):
    cond = _silu(c_ref[...]).astype(BF16)
    o_ref[0] = _dot(cond, w_ref[0].astype(BF16)) + b_ref[0]


def _modulation(c, mod_w, mod_b):
    depth, d, n = mod_w.shape
    b = c.shape[0]
    wn = n // 2
    return pl.pallas_call(
        _mod_kernel,
        grid=(depth, n // wn),
        in_specs=[pl.BlockSpec((b, d), lambda l, j: (0, 0)),
                  pl.BlockSpec((1, d, wn), lambda l, j: (l, 0, j)),
                  pl.BlockSpec((1, 1, wn), lambda l, j: (l, 0, j))],
        out_specs=pl.BlockSpec((1, b, wn), lambda l, j: (l, 0, j)),
        out_shape=jax.ShapeDtypeStruct((depth, b, n), F32),
        compiler_params=_params("arbitrary", "arbitrary"),
        name="adaln_mod",
    )(c, mod_w, mod_b.reshape(depth, 1, n))


def _in_proj_kernel(x_ref, mod_ref, w_ref, wide_ref, qkv_ref, hy_ref, cols_s):
    h = (_rms(x_ref[...]) * (1.0 + mod_ref[0, 1:2, :]) + mod_ref[0, 0:1, :]).astype(BF16)
    wide_ref[...] = _dot(h, w_ref[:, 0:WIDE_W])
    qkv_ref[...] = _dot(h, w_ref[:, WIDE_W:WIDE_W + QKV_W]).astype(BF16)
    half = x_ref.shape[0] // 2
    n_slabs = cols_s.shape[0]
    for cb in range(n_slabs):
        cols_s[cb] = h[:, cb * CHUNK:(cb + 1) * CHUNK].astype(F32)
    w_hy = w_ref[:, WIDE_W + QKV_W:P_PAD]
    for parity in range(2):
        rows = jnp.concatenate([cols_s[cb, pl.ds(parity, half, stride=2), :] for cb in range(n_slabs)], axis=1)
        hy_ref[:, parity * HY_W:(parity + 1) * HY_W] = _dot(rows.astype(BF16), w_hy)


def _in_proj(x2, mod_l, w_perm, layer):
    n_tok, d = x2.shape
    tm = IN_PROJ_TILE
    tiles_per_seq = SEQ // tm
    return pl.pallas_call(
        _in_proj_kernel,
        grid=(n_tok // tm,),
        in_specs=[pl.BlockSpec((tm, d), lambda i: (i, 0)),
                  pl.BlockSpec((1, 6, d), lambda i: (i // tiles_per_seq, 0, 0)),
                  pl.BlockSpec((None, d, P_PAD), lambda i: (layer, 0, 0), pipeline_mode=pl.Buffered(1))],
        out_specs=[pl.BlockSpec((tm, WIDE_W), lambda i: (i, 0)), pl.BlockSpec((tm, QKV_W), lambda i: (i, 0)),
                   pl.BlockSpec((tm // 2, 2 * HY_W), lambda i: (i, 0))],
        out_shape=[jax.ShapeDtypeStruct((n_tok, WIDE_W), F32), jax.ShapeDtypeStruct((n_tok, QKV_W), BF16),
                   jax.ShapeDtypeStruct((n_tok // 2, 2 * HY_W), F32)],
        scratch_shapes=[pltpu.VMEM((d // CHUNK, tm, CHUNK), F32)],
        compiler_params=_params("arbitrary"),
        name="in_proj",
    )(x2, mod_l, w_perm)


def _dwconv3_rows(src_ref, b, r0, cols, w, bias):
    x = src_ref[b, pl.ds(r0, CHUNK), cols]
    up = src_ref[b, pl.ds(jnp.maximum(r0 - 1, 0), 1), cols]
    dn = src_ref[b, pl.ds(jnp.minimum(r0 + CHUNK, SEQ - 1), 1), cols]
    up = jnp.where(r0 > 0, up, 0.0)
    dn = jnp.where(r0 + CHUNK < SEQ, dn, 0.0)
    row = lax.broadcasted_iota(jnp.int32, (CHUNK, 1), 0)
    prev = jnp.where(row == 0, up, pltpu.roll(x, 1, 0))
    nxt = jnp.where(row == CHUNK - 1, dn, pltpu.roll(x, CHUNK - 1, 0))
    return w[0:1, :] * prev + w[1:2, :] * x + w[2:3, :] * nxt + bias


def _chunk_scans(a, axis):
    n = a.shape[axis]
    shape = [1, 1]
    shape[axis] = n
    pos = lax.broadcasted_iota(jnp.int32, tuple(shape), axis) % CHUNK
    pre, suf = a, a
    k = 1
    while k < CHUNK:
        pre = pre + jnp.where(pos >= k, pltpu.roll(pre, k, axis), 0.0)
        suf = suf + jnp.where(pos < CHUNK - k, pltpu.roll(suf, n - k, axis), 0.0)
        k *= 2
    return pre, suf


def _gate_rows(sm_ref, rows_s, lo, n):
    for c in range(N_CHUNKS):
        t = sm_ref[0, c * CHUNK:(c + 1) * CHUNK, :].T
        rows_s[c] = t[lo:lo + n, :]


def _ssd_kernel(zx_ref, sm_ref, cw_ref, cb_ref, dtb_col_ref, a_col_ref, dsk_ref, g_ref,
                o_ref, xs_s, b_s, xst_s, ct_s, acol_s, rows_s, dtrow_s, arow_s, ytf_s, ytb_s, st_s):
    cw = cw_ref[...]
    cb = cb_ref[...]

    def conv_body(c, _):
        r0 = pl.multiple_of(c * CHUNK, CHUNK)
        v = _silu(_dwconv3_rows(zx_ref, 0, r0, slice(GROUP_W, GROUP_W + SSD_XBC), cw, cb))
        xs_s[pl.ds(r0, CHUNK), :] = v[:, 0:GROUP_W]
        b_s[pl.ds(r0, CHUNK), :] = v[:, GROUP_W:2 * GROUP_W].astype(BF16)
        xst_s[c] = v[:, 0:GROUP_W].T
        ct_s[c] = v[:, 2 * GROUP_W:3 * GROUP_W].astype(BF16).T
        return 0

    lax.fori_loop(0, N_CHUNKS, conv_body, 0)

    _gate_rows(sm_ref, rows_s, SM_DT, 8)
    dt_row = jax.nn.softplus(rows_s[...].reshape(N_CHUNKS * 8, CHUNK) + dtb_col_ref[...])
    pre, suf = _chunk_scans(dt_row * (-jnp.exp(a_col_ref[...])), 1)
    rid = lax.broadcasted_iota(jnp.int32, (N_CHUNKS * 8, 1), 0) % 8
    dtrow_s[...] = dt_row.reshape(N_CHUNKS, 8, CHUNK)
    arow_s[...] = jnp.where(rid < N_HEADS, pre, suf).reshape(N_CHUNKS, 8, CHUNK)
    pad_rows = jnp.zeros((CHUNK - 8, CHUNK), F32)
    for c in range(N_CHUNKS):
        acol_s[c * CHUNK:(c + 1) * CHUNK, :] = jnp.concatenate([arow_s[c], pad_rows], axis=0).T

    si = lax.broadcasted_iota(jnp.int32, (CHUNK, CHUNK), 0)
    ti = lax.broadcasted_iota(jnp.int32, (CHUNK, CHUNK), 1)
    first = lax.broadcasted_iota(jnp.int32, (2 * HEAD_DIM, 1), 0) < HEAD_DIM
    st_s[...] = jnp.zeros_like(st_s)

    def chunk_body(i, _):
        units = [(d, g) for d in range(2) for g in range(2)]
        chunk = {0: i, 1: N_CHUNKS - 1 - i}
        rows = {d: pl.multiple_of(chunk[d] * CHUNK, CHUNK) for d in range(2)}
        bgs, cgs, prevs, scores, y_offs = {}, {}, {}, {}, {}
        for d, g in units:
            bgs[d, g] = b_s[pl.ds(rows[d], CHUNK), g * SSD_STATE:(g + 1) * SSD_STATE]
            cgs[d, g] = ct_s[chunk[d], g * SSD_STATE:(g + 1) * SSD_STATE, :]
            prevs[d, g] = st_s[d, g]
            scores[d, g] = _dot(bgs[d, g], cgs[d, g])
            y_offs[d, g] = _dot(prevs[d, g].astype(BF16), cgs[d, g])
        weighted, xdts, x_ends, carries, grow = {}, {}, {}, {}, {}
        for d, g in units:
            mask = (si <= ti) if d == 0 else (si >= ti)
            end = CHUNK - 1 if d == 0 else 0
            acol = acol_s[pl.ds(rows[d], CHUNK), :]
            arow = arow_s[chunk[d]]
            dtr = dtrow_s[chunk[d]]
            for hh in range(2):
                h = 2 * g + hh
                j = d * N_HEADS + h
                ar = arow[j:j + 1, :]
                a_end = ar[:, end:end + 1]
                decay = jnp.exp(jnp.where(mask, ar - acol[:, j:j + 1], -jnp.inf))
                xdt_t = xst_s[chunk[d], h * HEAD_DIM:(h + 1) * HEAD_DIM, :] * dtr[j:j + 1, :]
                weighted[d, h] = (scores[d, g] * decay).astype(BF16)
                xdts[d, h] = xdt_t.astype(BF16)
                x_ends[d, h] = (xdt_t * jnp.exp(a_end - ar)).astype(BF16)
                carries[d, h] = jnp.exp(a_end)
                grow[d, h] = jnp.exp(ar)
        for d in range(2):
            ys = []
            for h in range(N_HEADS):
                g, hh = divmod(h, 2)
                y_off = y_offs[d, g][hh * HEAD_DIM:(hh + 1) * HEAD_DIM, :]
                ys.append(_dot(xdts[d, h], weighted[d, h]) + y_off * grow[d, h])
            y_t = jnp.concatenate(ys, axis=0)
            if d == 0:
                ytf_s[chunk[d]] = y_t
            else:
                ytb_s[chunk[d]] = y_t
        for d, g in units:
            keep = jnp.where(first, carries[d, 2 * g], carries[d, 2 * g + 1])
            x_end = jnp.concatenate([x_ends[d, 2 * g], x_ends[d, 2 * g + 1]], axis=0)
            st_s[d, g] = prevs[d, g] * keep + _dot(x_end, bgs[d, g])
        return 0

    lax.fori_loop(0, N_CHUNKS, chunk_body, 0, unroll=8)

    dsk = dsk_ref[...]
    gain = g_ref[...]

    def out_body(c, _):
        r0 = pl.multiple_of(c * CHUNK, CHUNK)
        y = (ytf_s[c] + ytb_s[c]).T + dsk * xs_s[pl.ds(r0, CHUNK), :]
        y = y * _silu(zx_ref[0, pl.ds(r0, CHUNK), 0:GROUP_W])
        o_ref[0, pl.ds(r0, CHUNK), :] = _rms(y) * gain
        return 0

    lax.fori_loop(0, N_CHUNKS, out_body, 0)


def _ssd_mixer(zx, sm, conv_w, conv_b, dt_bias, a_log, d_skip, norm_g):
    b = zx.shape[0]
    tile_col = lambda v: jnp.tile(v.reshape(8, 1), (N_CHUNKS, 1))
    full = lambda shape: pl.BlockSpec(shape, lambda i: (0,) * len(shape))
    return pl.pallas_call(
        _ssd_kernel,
        grid=(b,),
        in_specs=[pl.BlockSpec((1, SEQ, 1024), lambda i: (i, 0, WIDE_ZX_BLOCK)),
                  pl.BlockSpec((1, SEQ, CHUNK), lambda i: (i, 0, WIDE_SM_BLOCK)),
                  full((3, SSD_XBC)), full((1, SSD_XBC)),
                  full((N_CHUNKS * 8, 1)), full((N_CHUNKS * 8, 1)), full((1, GROUP_W)), full((1, GROUP_W))],
        out_specs=pl.BlockSpec((1, SEQ, GROUP_W), lambda i: (i, 0, 0)),
        out_shape=jax.ShapeDtypeStruct((b, SEQ, GROUP_W), F32),
        scratch_shapes=[pltpu.VMEM((SEQ, GROUP_W), F32), pltpu.VMEM((SEQ, GROUP_W), BF16),
                        pltpu.VMEM((N_CHUNKS, GROUP_W, CHUNK), F32), pltpu.VMEM((N_CHUNKS, GROUP_W, CHUNK), BF16),
                        pltpu.VMEM((SEQ, CHUNK), F32), pltpu.VMEM((N_CHUNKS, 8, CHUNK), F32),
                        pltpu.VMEM((N_CHUNKS, 8, CHUNK), F32), pltpu.VMEM((N_CHUNKS, 8, CHUNK), F32),
                        pltpu.VMEM((N_CHUNKS, GROUP_W, CHUNK), F32), pltpu.VMEM((N_CHUNKS, GROUP_W, CHUNK), F32),
                        pltpu.VMEM((2, 2, 2 * HEAD_DIM, SSD_STATE), F32)],
        compiler_params=_params("arbitrary"),
        name="ssd_mixer",
    )(zx, sm, conv_w, conv_b.reshape(1, SSD_XBC), tile_col(dt_bias), tile_col(a_log),
      jnp.repeat(d_skip, HEAD_DIM).reshape(1, GROUP_W), norm_g.reshape(1, GROUP_W))


def _na_kernel(qkv_ref, bias_ref, g_ref, o_ref):
    n_rows = SEQ // GRID_W
    k_tok = NA_KH * GRID_W
    pair_w = 2 * HEAD_DIM
    first = lax.broadcasted_iota(jnp.int32, (1, pair_w), 1) < HEAD_DIM
    gain = g_ref[...]

    def row_group(g, _):
        units = [(i, pair) for i in range(NA_ROW_GROUP) for pair in range(N_HEADS // 2)]
        q0, k0, dr0, scores, values = {}, {}, {}, {}, {}
        for i in range(NA_ROW_GROUP):
            r = g * NA_ROW_GROUP + i
            start = jnp.clip(r - NA_KH // 2, 0, n_rows - NA_KH)
            dr0[i] = start - r + NA_KH - 1
            q0[i] = pl.multiple_of(r * GRID_W, GRID_W)
            k0[i] = pl.multiple_of(start * GRID_W, GRID_W)
        for i, pair in units:
            lo = pair * pair_w
            q2 = qkv_ref[0, pl.ds(q0[i], GRID_W), lo:lo + pair_w] * (HEAD_DIM ** -0.5)
            k2 = qkv_ref[0, pl.ds(k0[i], k_tok), GROUP_W + lo:GROUP_W + lo + pair_w]
            values[i, pair] = qkv_ref[0, pl.ds(k0[i], k_tok), 2 * GROUP_W + lo:2 * GROUP_W + lo + pair_w]
            qm = jnp.concatenate([jnp.where(first, q2, 0.0), jnp.where(first, 0.0, q2)], axis=0).astype(BF16)
            scores[i, pair] = lax.dot_general(qm, k2, NT_DIMS, preferred_element_type=F32)
        probs, sums = {}, {}
        for i, pair in units:
            bias = jnp.concatenate(
                [jnp.concatenate([bias_ref[2 * pair + hh, dr0[i] + 2 * m] for m in range(NA_KH // 2)], axis=1)
                 for hh in range(2)], axis=0)
            s = scores[i, pair] + bias
            p = jnp.exp(s - jnp.max(s, axis=-1, keepdims=True))
            probs[i, pair] = p.astype(BF16)
            sums[i, pair] = jnp.sum(p, axis=-1, keepdims=True)
        for i in range(NA_ROW_GROUP):
            outs = []
            for pair in range(N_HEADS // 2):
                o2 = _dot(probs[i, pair], values[i, pair]) / sums[i, pair]
                outs.append(jnp.where(first, o2[0:GRID_W, :], o2[GRID_W:, :]))
            o_ref[0, pl.ds(q0[i], GRID_W), :] = _rms(jnp.concatenate(outs, axis=1)) * gain
        return 0

    lax.fori_loop(0, n_rows // NA_ROW_GROUP, row_group, 0, unroll=2)


def _na_bias_tiles(rpb):
    col = np.arange(GRID_W)
    cs = np.clip(col - NA_KW // 2, 0, GRID_W - NA_KW)
    valid_c = (col[None, :] >= cs[:, None]) & (col[None, :] < cs[:, None] + NA_KW)
    dc = col[None, :] - col[:, None] + NA_KW - 1
    sel_c = (valid_c[None] & (dc[None] == np.arange(2 * NA_KW - 1)[:, None, None])).astype(np.float32)
    by_row = jnp.einsum("hab,bqk->haqk", rpb, sel_c, precision=lax.Precision.HIGHEST)
    by_row = jnp.where(valid_c[None, None], by_row, -1e30)
    return jnp.concatenate([by_row[:, :-1], by_row[:, 1:]], axis=-1).astype(F32)


def _na_mixer(qkv, rpb, norm_g):
    b = qkv.shape[0]
    table = _na_bias_tiles(rpb)
    return pl.pallas_call(
        _na_kernel,
        grid=(b,),
        in_specs=[pl.BlockSpec((1, SEQ, 3 * GROUP_W), lambda i: (i, 0, QKV_NA_BLOCK)),
                  pl.BlockSpec(table.shape, lambda i: (0, 0, 0, 0)),
                  pl.BlockSpec((1, GROUP_W), lambda i: (0, 0))],
        out_specs=pl.BlockSpec((1, SEQ, GROUP_W), lambda i: (i, 0, 0)),
        out_shape=jax.ShapeDtypeStruct((b, SEQ, GROUP_W), F32),
        compiler_params=_params("arbitrary"),
        name="na_mixer",
    )(qkv, table, norm_g.reshape(1, GROUP_W))


ML_STATE_ROWS = 80
ML_PHASE_HEADS = 4


def _ml_kernel(qkv_ref, og_ref, sm_ref, bcol_ref, g_ref, o_ref, bcol_s, rows_s, lrow_s, qt_s, vt_s, htf_s, htb_s,
               c_s, m_s):
    _gate_rows(sm_ref, rows_s, SM_IG, 16)
    raw = rows_s[...].reshape(N_CHUNKS * 16, CHUNK) + bcol_ref[...]
    pre, suf = _chunk_scans(jax.nn.log_sigmoid(raw), 1)
    rid = lax.broadcasted_iota(jnp.int32, (N_CHUNKS * 16, 1), 0) % 16
    scanned = jnp.where(rid < 8 + N_HEADS, pre, suf)
    beta = raw - pltpu.roll(scanned, N_CHUNKS * 16 - 8, 0)
    lrow_s[...] = jnp.where(rid < 8, beta, scanned).reshape(N_CHUNKS, 16, CHUNK)

    pad_rows = jnp.zeros((CHUNK - 8, CHUNK), F32)
    for c in range(N_CHUNKS):
        rows = slice(c * CHUNK, (c + 1) * CHUNK)
        qt_s[c] = qkv_ref[0, rows, 0:GROUP_W].astype(F32).T.astype(BF16)
        vt_s[c] = qkv_ref[0, rows, 2 * GROUP_W:3 * GROUP_W].astype(F32).T.astype(BF16)
        bcol_s[rows, :] = jnp.concatenate([lrow_s[c, 0:8, :], pad_rows], axis=0).T

    si = lax.broadcasted_iota(jnp.int32, (CHUNK, CHUNK), 0)
    ti = lax.broadcasted_iota(jnp.int32, (CHUNK, CHUNK), 1)
    pad = ML_STATE_ROWS - HEAD_DIM
    one_rows = (lax.broadcasted_iota(jnp.int32, (pad, CHUNK), 0) == 0).astype(BF16)
    k_scale = HEAD_DIM ** -0.5
    c_s[...] = jnp.zeros_like(c_s)
    m_s[...] = jnp.zeros_like(m_s)

    def chunk_body(i, _):
        chunk = {0: i, 1: N_CHUNKS - 1 - i}
        rows = {d: pl.multiple_of(chunk[d] * CHUNK, CHUNK) for d in range(2)}
        for d in range(2):
            for h0 in range(0, N_HEADS, ML_PHASE_HEADS):
                phases([(d, h) for h in range(h0, h0 + ML_PHASE_HEADS)], chunk, rows)
        return 0

    def phases(units, chunk, rows):
        ks, qts, vts, cexts, raw_s, inters = {}, {}, {}, {}, {}, {}
        for d, h in units:
            j = d * N_HEADS + h
            k = qkv_ref[0, pl.ds(rows[d], CHUNK), GROUP_W + h * HEAD_DIM:GROUP_W + (h + 1) * HEAD_DIM]
            ks[d, h] = k * k_scale
            qts[d, h] = qt_s[chunk[d], h * HEAD_DIM:(h + 1) * HEAD_DIM, :]
            vts[d, h] = vt_s[chunk[d], h * HEAD_DIM:(h + 1) * HEAD_DIM, :]
            cexts[d, h] = c_s[j]
            raw_s[d, h] = _dot(ks[d, h], qts[d, h])
            inters[d, h] = _dot(cexts[d, h].astype(BF16), qts[d, h])
        s_ts, w_inters, floors, sources, keeps = {}, {}, {}, {}, {}
        for d, h in units:
            j = d * N_HEADS + h
            mask = (si <= ti) if d == 0 else (si >= ti)
            end = CHUNK - 1 if d == 0 else 0
            lrow = lrow_s[chunk[d]]
            beta_r = lrow[j:j + 1, :]
            b_r = lrow[8 + j:8 + j + 1, :]
            m_prev = m_s[j][0:1, 0:1]
            beta_m = jnp.where(mask, bcol_s[pl.ds(rows[d], CHUNK), j:j + 1], -jnp.inf)
            mu = jnp.maximum(m_prev, jnp.max(beta_m, axis=0, keepdims=True))
            s_ts[d, h] = raw_s[d, h] * jnp.exp(beta_m - mu)
            w_inters[d, h] = jnp.exp(m_prev - mu)
            floors[d, h] = jnp.exp(-(b_r + mu))
            top = jnp.maximum(m_prev, jnp.max(beta_r, axis=1, keepdims=True))
            v_ext = jnp.concatenate([vts[d, h], one_rows], axis=0).astype(F32)
            sources[d, h] = (v_ext * jnp.exp(beta_r - top)).astype(BF16)
            keeps[d, h] = jnp.exp(m_prev - top)
            m_s[j] = jnp.broadcast_to(b_r[:, end:end + 1] + top, (8, CHUNK))
        for d, h in units:
            s_t = s_ts[d, h]
            inter = inters[d, h]
            num = w_inters[d, h] * inter[0:HEAD_DIM, :] + _dot(vts[d, h], s_t.astype(BF16))
            den = w_inters[d, h] * inter[HEAD_DIM:HEAD_DIM + 1, :] + jnp.sum(s_t, axis=0, keepdims=True)
            out_s = htf_s if d == 0 else htb_s
            out_s[chunk[d], h * HEAD_DIM:(h + 1) * HEAD_DIM, :] = num / jnp.maximum(jnp.abs(den), floors[d, h])
        for d, h in units:
            c_s[d * N_HEADS + h] = keeps[d, h] * cexts[d, h] + _dot(sources[d, h], ks[d, h])

    lax.fori_loop(0, N_CHUNKS, chunk_body, 0, unroll=2)

    gain = g_ref[...]

    def out_body(c, _):
        r0 = pl.multiple_of(c * CHUNK, CHUNK)
        h_t = htf_s[c] + htb_s[c]
        normed = []
        for h in range(N_HEADS):
            x = h_t[h * HEAD_DIM:(h + 1) * HEAD_DIM, :]
            normed.append(x * lax.rsqrt(jnp.mean(x * x, axis=0, keepdims=True) + NORM_EPS))
        y = jnp.concatenate(normed, axis=0).T * gain
        gate = 0.5 + 0.5 * jnp.tanh(0.5 * og_ref[0, pl.ds(r0, CHUNK), :])
        o_ref[0, pl.ds(r0, CHUNK), :] = gate * y
        return 0

    lax.fori_loop(0, N_CHUNKS, out_body, 0)


def _ml_mixer(qkv, og, sm, i_bias, f_bias, norm_g):
    b = qkv.shape[0]
    bias16 = jnp.concatenate([i_bias.reshape(8), f_bias.reshape(8)])
    bcol = jnp.tile(bias16.reshape(16, 1), (N_CHUNKS, 1))
    full = lambda shape: pl.BlockSpec(shape, lambda i: (0,) * len(shape))
    return pl.pallas_call(
        _ml_kernel,
        grid=(b,),
        in_specs=[pl.BlockSpec((1, SEQ, 3 * GROUP_W), lambda i: (i, 0, QKV_ML_BLOCK)),
                  pl.BlockSpec((1, SEQ, GROUP_W), lambda i: (i, 0, WIDE_MLO_BLOCK)),
                  pl.BlockSpec((1, SEQ, CHUNK), lambda i: (i, 0, WIDE_SM_BLOCK)),
                  full((N_CHUNKS * 16, 1)), full((1, GROUP_W))],
        out_specs=pl.BlockSpec((1, SEQ, GROUP_W), lambda i: (i, 0, 0)),
        out_shape=jax.ShapeDtypeStruct((b, SEQ, GROUP_W), F32),
        scratch_shapes=[pltpu.VMEM((SEQ, CHUNK), F32), pltpu.VMEM((N_CHUNKS, 16, CHUNK), F32),
                        pltpu.VMEM((N_CHUNKS, 16, CHUNK), F32), pltpu.VMEM((N_CHUNKS, GROUP_W, CHUNK), BF16),
                        pltpu.VMEM((N_CHUNKS, GROUP_W, CHUNK), BF16), pltpu.VMEM((N_CHUNKS, GROUP_W, CHUNK), F32),
                        pltpu.VMEM((N_CHUNKS, GROUP_W, CHUNK), F32),
                        pltpu.VMEM((2 * N_HEADS, ML_STATE_ROWS, HEAD_DIM), F32),
                        pltpu.VMEM((2 * N_HEADS, 8, CHUNK), F32)],
        compiler_params=_params("arbitrary"),
        name="mlstm_mixer",
    )(qkv, og, sm, bcol, norm_g.reshape(1, GROUP_W))


HALF = SEQ // 2


def _dft_tables():
    f = np.arange(HALF, dtype=np.int64)[:, None]
    t = np.arange(SEQ, dtype=np.int64)[None, :]
    ang = (2.0 * np.pi / FFT_N) * ((f * t) % FFT_N).astype(np.float64)
    wc = np.cos(ang)
    ws = np.sin(ang)
    ws[0, :] = np.array([1.0, 1.0, -1.0, -1.0])[np.arange(SEQ) % 4]
    tables = []
    for parity in range(2):
        w = np.stack([wc[:, parity::2], ws[:, parity::2]]).reshape(2, HALF // FREQ_SUB, FREQ_SUB, HALF)
        w = np.ascontiguousarray(w.transpose(1, 0, 2, 3).reshape(2 * HALF, HALF)).astype(np.float32)
        tables.append((jnp.asarray(w).astype(BF16), jnp.asarray(np.ascontiguousarray(w.T)).astype(BF16)))
    return tables


def _hy_features():
    t = jnp.linspace(0.0, 1.0, SEQ, dtype=F32)[:, None]
    bands = (HY_EMB - 1) // 2
    f = jnp.linspace(1e-4, bands - 1, bands, dtype=F32)
    ang = (2.0 * math.pi) * (jnp.arange(SEQ, dtype=F32) / SEQ)[:, None] * f[None, :]
    feats = jnp.concatenate([t, jnp.cos(ang), -jnp.sin(ang)], axis=-1)
    deltas = jnp.abs(jnp.linspace(math.log(1e-2) / 1.5, math.log(1e-2) / 0.3, GROUP_W, dtype=F32))
    return jnp.pad(feats, ((0, 0), (0, CHUNK - HY_EMB))), t, deltas.reshape(1, GROUP_W)


def _hy_filter_kernel(feats_ref, w1_ref, b1_ref, w2_ref, b2_ref, w3_ref, freq_ref, t_ref, dl_ref, hp_ref, hm_ref):
    h = jnp.sin(freq_ref[0:1, :] * (_dot(feats_ref[...].astype(BF16), w1_ref[...].astype(BF16)) + b1_ref[...]))
    h = jnp.sin(freq_ref[1:2, :] * (_dot(h.astype(BF16), w2_ref[...].astype(BF16)) + b2_ref[...]))
    h = _dot(h.astype(BF16), w3_ref[...].astype(BF16))
    win = jnp.exp(-t_ref[...] * dl_ref[...])
    row = lax.broadcasted_iota(jnp.int32, (SEQ, 1), 0)
    for o in range(2):
        hf = h[:, (2 * o) * GROUP_W:(2 * o + 1) * GROUP_W] * win
        hb = h[:, (2 * o + 1) * GROUP_W:(2 * o + 2) * GROUP_W] * win
        r = lax.rsqrt(jnp.sum(hf * hf + hb * hb, axis=0, keepdims=True) + NORM_EPS)
        hf = hf * r
        hb = jnp.where(row == 0, 0.0, hb * r)
        hp_ref[:, o * GROUP_W:(o + 1) * GROUP_W] = (hf + hb).astype(BF16)
        hm_ref[:, o * GROUP_W:(o + 1) * GROUP_W] = (hf - hb).astype(BF16)


def _hy_spec_kernel(we_ref, wo_ref, hp_ref, hm_ref, kc_ref, ks_ref, kcp_ref, ksp_ref):
    fb = FREQ_SUB
    nc = 2 * GROUP_W
    row0 = (lax.broadcasted_iota(jnp.int32, (fb, 1), 0) + pl.program_id(0) * fb) == 0
    edge = jnp.where(row0, 1.0 / FFT_N, 2.0 / FFT_N)
    pe = _dot(we_ref[0:fb, :], hp_ref[:, 0:nc])
    po = _dot(wo_ref[0:fb, :], hp_ref[:, nc:])
    me = _dot(we_ref[fb:, :], hm_ref[:, 0:nc])
    mo = _dot(wo_ref[fb:, :], hm_ref[:, nc:])
    mid_cos = _dot(we_ref[fb:fb + 8, :], hp_ref[:, 0:nc])[0:1, :]
    kc_ref[...] = (pe + po) * edge
    kcp_ref[...] = (pe - po) * edge
    ks_ref[...] = jnp.where(row0, mid_cos, me + mo) * (2.0 / FFT_N)
    ksp_ref[...] = jnp.where(row0, mo, mo - me) * (2.0 / FFT_N)


def _hy_spectrum(tables, w1, b1, w2, b2, w3, freq):
    feats, t, deltas = _hy_features()
    w1p = jnp.pad(w1, ((0, CHUNK - HY_EMB), (0, 0)))
    hp, hm = pl.pallas_call(
        _hy_filter_kernel,
        out_shape=[jax.ShapeDtypeStruct((SEQ, 2 * GROUP_W), BF16)] * 2,
        compiler_params=pltpu.CompilerParams(vmem_limit_bytes=VMEM_LIMIT),
        name="hyena_filter",
    )(feats, w1p, b1.reshape(1, HY_FFN), w2, b2.reshape(1, HY_FFN), w3, freq, t, deltas)
    split = lambda h: h.reshape(HALF, 4 * GROUP_W)
    table_spec = pl.BlockSpec((2 * FREQ_SUB, HALF), lambda j: (j, 0))
    return pl.pallas_call(
        _hy_spec_kernel,
        grid=(HALF // FREQ_SUB,),
        in_specs=[table_spec, table_spec,
                  pl.BlockSpec((HALF, 4 * GROUP_W), lambda j: (0, 0)),
                  pl.BlockSpec((HALF, 4 * GROUP_W), lambda j: (0, 0))],
        out_specs=[pl.BlockSpec((FREQ_SUB, 2 * GROUP_W), lambda j: (j, 0))] * 4,
        out_shape=[jax.ShapeDtypeStruct((HALF, 2 * GROUP_W), F32)] * 4,
        compiler_params=_params("arbitrary"),
        name="hyena_spectrum",
    )(tables[0][0], tables[1][0], split(hp), split(hm))


def _split_conv_rows(e_ref, o_ref, b, r0, w, bias):
    e = e_ref[b, pl.ds(r0, CHUNK), :]
    o = o_ref[b, pl.ds(r0, CHUNK), :]
    o_prev = o_ref[b, pl.ds(jnp.maximum(r0 - 1, 0), 1), :]
    e_next = e_ref[b, pl.ds(jnp.minimum(r0 + CHUNK, HALF - 1), 1), :]
    o_prev = jnp.where(r0 > 0, o_prev, 0.0)
    e_next = jnp.where(r0 + CHUNK < HALF, e_next, 0.0)
    row = lax.broadcasted_iota(jnp.int32, (CHUNK, 1), 0)
    o_dn = jnp.where(row == 0, o_prev, pltpu.roll(o, 1, 0))
    e_up = jnp.where(row == CHUNK - 1, e_next, pltpu.roll(e, CHUNK - 1, 0))
    conv_e = w[0:1, :] * o_dn + w[1:2, :] * e + w[2:3, :] * o + bias
    conv_o = w[0:1, :] * e + w[1:2, :] * o + w[2:3, :] * e_up + bias
    return conv_e, conv_o


def _hy_conv_kernel(ze_ref, zo_ref, ge_ref, go_ref, cwz_ref, cbz_ref, cwg_ref, cbg_ref, we_ref, wo_ref, wet_ref, wot_ref,
                    kc_ref, ks_ref, kcp_ref, ksp_ref, skip_ref, g_ref, o_ref, ze_s, zo_s, acce_s, acco_s, *,
                    conv_z, final_norm):
    j = pl.program_id(1)
    n_b = ze_ref.shape[0]
    fb = FREQ_BLOCK
    half_chunks = HALF // CHUNK

    def z_rows(b, r0):
        if conv_z:
            return _split_conv_rows(ze_ref, zo_ref, b, r0, cwz_ref[...], cbz_ref[...])
        return ze_ref[b, pl.ds(r0, CHUNK), :], zo_ref[b, pl.ds(r0, CHUNK), :]

    @pl.when(j == 0)
    def _():
        for b in range(n_b):
            def fill(c, _, b=b):
                r0 = pl.multiple_of(c * CHUNK, CHUNK)
                z_e, z_o = z_rows(b, r0)
                ze_s[pl.ds(r0, CHUNK), b * GROUP_W:(b + 1) * GROUP_W] = z_e.astype(BF16)
                zo_s[pl.ds(r0, CHUNK), b * GROUP_W:(b + 1) * GROUP_W] = z_o.astype(BF16)
                return 0

            lax.fori_loop(0, half_chunks, fill, 0)
        acce_s[...] = jnp.zeros_like(acce_s)
        acco_s[...] = jnp.zeros_like(acco_s)

    sub = FREQ_SUB
    groups = range(fb // sub)
    fwd = []
    for s in groups:
        rows = slice(2 * sub * s, 2 * sub * (s + 1))
        fwd.append((_dot(we_ref[rows, :], ze_s[...]), _dot(wo_ref[rows, :], zo_s[...])))
    inv_e, inv_o = [], []
    for s in groups:
        row0 = (lax.broadcasted_iota(jnp.int32, (sub, 1), 0) + j * fb + s * sub) == 0
        tiled = lambda ref: jnp.concatenate([ref[sub * s:sub * (s + 1), :]] * n_b, axis=1)
        kc, ks, kcp, ksp = tiled(kc_ref), tiled(ks_ref), tiled(kcp_ref), tiled(ksp_ref)
        xe, xo = fwd[s]
        ae, be, ao, bo = xe[0:sub, :], xe[sub:, :], xo[0:sub, :], xo[sub:, :]
        xc, xcp, xs, xsp = ae + ao, ae - ao, be + bo, bo - be
        yc = xc * kc - xs * ks
        ys = xc * ks + xs * kc
        ycp = xcp * kcp - xsp * ksp
        ysp = xcp * ksp + xsp * kcp
        dc, nyq = xc * kc, xcp * kcp
        pc = jnp.where(row0, dc + nyq, yc + ycp)
        mc = jnp.where(row0, dc - nyq, yc - ycp)
        ms = jnp.where(row0, be * ks - bo * ksp, ys - ysp)
        ps = jnp.where(row0, be * ksp + bo * ks, ys + ysp)
        cols = slice(2 * sub * s, 2 * sub * (s + 1))
        inv_e.append(_dot(wet_ref[:, cols], jnp.concatenate([pc, ms], axis=0).astype(BF16)))
        inv_o.append(_dot(wot_ref[:, cols], jnp.concatenate([mc, ps], axis=0).astype(BF16)))
    acce_s[...] += sum(inv_e)
    acco_s[...] += sum(inv_o)

    @pl.when(j == pl.num_programs(1) - 1)
    def _():
        for b in range(n_b):
            def finish(c, _, b=b):
                r0 = pl.multiple_of(c * CHUNK, CHUNK)
                gates = _split_conv_rows(ge_ref, go_ref, b, r0, cwg_ref[...], cbg_ref[...])
                for parity, (gate, z, acc) in enumerate(zip(gates, z_rows(b, r0), (acce_s, acco_s))):
                    r = gate * (acc[pl.ds(r0, CHUNK), b * GROUP_W:(b + 1) * GROUP_W] + skip_ref[...] * z)
                    if final_norm:
                        r = _rms(r) * g_ref[...]
                    o_ref[b, pl.ds(r0, CHUNK), parity * GROUP_W:(parity + 1) * GROUP_W] = r
                return 0

            lax.fori_loop(0, half_chunks, finish, 0)


def _hy_long_conv(z, z_cols, gates, gate_cols, conv_w, conv_b, conv_z, tables, spectrum, order, skip, norm_g,
                  final_norm):
    b = z.shape[0]
    bg = HY_BATCH_BLOCK
    taps = lambda col: (conv_w[:, col * GROUP_W:(col + 1) * GROUP_W],
                        conv_b[col * GROUP_W:(col + 1) * GROUP_W].reshape(1, GROUP_W))
    small = lambda shape: pl.BlockSpec(shape, lambda g, j: (0, 0))
    act = lambda col: pl.BlockSpec((bg, HALF, GROUP_W), lambda g, j: (g, 0, col))
    fwd = pl.BlockSpec((2 * FREQ_BLOCK, HALF), lambda g, j: (j, 0))
    inv = pl.BlockSpec((HALF, 2 * FREQ_BLOCK), lambda g, j: (0, j))
    spec = pl.BlockSpec((FREQ_BLOCK, GROUP_W), lambda g, j: (j, order))
    (we, wet), (wo, wot) = tables
    return pl.pallas_call(
        functools.partial(_hy_conv_kernel, conv_z=conv_z, final_norm=final_norm),
        grid=(b // bg, N_FREQ_BLOCKS),
        in_specs=[act(z_cols[0]), act(z_cols[1]), act(gate_cols[0]), act(gate_cols[1]),
                  small((3, GROUP_W)), small((1, GROUP_W)), small((3, GROUP_W)), small((1, GROUP_W)),
                  fwd, fwd, inv, inv, spec, spec, spec, spec, small((1, GROUP_W)), small((1, GROUP_W))],
        out_specs=pl.BlockSpec((bg, HALF, 2 * GROUP_W), lambda g, j: (g, 0, 0)),
        out_shape=jax.ShapeDtypeStruct((b, HALF, 2 * GROUP_W), F32),
        scratch_shapes=[pltpu.VMEM((HALF, bg * GROUP_W), BF16), pltpu.VMEM((HALF, bg * GROUP_W), BF16),
                        pltpu.VMEM((HALF, bg * GROUP_W), F32), pltpu.VMEM((HALF, bg * GROUP_W), F32)],
        compiler_params=_params("arbitrary", "arbitrary"),
        name="hyena_long_conv",
    )(z, z, gates, gates, *taps(0), *taps(order + 1), we, wo, wet, wot, *spectrum,
      skip.reshape(1, GROUP_W), norm_g.reshape(1, GROUP_W))


def _hy_mixer(u, tables, conv_w, conv_b, w1, b1, w2, b2, w3, freq, skip, norm_g):
    spectrum = _hy_spectrum(tables, w1, b1, w2, b2, w3, freq)
    z1 = _hy_long_conv(u, (0, 3), u, (1, 4), conv_w, conv_b, True, tables, spectrum, 0, skip[0], norm_g, False)
    return _hy_long_conv(z1, (0, 1), u, (2, 5), conv_w, conv_b, False, tables, spectrum, 1, skip[1], norm_g, True)


FFN_SPLITS = ((0, 1024), (1024, 2048), (2048, FFN_HIDDEN))


def _ffn_kernel(x_ref, mod_ref, ya_ref, yb_ref, yc_ref, yd_ref, perm_ref, wo_ref, wg_ref, wu_ref, wd_ref, gf_ref, o_ref,
                *, final_norm):
    yd = yd_ref[...]
    yd = jnp.concatenate([yd[:, 0:GROUP_W], yd[:, GROUP_W:]], axis=0).astype(BF16)
    yd = _dot(perm_ref[...], yd)
    y_mix = jnp.concatenate([ya_ref[...], yb_ref[...], yc_ref[...], yd], axis=1).astype(BF16)
    x = x_ref[...] + mod_ref[0, 2:3, :] * _dot(y_mix, wo_ref[...])
    h = (_rms(x) * (1.0 + mod_ref[0, 4:5, :]) + mod_ref[0, 3:4, :]).astype(BF16)
    acc = jnp.zeros_like(x)
    for lo, hi in FFN_SPLITS:
        a = _silu(_dot(h, wg_ref[:, lo:hi])) * _dot(h, wu_ref[:, lo:hi])
        acc = acc + _dot(a.astype(BF16), wd_ref[lo:hi, :])
    y = x + mod_ref[0, 5:6, :] * acc
    if final_norm:
        y = _rms(y) * gf_ref[...]
    o_ref[...] = y


def _out_ffn(x2, mod_l, ys, w_out, wg, wu, wd, layer, final_g, final_norm):
    n_tok, d = x2.shape
    tm = TOKEN_TILE
    tiles_per_seq = SEQ // tm
    tok = np.arange(tm)
    perm = (np.arange(tm)[None, :] == (tok // 2 + (tok % 2) * (tm // 2))[:, None]).astype(np.float32)
    resident = lambda shape: pl.BlockSpec((None,) + shape, lambda i: (layer, 0, 0), pipeline_mode=pl.Buffered(1))
    return pl.pallas_call(
        functools.partial(_ffn_kernel, final_norm=final_norm),
        grid=(n_tok // tm,),
        in_specs=[pl.BlockSpec((tm, d), lambda i: (i, 0)),
                  pl.BlockSpec((1, 6, d), lambda i: (i // tiles_per_seq, 0, 0))]
                 + [pl.BlockSpec((tm, GROUP_W), lambda i: (i, 0))] * 3
                 + [pl.BlockSpec((tm // 2, 2 * GROUP_W), lambda i: (i, 0)),
                    pl.BlockSpec((tm, tm), lambda i: (0, 0), pipeline_mode=pl.Buffered(1)),
                    resident((d, d)), resident((d, FFN_HIDDEN)), resident((d, FFN_HIDDEN)), resident((FFN_HIDDEN, d)),
                    pl.BlockSpec((1, d), lambda i: (0, 0))],
        out_specs=pl.BlockSpec((tm, d), lambda i: (i, 0)),
        out_shape=jax.ShapeDtypeStruct((n_tok, d), F32),
        compiler_params=_params("arbitrary"),
        name="out_proj_swiglu",
    )(x2, mod_l, *ys, jnp.asarray(perm).astype(BF16), w_out, wg, wu, wd, final_g.reshape(1, d))


def _permute_w_in(w):
    w = w.astype(BF16)
    cols = [w[..., 0:1024], w[..., 2568:2824], w[..., 1024:1032], w[..., 2824:2840],
            jnp.zeros(w.shape[:-1] + (P_PAD - 3608,), BF16), w[..., 1032:1800], w[..., 1800:2568], w[..., 2840:3608]]
    return jnp.concatenate(cols, axis=-1)


def kernel(x, c, mod_w, mod_b, w_in, ssd_conv_w, ssd_conv_b, ssd_dt_bias, ssd_a_log, ssd_d, ssd_norm_g, na_rpb, na_norm_g, ml_i_bias, ml_f_bias, ml_norm_g, hy_conv_w, hy_conv_b, hy_w1, hy_b1, hy_w2, hy_b2, hy_w3, hy_freq, hy_skip, hy_norm_g, w_out, ffn_w_gate, ffn_w_up, ffn_w_down, final_norm_g):
    b, seq, d = x.shape
    assert seq == SEQ and d == D_MODEL and b % HY_BATCH_BLOCK == 0
    depth = mod_w.shape[0]
    mod = _modulation(c, mod_w, mod_b).reshape(depth, b, 6, d)
    dft = _dft_tables()
    w_in_p = _permute_w_in(w_in)
    w_out_b, w_gate_b, w_up_b, w_down_b = (w.astype(BF16) for w in (w_out, ffn_w_gate, ffn_w_up, ffn_w_down))
    x2 = x.reshape(b * seq, d)
    for l in range(depth):
        wide, qkv, hy = _in_proj(x2, mod[l], w_in_p, l)
        wide = wide.reshape(b, seq, WIDE_W)
        qkv = qkv.reshape(b, seq, QKV_W)
        y_ssd = _ssd_mixer(wide, wide, ssd_conv_w[l], ssd_conv_b[l], ssd_dt_bias[l], ssd_a_log[l], ssd_d[l],
                           ssd_norm_g[l])
        y_na = _na_mixer(qkv, na_rpb[l], na_norm_g[l])
        y_ml = _ml_mixer(qkv, wide, wide, ml_i_bias[l], ml_f_bias[l], ml_norm_g[l])
        y_hy = _hy_mixer(hy.reshape(b, HALF, 2 * HY_W), dft, hy_conv_w[l], hy_conv_b[l], hy_w1[l], hy_b1[l], hy_w2[l],
                         hy_b2[l], hy_w3[l], hy_freq[l], hy_skip[l], hy_norm_g[l])
        ys = [t.reshape(b * seq, GROUP_W) for t in (y_ssd, y_na, y_ml)] + [y_hy.reshape(b * HALF, 2 * GROUP_W)]
        x2 = _out_ffn(x2, mod[l], ys, w_out_b, w_gate_b, w_up_b, w_down_b, l, final_norm_g, l == depth - 1)
    return x2.reshape(b, seq, d)
```

```python
import functools
import math

import numpy as np
import jax
import jax.numpy as jnp
from jax import lax
from jax.experimental import pallas as pl
from jax.experimental.pallas import tpu as pltpu

F32 = jnp.float32
BF16 = jnp.bfloat16

D_MODEL = 1024
SEQ = 2048
GRID_W = 64
GROUP_W = 256
HEAD_DIM = 64
N_HEADS = 4
SSD_STATE = 128
SSD_XBC = 768
CHUNK = 128
N_CHUNKS = SEQ // CHUNK
NA_KH = 8
NA_KW = 16
NA_ROW_GROUP = 4
HY_EMB = 33
HY_FFN = 64
FFN_HIDDEN = 2816
NORM_EPS = 1e-6
FFT_N = 2 * SEQ
FREQ_SUB = 128
FREQ_BLOCK = 256
N_FREQ_BLOCKS = SEQ // 2 // FREQ_BLOCK
HY_BATCH_BLOCK = 2
TOKEN_TILE = 512
IN_PROJ_TILE = 1024
VMEM_LIMIT = 56 * 1024 * 1024

WIDE_W = 1408
WIDE_ZX_BLOCK = 0
WIDE_MLO_BLOCK = 4
WIDE_SM_BLOCK = 10
QKV_W = 1536
QKV_NA_BLOCK = 0
QKV_ML_BLOCK = 1
HY_W = 768
P_PAD = WIDE_W + QKV_W + HY_W
SM_DT = 0
SM_IG = 8
SM_FG = 16

NT_DIMS = (((1,), (1,)), ((), ()))
TN_DIMS = (((0,), (0,)), ((), ()))


def _rms(x):
    return x * lax.rsqrt(jnp.mean(x * x, axis=-1, keepdims=True) + NORM_EPS)


def _silu(x):
    half = 0.5 * x
    return half + half * jnp.tanh(half)


def _dot(a, b):
    return jnp.dot(a, b, preferred_element_type=F32)


def _params(*sem):
    return pltpu.CompilerParams(dimension_semantics=sem, vmem_limit_bytes=VMEM_LIMIT)


def _mod_kernel(c_ref, w_ref, b_ref, o_ref):
    cond = _silu(c_ref[...]).astype(BF16)
    o_ref[0] = _dot(cond, w_ref[0].astype(BF16)) + b_ref[0]


def _modulation(c, mod_w, mod_b):
    depth, d, n = mod_w.shape
    b = c.shape[0]
    wn = n // 2
    return pl.pallas_call(
        _mod_kernel,
        grid=(depth, n // wn),
        in_specs=[pl.BlockSpec((b, d), lambda l, j: (0, 0)),
                  pl.BlockSpec((1, d, wn), lambda l, j: (l, 0, j)),
                  pl.BlockSpec((1, 1, wn), lambda l, j: (l, 0, j))],
        out_specs=pl.BlockSpec((1, b, wn), lambda l, j: (l, 0, j)),
        out_shape=jax.ShapeDtypeStruct((depth, b, n), F32),
        compiler_params=_params("arbitrary", "arbitrary"),
        name="adaln_mod",
    )(c, mod_w, mod_b.reshape(depth, 1, n))


def _in_proj_kernel(x_ref, mod_ref, w_ref, wide_ref, qkv_ref, hy_ref, cols_s):
    h = (_rms(x_ref[...]) * (1.0 + mod_ref[0, 1:2, :]) + mod_ref[0, 0:1, :]).astype(BF16)
    wide_ref[...] = _dot(h, w_ref[:, 0:WIDE_W])
    qkv_ref[...] = _dot(h, w_ref[:, WIDE_W:WIDE_W + QKV_W]).astype(BF16)
    half = x_ref.shape[0] // 2
    n_slabs = cols_s.shape[0]
    for cb in range(n_slabs):
        cols_s[cb] = h[:, cb * CHUNK:(cb + 1) * CHUNK].astype(F32)
    w_hy = w_ref[:, WIDE_W + QKV_W:P_PAD]
    for parity in range(2):
        rows = jnp.concatenate([cols_s[cb, pl.ds(parity, half, stride=2), :] for cb in range(n_slabs)], axis=1)
        hy_ref[:, parity * HY_W:(parity + 1) * HY_W] = _dot(rows.astype(BF16), w_hy)


def _in_proj(x2, mod_l, w_perm, layer):
    n_tok, d = x2.shape
    tm = IN_PROJ_TILE
    tiles_per_seq = SEQ // tm
    return pl.pallas_call(
        _in_proj_kernel,
        grid=(n_tok // tm,),
        in_specs=[pl.BlockSpec((tm, d), lambda i: (i, 0)),
                  pl.BlockSpec((1, 6, d), lambda i: (i // tiles_per_seq, 0, 0)),
                  pl.BlockSpec((None, d, P_PAD), lambda i: (layer, 0, 0), pipeline_mode=pl.Buffered(1))],
        out_specs=[pl.BlockSpec((tm, WIDE_W), lambda i: (i, 0)), pl.BlockSpec((tm, QKV_W), lambda i: (i, 0)),
                   pl.BlockSpec((tm // 2, 2 * HY_W), lambda i: (i, 0))],
        out_shape=[jax.ShapeDtypeStruct((n_tok, WIDE_W), F32), jax.ShapeDtypeStruct((n_tok, QKV_W), BF16),
                   jax.ShapeDtypeStruct((n_tok // 2, 2 * HY_W), F32)],
        scratch_shapes=[pltpu.VMEM((d // CHUNK, tm, CHUNK), F32)],
        compiler_params=_params("arbitrary"),
        name="in_proj",
    )(x2, mod_l, w_perm)


def _dwconv3_rows(src_ref, b, r0, cols, w, bias):
    x = src_ref[b, pl.ds(r0, CHUNK), cols]
    up = src_ref[b, pl.ds(jnp.maximum(r0 - 1, 0), 1), cols]
    dn = src_ref[b, pl.ds(jnp.minimum(r0 + CHUNK, SEQ - 1), 1), cols]
    up = jnp.where(r0 > 0, up, 0.0)
    dn = jnp.where(r0 + CHUNK < SEQ, dn, 0.0)
    row = lax.broadcasted_iota(jnp.int32, (CHUNK, 1), 0)
    prev = jnp.where(row == 0, up, pltpu.roll(x, 1, 0))
    nxt = jnp.where(row == CHUNK - 1, dn, pltpu.roll(x, CHUNK - 1, 0))
    return w[0:1, :] * prev + w[1:2, :] * x + w[2:3, :] * nxt + bias


def _chunk_scans(a, axis):
    n = a.shape[axis]
    shape = [1, 1]
    shape[axis] = n
    pos = lax.broadcasted_iota(jnp.int32, tuple(shape), axis) % CHUNK
    pre, suf = a, a
    k = 1
    while k < CHUNK:
        pre = pre + jnp.where(pos >= k, pltpu.roll(pre, k, axis), 0.0)
        suf = suf + jnp.where(pos < CHUNK - k, pltpu.roll(suf, n - k, axis), 0.0)
        k *= 2
    return pre, suf


def _gate_rows(sm_ref, rows_s, lo, n):
    for c in range(N_CHUNKS):
        t = sm_ref[0, c * CHUNK:(c + 1) * CHUNK, :].T
        rows_s[c] = t[lo:lo + n, :]


def _ssd_kernel(zx_ref, sm_ref, cw_ref, cb_ref, dtb_col_ref, a_col_ref, dsk_ref, g_ref,
                o_ref, xs_s, b_s, xst_s, ct_s, acol_s, rows_s, dtrow_s, arow_s, ytf_s, ytb_s, st_s):
    cw = cw_ref[...]
    cb = cb_ref[...]

    def conv_body(c, _):
        r0 = pl.multiple_of(c * CHUNK, CHUNK)
        v = _silu(_dwconv3_rows(zx_ref, 0, r0, slice(GROUP_W, GROUP_W + SSD_XBC), cw, cb))
        xs_s[pl.ds(r0, CHUNK), :] = v[:, 0:GROUP_W]
        b_s[pl.ds(r0, CHUNK), :] = v[:, GROUP_W:2 * GROUP_W].astype(BF16)
        xst_s[c] = v[:, 0:GROUP_W].T
        ct_s[c] = v[:, 2 * GROUP_W:3 * GROUP_W].astype(BF16).T
        return 0

    lax.fori_loop(0, N_CHUNKS, conv_body, 0)

    _gate_rows(sm_ref, rows_s, SM_DT, 8)
    dt_row = jax.nn.softplus(rows_s[...].reshape(N_CHUNKS * 8, CHUNK) + dtb_col_ref[...])
    pre, suf = _chunk_scans(dt_row * (-jnp.exp(a_col_ref[...])), 1)
    rid = lax.broadcasted_iota(jnp.int32, (N_CHUNKS * 8, 1), 0) % 8
    dtrow_s[...] = dt_row.reshape(N_CHUNKS, 8, CHUNK)
    arow_s[...] = jnp.where(rid < N_HEADS, pre, suf).reshape(N_CHUNKS, 8, CHUNK)
    pad_rows = jnp.zeros((CHUNK - 8, CHUNK), F32)
    for c in range(N_CHUNKS):
        acol_s[c * CHUNK:(c + 1) * CHUNK, :] = jnp.concatenate([arow_s[c], pad_rows], axis=0).T

    si = lax.broadcasted_iota(jnp.int32, (CHUNK, CHUNK), 0)
    ti = lax.broadcasted_iota(jnp.int32, (CHUNK, CHUNK), 1)
    first = lax.broadcasted_iota(jnp.int32, (2 * HEAD_DIM, 1), 0) < HEAD_DIM
    st_s[...] = jnp.zeros_like(st_s)

    def chunk_body(i, _):
        units = [(d, g) for d in range(2) for g in range(2)]
        chunk = {0: i, 1: N_CHUNKS - 1 - i}
        rows = {d: pl.multiple_of(chunk[d] * CHUNK, CHUNK) for d in range(2)}
        bgs, cgs, prevs, scores, y_offs = {}, {}, {}, {}, {}
        for d, g in units:
            bgs[d, g] = b_s[pl.ds(rows[d], CHUNK), g * SSD_STATE:(g + 1) * SSD_STATE]
            cgs[d, g] = ct_s[chunk[d], g * SSD_STATE:(g + 1) * SSD_STATE, :]
            prevs[d, g] = st_s[d, g]
            scores[d, g] = _dot(bgs[d, g], cgs[d, g])
            y_offs[d, g] = _dot(prevs[d, g].astype(BF16), cgs[d, g])
        weighted, xdts, x_ends, carries, grow = {}, {}, {}, {}, {}
        for d, g in units:
            mask = (si <= ti) if d == 0 else (si >= ti)
            end = CHUNK - 1 if d == 0 else 0
            acol = acol_s[pl.ds(rows[d], CHUNK), :]
            arow = arow_s[chunk[d]]
            dtr = dtrow_s[chunk[d]]
            for hh in range(2):
                h = 2 * g + hh
                j = d * N_HEADS + h
                ar = arow[j:j + 1, :]
                a_end = ar[:, end:end + 1]
                decay = jnp.exp(jnp.where(mask, ar - acol[:, j:j + 1], -jnp.inf))
                xdt_t = xst_s[chunk[d], h * HEAD_DIM:(h + 1) * HEAD_DIM, :] * dtr[j:j + 1, :]
                weighted[d, h] = (scores[d, g] * decay).astype(BF16)
                xdts[d, h] = xdt_t.astype(BF16)
                x_ends[d, h] = (xdt_t * jnp.exp(a_end - ar)).astype(BF16)
                carries[d, h] = jnp.exp(a_end)
                grow[d, h] = jnp.exp(ar)
        for d in range(2):
            ys = []
            for h in range(N_HEADS):
                g, hh = divmod(h, 2)
                y_off = y_offs[d, g][hh * HEAD_DIM:(hh + 1) * HEAD_DIM, :]
                ys.append(_dot(xdts[d, h], weighted[d, h]) + y_off * grow[d, h])
            y_t = jnp.concatenate(ys, axis=0)
            if d == 0:
                ytf_s[chunk[d]] = y_t
            else:
                ytb_s[chunk[d]] = y_t
        for d, g in units:
            keep = jnp.where(first, carries[d, 2 * g], carries[d, 2 * g + 1])
            x_end = jnp.concatenate([x_ends[d, 2 * g], x_ends[d, 2 * g + 1]], axis=0)
            st_s[d, g] = prevs[d, g] * keep + _dot(x_end, bgs[d, g])
        return 0

    lax.fori_loop(0, N_CHUNKS, chunk_body, 0, unroll=8)

    dsk = dsk_ref[...]
    gain = g_ref[...]

    def out_body(c, _):
        r0 = pl.multiple_of(c * CHUNK, CHUNK)
        y = (ytf_s[c] + ytb_s[c]).T + dsk * xs_s[pl.ds(r0, CHUNK), :]
        y = y * _silu(zx_ref[0, pl.ds(r0, CHUNK), 0:GROUP_W])
        o_ref[0, pl.ds(r0, CHUNK), :] = _rms(y) * gain
        return 0

    lax.fori_loop(0, N_CHUNKS, out_body, 0, unroll=2)


def _ssd_mixer(zx, sm, conv_w, conv_b, dt_bias, a_log, d_skip, norm_g):
    b = zx.shape[0]
    tile_col = lambda v: jnp.tile(v.reshape(8, 1), (N_CHUNKS, 1))
    full = lambda shape: pl.BlockSpec(shape, lambda i: (0,) * len(shape))
    return pl.pallas_call(
        _ssd_kernel,
        grid=(b,),
        in_specs=[pl.BlockSpec((1, SEQ, 1024), lambda i: (i, 0, WIDE_ZX_BLOCK)),
                  pl.BlockSpec((1, SEQ, CHUNK), lambda i: (i, 0, WIDE_SM_BLOCK)),
                  full((3, SSD_XBC)), full((1, SSD_XBC)),
                  full((N_CHUNKS * 8, 1)), full((N_CHUNKS * 8, 1)), full((1, GROUP_W)), full((1, GROUP_W))],
        out_specs=pl.BlockSpec((1, SEQ, GROUP_W), lambda i: (i, 0, 0)),
        out_shape=jax.ShapeDtypeStruct((b, SEQ, GROUP_W), F32),
        scratch_shapes=[pltpu.VMEM((SEQ, GROUP_W), F32), pltpu.VMEM((SEQ, GROUP_W), BF16),
                        pltpu.VMEM((N_CHUNKS, GROUP_W, CHUNK), F32), pltpu.VMEM((N_CHUNKS, GROUP_W, CHUNK), BF16),
                        pltpu.VMEM((SEQ, CHUNK), F32), pltpu.VMEM((N_CHUNKS, 8, CHUNK), F32),
                        pltpu.VMEM((N_CHUNKS, 8, CHUNK), F32), pltpu.VMEM((N_CHUNKS, 8, CHUNK), F32),
                        pltpu.VMEM((N_CHUNKS, GROUP_W, CHUNK), F32), pltpu.VMEM((N_CHUNKS, GROUP_W, CHUNK), F32),
                        pltpu.VMEM((2, 2, 2 * HEAD_DIM, SSD_STATE), F32)],
        compiler_params=_params("arbitrary"),
        name="ssd_mixer",
    )(zx, sm, conv_w, conv_b.reshape(1, SSD_XBC), tile_col(dt_bias), tile_col(a_log),
      jnp.repeat(d_skip, HEAD_DIM).reshape(1, GROUP_W), norm_g.reshape(1, GROUP_W))


def _na_kernel(qkv_ref, bias_ref, g_ref, o_ref):
    n_rows = SEQ // GRID_W
    k_tok = NA_KH * GRID_W
    pair_w = 2 * HEAD_DIM
    first = lax.broadcasted_iota(jnp.int32, (1, pair_w), 1) < HEAD_DIM
    gain = g_ref[...]

    def row_group(g, _):
        units = [(i, pair) for i in range(NA_ROW_GROUP) for pair in range(N_HEADS // 2)]
        q0, k0, dr0, scores, values = {}, {}, {}, {}, {}
        for i in range(NA_ROW_GROUP):
            r = g * NA_ROW_GROUP + i
            start = jnp.clip(r - NA_KH // 2, 0, n_rows - NA_KH)
            dr0[i] = start - r + NA_KH - 1
            q0[i] = pl.multiple_of(r * GRID_W, GRID_W)
            k0[i] = pl.multiple_of(start * GRID_W, GRID_W)
        for i, pair in units:
            lo = pair * pair_w
            q2 = qkv_ref[0, pl.ds(q0[i], GRID_W), lo:lo + pair_w] * (HEAD_DIM ** -0.5)
            k2 = qkv_ref[0, pl.ds(k0[i], k_tok), GROUP_W + lo:GROUP_W + lo + pair_w]
            values[i, pair] = qkv_ref[0, pl.ds(k0[i], k_tok), 2 * GROUP_W + lo:2 * GROUP_W + lo + pair_w]
            qm = jnp.concatenate([jnp.where(first, q2, 0.0), jnp.where(first, 0.0, q2)], axis=0).astype(BF16)
            scores[i, pair] = lax.dot_general(qm, k2, NT_DIMS, preferred_element_type=F32)
        probs, sums = {}, {}
        for i, pair in units:
            bias = jnp.concatenate(
                [jnp.concatenate([bias_ref[2 * pair + hh, dr0[i] + 2 * m] for m in range(NA_KH // 2)], axis=1)
                 for hh in range(2)], axis=0)
            s = scores[i, pair] + bias
            p = jnp.exp(s - jnp.max(s, axis=-1, keepdims=True))
            probs[i, pair] = p.astype(BF16)
            sums[i, pair] = jnp.sum(p, axis=-1, keepdims=True)
        for i in range(NA_ROW_GROUP):
            outs = []
            for pair in range(N_HEADS // 2):
                o2 = _dot(probs[i, pair], values[i, pair]) / sums[i, pair]
                outs.append(jnp.where(first, o2[0:GRID_W, :], o2[GRID_W:, :]))
            o_ref[0, pl.ds(q0[i], GRID_W), :] = _rms(jnp.concatenate(outs, axis=1)) * gain
        return 0

    lax.fori_loop(0, n_rows // NA_ROW_GROUP, row_group, 0, unroll=2)


def _na_bias_tiles(rpb):
    col = np.arange(GRID_W)
    cs = np.clip(col - NA_KW // 2, 0, GRID_W - NA_KW)
    valid_c = (col[None, :] >= cs[:, None]) & (col[None, :] < cs[:, None] + NA_KW)
    dc = col[None, :] - col[:, None] + NA_KW - 1
    sel_c = (valid_c[None] & (dc[None] == np.arange(2 * NA_KW - 1)[:, None, None])).astype(np.float32)
    by_row = jnp.einsum("hab,bqk->haqk", rpb, sel_c, precision=lax.Precision.HIGHEST)
    by_row = jnp.where(valid_c[None, None], by_row, -1e30)
    return jnp.concatenate([by_row[:, :-1], by_row[:, 1:]], axis=-1).astype(F32)


def _na_mixer(qkv, rpb, norm_g):
    b = qkv.shape[0]
    table = _na_bias_tiles(rpb)
    return pl.pallas_call(
        _na_kernel,
        grid=(b,),
        in_specs=[pl.BlockSpec((1, SEQ, 3 * GROUP_W), lambda i: (i, 0, QKV_NA_BLOCK)),
                  pl.BlockSpec(table.shape, lambda i: (0, 0, 0, 0)),
                  pl.BlockSpec((1, GROUP_W), lambda i: (0, 0))],
        out_specs=pl.BlockSpec((1, SEQ, GROUP_W), lambda i: (i, 0, 0)),
        out_shape=jax.ShapeDtypeStruct((b, SEQ, GROUP_W), F32),
        compiler_params=_params("arbitrary"),
        name="na_mixer",
    )(qkv, table, norm_g.reshape(1, GROUP_W))


ML_STATE_ROWS = 80
ML_PHASE_HEADS = 4


def _ml_kernel(qkv_ref, og_ref, sm_ref, bcol_ref, g_ref, o_ref, bcol_s, rows_s, lrow_s, qt_s, vt_s, htf_s, htb_s,
               c_s, m_s):
    _gate_rows(sm_ref, rows_s, SM_IG, 16)
    raw = rows_s[...].reshape(N_CHUNKS * 16, CHUNK) + bcol_ref[...]
    pre, suf = _chunk_scans(jax.nn.log_sigmoid(raw), 1)
    rid = lax.broadcasted_iota(jnp.int32, (N_CHUNKS * 16, 1), 0) % 16
    scanned = jnp.where(rid < 8 + N_HEADS, pre, suf)
    beta = raw - pltpu.roll(scanned, N_CHUNKS * 16 - 8, 0)
    lrow_s[...] = jnp.where(rid < 8, beta, scanned).reshape(N_CHUNKS, 16, CHUNK)

    pad_rows = jnp.zeros((CHUNK - 8, CHUNK), F32)
    for c in range(N_CHUNKS):
        rows = slice(c * CHUNK, (c + 1) * CHUNK)
        qt_s[c] = qkv_ref[0, rows, 0:GROUP_W].astype(F32).T.astype(BF16)
        vt_s[c] = qkv_ref[0, rows, 2 * GROUP_W:3 * GROUP_W].astype(F32).T.astype(BF16)
        bcol_s[rows, :] = jnp.concatenate([lrow_s[c, 0:8, :], pad_rows], axis=0).T

    si = lax.broadcasted_iota(jnp.int32, (CHUNK, CHUNK), 0)
    ti = lax.broadcasted_iota(jnp.int32, (CHUNK, CHUNK), 1)
    pad = ML_STATE_ROWS - HEAD_DIM
    one_rows = (lax.broadcasted_iota(jnp.int32, (pad, CHUNK), 0) == 0).astype(BF16)
    k_scale = HEAD_DIM ** -0.5
    c_s[...] = jnp.zeros_like(c_s)
    m_s[...] = jnp.zeros_like(m_s)

    def chunk_body(i, _):
        chunk = {0: i, 1: N_CHUNKS - 1 - i}
        rows = {d: pl.multiple_of(chunk[d] * CHUNK, CHUNK) for d in range(2)}
        for d in range(2):
            for h0 in range(0, N_HEADS, ML_PHASE_HEADS):
                phases([(d, h) for h in range(h0, h0 + ML_PHASE_HEADS)], chunk, rows)
        return 0

    def phases(units, chunk, rows):
        ks, qts, vts, cexts, raw_s, inters = {}, {}, {}, {}, {}, {}
        for d, h in units:
            j = d * N_HEADS + h
            k = qkv_ref[0, pl.ds(rows[d], CHUNK), GROUP_W + h * HEAD_DIM:GROUP_W + (h + 1) * HEAD_DIM]
            ks[d, h] = k * k_scale
            qts[d, h] = qt_s[chunk[d], h * HEAD_DIM:(h + 1) * HEAD_DIM, :]
            vts[d, h] = vt_s[chunk[d], h * HEAD_DIM:(h + 1) * HEAD_DIM, :]
            cexts[d, h] = c_s[j]
            raw_s[d, h] = _dot(ks[d, h], qts[d, h])
            inters[d, h] = _dot(cexts[d, h].astype(BF16), qts[d, h])
        s_ts, w_inters, floors, sources, keeps = {}, {}, {}, {}, {}
        for d, h in units:
            j = d * N_HEADS + h
            mask = (si <= ti) if d == 0 else (si >= ti)
            end = CHUNK - 1 if d == 0 else 0
            lrow = lrow_s[chunk[d]]
            beta_r = lrow[j:j + 1, :]
            b_r = lrow[8 + j:8 + j + 1, :]
            m_prev = m_s[j][0:1, 0:1]
            beta_m = jnp.where(mask, bcol_s[pl.ds(rows[d], CHUNK), j:j + 1], -jnp.inf)
            mu = jnp.maximum(m_prev, jnp.max(beta_m, axis=0, keepdims=True))
            s_ts[d, h] = raw_s[d, h] * jnp.exp(beta_m - mu)
            w_inters[d, h] = jnp.exp(m_prev - mu)
            floors[d, h] = jnp.exp(-(b_r + mu))
            top = jnp.maximum(m_prev, jnp.max(beta_r, axis=1, keepdims=True))
            v_ext = jnp.concatenate([vts[d, h], one_rows], axis=0).astype(F32)
            sources[d, h] = (v_ext * jnp.exp(beta_r - top)).astype(BF16)
            keeps[d, h] = jnp.exp(m_prev - top)
            m_s[j] = jnp.broadcast_to(b_r[:, end:end + 1] + top, (8, CHUNK))
        for d, h in units:
            s_t = s_ts[d, h]
            inter = inters[d, h]
            num = w_inters[d, h] * inter[0:HEAD_DIM, :] + _dot(vts[d, h], s_t.astype(BF16))
            den = w_inters[d, h] * inter[HEAD_DIM:HEAD_DIM + 1, :] + jnp.sum(s_t, axis=0, keepdims=True)
            out_s = htf_s if d == 0 else htb_s
            out_s[chunk[d], h * HEAD_DIM:(h + 1) * HEAD_DIM, :] = num / jnp.maximum(jnp.abs(den), floors[d, h])
        for d, h in units:
            c_s[d * N_HEADS + h] = keeps[d, h] * cexts[d, h] + _dot(sources[d, h], ks[d, h])

    lax.fori_loop(0, N_CHUNKS, chunk_body, 0, unroll=2)

    gain = g_ref[...]

    def out_body(c, _):
        r0 = pl.multiple_of(c * CHUNK, CHUNK)
        h_t = htf_s[c] + htb_s[c]
        normed = []
        for h in range(N_HEADS):
            x = h_t[h * HEAD_DIM:(h + 1) * HEAD_DIM, :]
            normed.append(x * lax.rsqrt(jnp.mean(x * x, axis=0, keepdims=True) + NORM_EPS))
        y = jnp.concatenate(normed, axis=0).T * gain
        gate = 0.5 + 0.5 * jnp.tanh(0.5 * og_ref[0, pl.ds(r0, CHUNK), :])
        o_ref[0, pl.ds(r0, CHUNK), :] = gate * y
        return 0

    lax.fori_loop(0, N_CHUNKS, out_body, 0, unroll=2)


def _ml_mixer(qkv, og, sm, i_bias, f_bias, norm_g):
    b = qkv.shape[0]
    bias16 = jnp.concatenate([i_bias.reshape(8), f_bias.reshape(8)])
    bcol = jnp.tile(bias16.reshape(16, 1), (N_CHUNKS, 1))
    full = lambda shape: pl.BlockSpec(shape, lambda i: (0,) * len(shape))
    return pl.pallas_call(
        _ml_kernel,
        grid=(b,),
        in_specs=[pl.BlockSpec((1, SEQ, 3 * GROUP_W), lambda i: (i, 0, QKV_ML_BLOCK)),
                  pl.BlockSpec((1, SEQ, GROUP_W), lambda i: (i, 0, WIDE_MLO_BLOCK)),
                  pl.BlockSpec((1, SEQ, CHUNK), lambda i: (i, 0, WIDE_SM_BLOCK)),
                  full((N_CHUNKS * 16, 1)), full((1, GROUP_W))],
        out_specs=pl.BlockSpec((1, SEQ, GROUP_W), lambda i: (i, 0, 0)),
        out_shape=jax.ShapeDtypeStruct((b, SEQ, GROUP_W), F32),
        scratch_shapes=[pltpu.VMEM((SEQ, CHUNK), F32), pltpu.VMEM((N_CHUNKS, 16, CHUNK), F32),
                        pltpu.VMEM((N_CHUNKS, 16, CHUNK), F32), pltpu.VMEM((N_CHUNKS, GROUP_W, CHUNK), BF16),
                        pltpu.VMEM((N_CHUNKS, GROUP_W, CHUNK), BF16), pltpu.VMEM((N_CHUNKS, GROUP_W, CHUNK), F32),
                        pltpu.VMEM((N_CHUNKS, GROUP_W, CHUNK), F32),
                        pltpu.VMEM((2 * N_HEADS, ML_STATE_ROWS, HEAD_DIM), F32),
                        pltpu.VMEM((2 * N_HEADS, 8, CHUNK), F32)],
        compiler_params=_params("arbitrary"),
        name="mlstm_mixer",
    )(qkv, og, sm, bcol, norm_g.reshape(1, GROUP_W))


HALF = SEQ // 2


def _dft_tables():
    f = np.arange(HALF, dtype=np.int64)[:, None]
    t = np.arange(SEQ, dtype=np.int64)[None, :]
    ang = (2.0 * np.pi / FFT_N) * ((f * t) % FFT_N).astype(np.float64)
    wc = np.cos(ang)
    ws = np.sin(ang)
    ws[0, :] = np.array([1.0, 1.0, -1.0, -1.0])[np.arange(SEQ) % 4]
    tables = []
    for parity in range(2):
        w = np.stack([wc[:, parity::2], ws[:, parity::2]]).reshape(2, HALF // FREQ_SUB, FREQ_SUB, HALF)
        w = np.ascontiguousarray(w.transpose(1, 0, 2, 3).reshape(2 * HALF, HALF)).astype(np.float32)
        tables.append((jnp.asarray(w).astype(BF16), jnp.asarray(np.ascontiguousarray(w.T)).astype(BF16)))
    return tables


def _hy_features():
    t = jnp.linspace(0.0, 1.0, SEQ, dtype=F32)[:, None]
    bands = (HY_EMB - 1) // 2
    f = jnp.linspace(1e-4, bands - 1, bands, dtype=F32)
    ang = (2.0 * math.pi) * (jnp.arange(SEQ, dtype=F32) / SEQ)[:, None] * f[None, :]
    feats = jnp.concatenate([t, jnp.cos(ang), -jnp.sin(ang)], axis=-1)
    deltas = jnp.abs(jnp.linspace(math.log(1e-2) / 1.5, math.log(1e-2) / 0.3, GROUP_W, dtype=F32))
    return jnp.pad(feats, ((0, 0), (0, CHUNK - HY_EMB))), t, deltas.reshape(1, GROUP_W)


def _hy_filter_kernel(feats_ref, w1_ref, b1_ref, w2_ref, b2_ref, w3_ref, freq_ref, t_ref, dl_ref, hp_ref, hm_ref):
    h = jnp.sin(freq_ref[0:1, :] * (_dot(feats_ref[...].astype(BF16), w1_ref[...].astype(BF16)) + b1_ref[...]))
    h = jnp.sin(freq_ref[1:2, :] * (_dot(h.astype(BF16), w2_ref[...].astype(BF16)) + b2_ref[...]))
    h = _dot(h.astype(BF16), w3_ref[...].astype(BF16))
    win = jnp.exp(-t_ref[...] * dl_ref[...])
    row = lax.broadcasted_iota(jnp.int32, (SEQ, 1), 0)
    for o in range(2):
        hf = h[:, (2 * o) * GROUP_W:(2 * o + 1) * GROUP_W] * win
        hb = h[:, (2 * o + 1) * GROUP_W:(2 * o + 2) * GROUP_W] * win
        r = lax.rsqrt(jnp.sum(hf * hf + hb * hb, axis=0, keepdims=True) + NORM_EPS)
        hf = hf * r
        hb = jnp.where(row == 0, 0.0, hb * r)
        hp_ref[:, o * GROUP_W:(o + 1) * GROUP_W] = (hf + hb).astype(BF16)
        hm_ref[:, o * GROUP_W:(o + 1) * GROUP_W] = (hf - hb).astype(BF16)


def _hy_spec_kernel(we_ref, wo_ref, hp_ref, hm_ref, kc_ref, ks_ref, kcp_ref, ksp_ref):
    fb = FREQ_SUB
    nc = 2 * GROUP_W
    row0 = (lax.broadcasted_iota(jnp.int32, (fb, 1), 0) + pl.program_id(0) * fb) == 0
    edge = jnp.where(row0, 1.0 / FFT_N, 2.0 / FFT_N)
    pe = _dot(we_ref[0:fb, :], hp_ref[:, 0:nc])
    po = _dot(wo_ref[0:fb, :], hp_ref[:, nc:])
    me = _dot(we_ref[fb:, :], hm_ref[:, 0:nc])
    mo = _dot(wo_ref[fb:, :], hm_ref[:, nc:])
    mid_cos = _dot(we_ref[fb:fb + 8, :], hp_ref[:, 0:nc])[0:1, :]
    kc_ref[...] = (pe + po) * edge
    kcp_ref[...] = (pe - po) * edge
    ks_ref[...] = jnp.where(row0, mid_cos, me + mo) * (2.0 / FFT_N)
    ksp_ref[...] = jnp.where(row0, mo, mo - me) * (2.0 / FFT_N)


def _hy_spectrum(tables, w1, b1, w2, b2, w3, freq):
    feats, t, deltas = _hy_features()
    w1p = jnp.pad(w1, ((0, CHUNK - HY_EMB), (0, 0)))
    hp, hm = pl.pallas_call(
        _hy_filter_kernel,
        out_shape=[jax.ShapeDtypeStruct((SEQ, 2 * GROUP_W), BF16)] * 2,
        compiler_params=pltpu.CompilerParams(vmem_limit_bytes=VMEM_LIMIT),
        name="hyena_filter",
    )(feats, w1p, b1.reshape(1, HY_FFN), w2, b2.reshape(1, HY_FFN), w3, freq, t, deltas)
    split = lambda h: h.reshape(HALF, 4 * GROUP_W)
    table_spec = pl.BlockSpec((2 * FREQ_SUB, HALF), lambda j: (j, 0))
    return pl.pallas_call(
        _hy_spec_kernel,
        grid=(HALF // FREQ_SUB,),
        in_specs=[table_spec, table_spec,
                  pl.BlockSpec((HALF, 4 * GROUP_W), lambda j: (0, 0)),
                  pl.BlockSpec((HALF, 4 * GROUP_W), lambda j: (0, 0))],
        out_specs=[pl.BlockSpec((FREQ_SUB, 2 * GROUP_W), lambda j: (j, 0))] * 4,
        out_shape=[jax.ShapeDtypeStruct((HALF, 2 * GROUP_W), F32)] * 4,
        compiler_params=_params("arbitrary"),
        name="hyena_spectrum",
    )(tables[0][0], tables[1][0], split(hp), split(hm))


def _split_conv_rows(e_ref, o_ref, b, r0, w, bias):
    e = e_ref[b, pl.ds(r0, CHUNK), :]
    o = o_ref[b, pl.ds(r0, CHUNK), :]
    o_prev = o_ref[b, pl.ds(jnp.maximum(r0 - 1, 0), 1), :]
    e_next = e_ref[b, pl.ds(jnp.minimum(r0 + CHUNK, HALF - 1), 1), :]
    o_prev = jnp.where(r0 > 0, o_prev, 0.0)
    e_next = jnp.where(r0 + CHUNK < HALF, e_next, 0.0)
    row = lax.broadcasted_iota(jnp.int32, (CHUNK, 1), 0)
    o_dn = jnp.where(row == 0, o_prev, pltpu.roll(o, 1, 0))
    e_up = jnp.where(row == CHUNK - 1, e_next, pltpu.roll(e, CHUNK - 1, 0))
    conv_e = w[0:1, :] * o_dn + w[1:2, :] * e + w[2:3, :] * o + bias
    conv_o = w[0:1, :] * e + w[1:2, :] * o + w[2:3, :] * e_up + bias
    return conv_e, conv_o


def _hy_conv_kernel(ze_ref, zo_ref, ge_ref, go_ref, cwz_ref, cbz_ref, cwg_ref, cbg_ref, we_ref, wo_ref, wet_ref, wot_ref,
                    kc_ref, ks_ref, kcp_ref, ksp_ref, skip_ref, g_ref, o_ref, ze_s, zo_s, acce_s, acco_s, *,
                    conv_z, final_norm):
    j = pl.program_id(1)
    n_b = ze_ref.shape[0]
    fb = FREQ_BLOCK
    half_chunks = HALF // CHUNK

    def z_rows(b, r0):
        if conv_z:
            return _split_conv_rows(ze_ref, zo_ref, b, r0, cwz_ref[...], cbz_ref[...])
        return ze_ref[b, pl.ds(r0, CHUNK), :], zo_ref[b, pl.ds(r0, CHUNK), :]

    @pl.when(j == 0)
    def _():
        for b in range(n_b):
            def fill(c, _, b=b):
                r0 = pl.multiple_of(c * CHUNK, CHUNK)
                z_e, z_o = z_rows(b, r0)
                ze_s[pl.ds(r0, CHUNK), b * GROUP_W:(b + 1) * GROUP_W] = z_e.astype(BF16)
                zo_s[pl.ds(r0, CHUNK), b * GROUP_W:(b + 1) * GROUP_W] = z_o.astype(BF16)
                return 0

            lax.fori_loop(0, half_chunks, fill, 0)
        acce_s[...] = jnp.zeros_like(acce_s)
        acco_s[...] = jnp.zeros_like(acco_s)

    sub = FREQ_SUB
    groups = range(fb // sub)
    fwd = []
    for s in groups:
        rows = slice(2 * sub * s, 2 * sub * (s + 1))
        fwd.append((_dot(we_ref[rows, :], ze_s[...]), _dot(wo_ref[rows, :], zo_s[...])))
    inv_e, inv_o = [], []
    for s in groups:
        row0 = (lax.broadcasted_iota(jnp.int32, (sub, 1), 0) + j * fb + s * sub) == 0
        tiled = lambda ref: jnp.concatenate([ref[sub * s:sub * (s + 1), :]] * n_b, axis=1)
        kc, ks, kcp, ksp = tiled(kc_ref), tiled(ks_ref), tiled(kcp_ref), tiled(ksp_ref)
        xe, xo = fwd[s]
        ae, be, ao, bo = xe[0:sub, :], xe[sub:, :], xo[0:sub, :], xo[sub:, :]
        xc, xcp, xs, xsp = ae + ao, ae - ao, be + bo, bo - be
        yc = xc * kc - xs * ks
        ys = xc * ks + xs * kc
        ycp = xcp * kcp - xsp * ksp
        ysp = xcp * ksp + xsp * kcp
        dc, nyq = xc * kc, xcp * kcp
        pc = jnp.where(row0, dc + nyq, yc + ycp)
        mc = jnp.where(row0, dc - nyq, yc - ycp)
        ms = jnp.where(row0, be * ks - bo * ksp, ys - ysp)
        ps = jnp.where(row0, be * ksp + bo * ks, ys + ysp)
        cols = slice(2 * sub * s, 2 * sub * (s + 1))
        inv_e.append(_dot(wet_ref[:, cols], jnp.concatenate([pc, ms], axis=0).astype(BF16)))
        inv_o.append(_dot(wot_ref[:, cols], jnp.concatenate([mc, ps], axis=0).astype(BF16)))
    acce_s[...] += sum(inv_e)
    acco_s[...] += sum(inv_o)

    @pl.when(j == pl.num_programs(1) - 1)
    def _():
        for b in range(n_b):
            def finish(c, _, b=b):
                r0 = pl.multiple_of(c * CHUNK, CHUNK)
                gates = _split_conv_rows(ge_ref, go_ref, b, r0, cwg_ref[...], cbg_ref[...])
                for parity, (gate, z, acc) in enumerate(zip(gates, z_rows(b, r0), (acce_s, acco_s))):
                    r = gate * (acc[pl.ds(r0, CHUNK), b * GROUP_W:(b + 1) * GROUP_W] + skip_ref[...] * z)
                    if final_norm:
                        r = _rms(r) * g_ref[...]
                    o_ref[b, pl.ds(r0, CHUNK), parity * GROUP_W:(parity + 1) * GROUP_W] = r
                return 0

            lax.fori_loop(0, half_chunks, finish, 0)


def _hy_long_conv(z, z_cols, gates, gate_cols, conv_w, conv_b, conv_z, tables, spectrum, order, skip, norm_g,
                  final_norm):
    b = z.shape[0]
    bg = HY_BATCH_BLOCK
    taps = lambda col: (conv_w[:, col * GROUP_W:(col + 1) * GROUP_W],
                        conv_b[col * GROUP_W:(col + 1) * GROUP_W].reshape(1, GROUP_W))
    small = lambda shape: pl.BlockSpec(shape, lambda g, j: (0, 0))
    act = lambda col: pl.BlockSpec((bg, HALF, GROUP_W), lambda g, j: (g, 0, col))
    fwd = pl.BlockSpec((2 * FREQ_BLOCK, HALF), lambda g, j: (j, 0))
    inv = pl.BlockSpec((HALF, 2 * FREQ_BLOCK), lambda g, j: (0, j))
    spec = pl.BlockSpec((FREQ_BLOCK, GROUP_W), lambda g, j: (j, order))
    (we, wet), (wo, wot) = tables
    return pl.pallas_call(
        functools.partial(_hy_conv_kernel, conv_z=conv_z, final_norm=final_norm),
        grid=(b // bg, N_FREQ_BLOCKS),
        in_specs=[act(z_cols[0]), act(z_cols[1]), act(gate_cols[0]), act(gate_cols[1]),
                  small((3, GROUP_W)), small((1, GROUP_W)), small((3, GROUP_W)), small((1, GROUP_W)),
                  fwd, fwd, inv, inv, spec, spec, spec, spec, small((1, GROUP_W)), small((1, GROUP_W))],
        out_specs=pl.BlockSpec((bg, HALF, 2 * GROUP_W), lambda g, j: (g, 0, 0)),
        out_shape=jax.ShapeDtypeStruct((b, HALF, 2 * GROUP_W), F32),
        scratch_shapes=[pltpu.VMEM((HALF, bg * GROUP_W), BF16), pltpu.VMEM((HALF, bg * GROUP_W), BF16),
                        pltpu.VMEM((HALF, bg * GROUP_W), F32), pltpu.VMEM((HALF, bg * GROUP_W), F32)],
        compiler_params=_params("arbitrary", "arbitrary"),
        name="hyena_long_conv",
    )(z, z, gates, gates, *taps(0), *taps(order + 1), we, wo, wet, wot, *spectrum,
      skip.reshape(1, GROUP_W), norm_g.reshape(1, GROUP_W))


def _hy_mixer(u, tables, conv_w, conv_b, w1, b1, w2, b2, w3, freq, skip, norm_g):
    spectrum = _hy_spectrum(tables, w1, b1, w2, b2, w3, freq)
    z1 = _hy_long_conv(u, (0, 3), u, (1, 4), conv_w, conv_b, True, tables, spectrum, 0, skip[0], norm_g, False)
    return _hy_long_conv(z1, (0, 1), u, (2, 5), conv_w, conv_b, False, tables, spectrum, 1, skip[1], norm_g, True)


FFN_SPLITS = ((0, 1024), (1024, 2048), (2048, FFN_HIDDEN))


def _ffn_kernel(x_ref, mod_ref, ya_ref, yb_ref, yc_ref, yd_ref, perm_ref, wo_ref, wg_ref, wu_ref, wd_ref, gf_ref, o_ref,
                *, final_norm):
    yd = yd_ref[...]
    yd = jnp.concatenate([yd[:, 0:GROUP_W], yd[:, GROUP_W:]], axis=0).astype(BF16)
    yd = _dot(perm_ref[...], yd)
    y_mix = jnp.concatenate([ya_ref[...], yb_ref[...], yc_ref[...], yd], axis=1).astype(BF16)
    x = x_ref[...] + mod_ref[0, 2:3, :] * _dot(y_mix, wo_ref[...])
    h = (_rms(x) * (1.0 + mod_ref[0, 4:5, :]) + mod_ref[0, 3:4, :]).astype(BF16)
    acc = jnp.zeros_like(x)
    for lo, hi in FFN_SPLITS:
        a = _silu(_dot(h, wg_ref[:, lo:hi])) * _dot(h, wu_ref[:, lo:hi])
        acc = acc + _dot(a.astype(BF16), wd_ref[lo:hi, :])
    y = x + mod_ref[0, 5:6, :] * acc
    if final_norm:
        y = _rms(y) * gf_ref[...]
    o_ref[...] = y


def _out_ffn(x2, mod_l, ys, w_out, wg, wu, wd, layer, final_g, final_norm):
    n_tok, d = x2.shape
    tm = TOKEN_TILE
    tiles_per_seq = SEQ // tm
    tok = np.arange(tm)
    perm = (np.arange(tm)[None, :] == (tok // 2 + (tok % 2) * (tm // 2))[:, None]).astype(np.float32)
    resident = lambda shape: pl.BlockSpec((None,) + shape, lambda i: (layer, 0, 0), pipeline_mode=pl.Buffered(1))
    return pl.pallas_call(
        functools.partial(_ffn_kernel, final_norm=final_norm),
        grid=(n_tok // tm,),
        in_specs=[pl.BlockSpec((tm, d), lambda i: (i, 0)),
                  pl.BlockSpec((1, 6, d), lambda i: (i // tiles_per_seq, 0, 0))]
                 + [pl.BlockSpec((tm, GROUP_W), lambda i: (i, 0))] * 3
                 + [pl.BlockSpec((tm // 2, 2 * GROUP_W), lambda i: (i, 0)),
                    pl.BlockSpec((tm, tm), lambda i: (0, 0), pipeline_mode=pl.Buffered(1)),
                    resident((d, d)), resident((d, FFN_HIDDEN)), resident((d, FFN_HIDDEN)), resident((FFN_HIDDEN, d)),
                    pl.BlockSpec((1, d), lambda i: (0, 0))],
        out_specs=pl.BlockSpec((tm, d), lambda i: (i, 0)),
        out_shape=jax.ShapeDtypeStruct((n_tok, d), F32),
        compiler_params=_params("arbitrary"),
        name="out_proj_swiglu",
    )(x2, mod_l, *ys, jnp.asarray(perm).astype(BF16), w_out, wg, wu, wd, final_g.reshape(1, d))


def _permute_w_in(w):
    w = w.astype(BF16)
    cols = [w[..., 0:1024], w[..., 2568:2824], w[..., 1024:1032], w[..., 2824:2840],
            jnp.zeros(w.shape[:-1] + (P_PAD - 3608,), BF16), w[..., 1032:1800], w[..., 1800:2568], w[..., 2840:3608]]
    return jnp.concatenate(cols, axis=-1)


def kernel(x, c, mod_w, mod_b, w_in, ssd_conv_w, ssd_conv_b, ssd_dt_bias, ssd_a_log, ssd_d, ssd_norm_g, na_rpb, na_norm_g, ml_i_bias, ml_f_bias, ml_norm_g, hy_conv_w, hy_conv_b, hy_w1, hy_b1, hy_w2, hy_b2, hy_w3, hy_freq, hy_skip, hy_norm_g, w_out, ffn_w_gate, ffn_w_up, ffn_w_down, final_norm_g):
    b, seq, d = x.shape
    assert seq == SEQ and d == D_MODEL and b % HY_BATCH_BLOCK == 0
    depth = mod_w.shape[0]
    mod = _modulation(c, mod_w, mod_b).reshape(depth, b, 6, d)
    dft = _dft_tables()
    w_in_p = _permute_w_in(w_in)
    w_out_b, w_gate_b, w_up_b, w_down_b = (w.astype(BF16) for w in (w_out, ffn_w_gate, ffn_w_up, ffn_w_down))
    x2 = x.reshape(b * seq, d)
    for l in range(depth):
        wide, qkv, hy = _in_proj(x2, mod[l], w_in_p, l)
        wide = wide.reshape(b, seq, WIDE_W)
        qkv = qkv.reshape(b, seq, QKV_W)
        y_ssd = _ssd_mixer(wide, wide, ssd_conv_w[l], ssd_conv_b[l], ssd_dt_bias[l], ssd_a_log[l], ssd_d[l],
                           ssd_norm_g[l])
        y_na = _na_mixer(qkv, na_rpb[l], na_norm_g[l])
        y_ml = _ml_mixer(qkv, wide, wide, ml_i_bias[l], ml_f_bias[l], ml_norm_g[l])
        y_hy = _hy_mixer(hy.reshape(b, HALF, 2 * HY_W), dft, hy_conv_w[l], hy_conv_b[l], hy_w1[l], hy_b1[l], hy_w2[l],
                         hy_b2[l], hy_w3[l], hy_freq[l], hy_skip[l], hy_norm_g[l])
        ys = [t.reshape(b * seq, GROUP_W) for t in (y_ssd, y_na, y_ml)] + [y_hy.reshape(b * HALF, 2 * GROUP_W)]
        x2 = _out_ffn(x2, mod[l], ys, w_out_b, w_gate_b, w_up_b, w_down_b, l, final_norm_g, l == depth - 1)
    return x2.reshape(b, seq, d)
```
